```python
import math, functools
import jax, jax.numpy as jnp
from jax import lax
import numpy as np

D_MODEL = 4096
BATCH = 4
SEQ = 2048
DEPTH = 1
DEC_BATCH = 32
DEC_SEQ = 32
PAST_LEN = 2048

CHUNK = 64
Q_BLOCK = 128
MIX_WIDTH = D_MODEL
ATT_WIDTH = MIX_WIDTH // 2
HEAD_DIM = 128
N_ATT_HEADS = ATT_WIDTH // (2 * HEAD_DIM)
CONV_CHANNELS = MIX_WIDTH - ATT_WIDTH
CONV_WIDTH = 31
NUM_BUCKETS = 32
REL_MAX_DISTANCE = 128
N_EXPERTS = 256
TOP_K = 8
N_EXPERT_GROUPS = 8
TOPK_GROUPS = 4
EXPERTS_PER_GROUP = N_EXPERTS // N_EXPERT_GROUPS
EXPERT_FF = 512
SHARED_FF = 512
ROUTED_SCALE = 2.5
EXPERT_BLOCK = 128
RMS_EPS = 1e-6
SUBLN_EPS = 1e-5
LN_EPS = 1e-5
N_MOD = 6
IN_COLS = 3 * ATT_WIDTH + 2 * CONV_CHANNELS

kernel_name = 'hybrid_diffattn_conformer_moe_stream_step'


def rmsnorm(x, g, eps=RMS_EPS):
    xf = x.astype(jnp.float32)
    y = xf * lax.rsqrt(jnp.mean(xf * xf, axis=-1, keepdims=True) + eps)
    return (y * g.astype(jnp.float32)).astype(x.dtype)


def layernorm(x, g, b, eps=LN_EPS):
    xf = x.astype(jnp.float32)
    mu = jnp.mean(xf, axis=-1, keepdims=True)
    var = jnp.mean(jnp.square(xf - mu), axis=-1, keepdims=True)
    y = (xf - mu) * lax.rsqrt(var + eps)
    return (y * g.astype(jnp.float32) + b.astype(jnp.float32)).astype(x.dtype)


def rel_pos_bucket(rel):
    half = NUM_BUCKETS // 2
    max_exact = half // 2
    n = jnp.abs(rel)
    nf = jnp.maximum(n, 1).astype(jnp.float32)
    large = max_exact + (jnp.log(nf / max_exact) / math.log(REL_MAX_DISTANCE / max_exact)
                         * (half - max_exact)).astype(jnp.int32)
    large = jnp.minimum(large, half - 1)
    return jnp.where(rel > 0, half, 0) + jnp.where(n < max_exact, n, large)


def diff_attention(q, k, v, q_pos, k_pos, rel_bias, lam, lam_init, subln_w):
    logits = jnp.einsum('bqhmd,bkhmd->bhmqk', q, k).astype(jnp.float32) * (HEAD_DIM ** -0.5)
    bucket = rel_pos_bucket(k_pos[None, :] - q_pos[:, None])
    bias = jnp.transpose(rel_bias[bucket], (2, 3, 0, 1)).astype(jnp.float32)
    visible = (k_pos // CHUNK)[None, :] <= (q_pos // CHUNK)[:, None]
    probs = jax.nn.softmax(jnp.where(visible, logits + bias, -jnp.inf), axis=-1)
    attn = probs[:, :, 0] - lam * probs[:, :, 1]
    o = jnp.einsum('bhqk,bkhe->bqhe', attn.astype(v.dtype), v)
    o = rmsnorm(o, subln_w, SUBLN_EPS) * (1.0 - lam_init)
    return o.reshape(o.shape[0], o.shape[1], -1)


def routed_experts(h, idx, gates, w_gate, w_up, w_down):
    t, d = h.shape
    a = t * TOP_K
    flat_e = idx.reshape(a)
    flat_tok = jnp.repeat(jnp.arange(t, dtype=jnp.int32), TOP_K)
    flat_g = gates.reshape(a)
    order = jnp.argsort(flat_e)
    se, stok, sg = flat_e[order], flat_tok[order], flat_g[order]
    counts = jnp.bincount(flat_e, length=N_EXPERTS)
    starts = jnp.cumsum(counts) - counts
    padded = (counts + EXPERT_BLOCK - 1) // EXPERT_BLOCK * EXPERT_BLOCK
    pend = jnp.cumsum(padded)
    dest = (pend - padded)[se] + jnp.arange(a, dtype=jnp.int32) - starts[se]
    n_blocks = -(-a // EXPERT_BLOCK) + N_EXPERTS
    rows = n_blocks * EXPERT_BLOCK
    row_tok = jnp.full((rows,), t, jnp.int32).at[dest].set(stok)
    row_g = jnp.zeros((rows,), h.dtype).at[dest].set(sg)
    blk_start = jnp.arange(n_blocks, dtype=pend.dtype) * EXPERT_BLOCK
    blk_e = jnp.minimum(jnp.searchsorted(pend, blk_start, side='right'), N_EXPERTS - 1)
    h_pad = jnp.concatenate([h, jnp.zeros((1, d), h.dtype)], axis=0)

    def block_step(y, blk):
        toks, g, e = blk
        xb = h_pad[toks]
        out = (jax.nn.silu(xb @ w_gate[e]) * (xb @ w_up[e])) @ w_down[e]
        return y.at[toks].add((out * g[:, None]).astype(y.dtype)), None

    y, _ = lax.scan(block_step, jnp.zeros((t + 1, d), h.dtype),
                    (row_tok.reshape(n_blocks, EXPERT_BLOCK), row_g.reshape(n_blocks, EXPERT_BLOCK), blk_e))
    return y[:t]


def moe_ffn(h, p):
    t = h.shape[0]
    scores = jax.nn.sigmoid(jnp.dot(h.astype(jnp.float32), p['w_router'].astype(jnp.float32)))
    choice = scores + p['b_router_corr'].astype(jnp.float32)
    grp_score = jnp.sum(lax.top_k(choice.reshape(t, N_EXPERT_GROUPS, EXPERTS_PER_GROUP), 2)[0], axis=-1)
    _, grp_idx = lax.top_k(grp_score, TOPK_GROUPS)
    grp_keep = jnp.sum(jax.nn.one_hot(grp_idx, N_EXPERT_GROUPS), axis=1) > 0
    choice = jnp.where(jnp.repeat(grp_keep, EXPERTS_PER_GROUP, axis=1), choice, -jnp.inf)
    _, idx = lax.top_k(choice, TOP_K)
    g = jnp.take_along_axis(scores, idx, axis=1)
    g = g / jnp.sum(g, axis=-1, keepdims=True) * ROUTED_SCALE
    routed = routed_experts(h, idx, g.astype(h.dtype), p['w_exp_gate'], p['w_exp_up'], p['w_exp_down'])
    shared = (jax.nn.silu(h @ p['w_sh_gate']) * (h @ p['w_sh_up'])) @ p['w_sh_down']
    return routed + shared


def trunk_layer(x, c, past_k, past_v, conv_past, rel_bias, lam_init, p):
    b, t, d = x.shape
    f32 = jnp.float32
    mod = (jax.nn.silu(c) @ p['w_ada'] + p['b_ada'])[:, None, :]
    sh1, sc1, g1, sh2, sc2, g2 = jnp.split(mod, N_MOD, axis=-1)
    h = rmsnorm(x, p['g_pre_mix']) * (1 + sc1) + sh1
    u = h @ p['w_in']
    q, k, v, ga, gb = jnp.split(
        u, [ATT_WIDTH, 2 * ATT_WIDTH, 3 * ATT_WIDTH, 3 * ATT_WIDTH + CONV_CHANNELS], axis=-1)
    q = q.reshape(b, t, N_ATT_HEADS, 2, HEAD_DIM)
    k = k.reshape(b, t, N_ATT_HEADS, 2 * HEAD_DIM)
    v = v.reshape(b, t, N_ATT_HEADS, 2 * HEAD_DIM)
    if past_k is None:
        k_all, v_all = k, v
        conv_past = jnp.zeros((b, CONV_WIDTH - 1, CONV_CHANNELS), u.dtype)
    else:
        k_all = jnp.concatenate([past_k.astype(k.dtype), k], axis=1)
        v_all = jnp.concatenate([past_v.astype(v.dtype), v], axis=1)
        conv_past = conv_past.astype(u.dtype)
    tk = k_all.shape[1]
    k_pos = jnp.arange(tk, dtype=jnp.int32)
    q_pos = (tk - t) + jnp.arange(t, dtype=jnp.int32)

    lam = (jnp.exp(jnp.sum(p['lambda_q1'].astype(f32) * p['lambda_k1'].astype(f32)))
           - jnp.exp(jnp.sum(p['lambda_q2'].astype(f32) * p['lambda_k2'].astype(f32))) + lam_init)
    attend = functools.partial(diff_attention, k=k_all.reshape(b, tk, N_ATT_HEADS, 2, HEAD_DIM),
                               v=v_all, k_pos=k_pos, rel_bias=rel_bias, lam=lam,
                               lam_init=lam_init, subln_w=p['subln_w'])
    if t > Q_BLOCK:
        nb = t // Q_BLOCK
        qb = jnp.moveaxis(q.reshape(b, nb, Q_BLOCK, N_ATT_HEADS, 2, HEAD_DIM), 1, 0)
        pb = q_pos.reshape(nb, Q_BLOCK)
        ob = lax.map(lambda qa: attend(qa[0], q_pos=qa[1]), (qb, pb))
        attn_out = jnp.moveaxis(ob, 0, 1).reshape(b, t, ATT_WIDTH)
    else:
        attn_out = attend(q, q_pos=q_pos)

    glu = ga * jax.nn.sigmoid(gb)
    xpad = jnp.concatenate([conv_past, glu], axis=1)
    dw = lax.conv_general_dilated(xpad, p['conv_dw_w'][:, None, :].astype(xpad.dtype),
                                  window_strides=(1,), padding='VALID',
                                  dimension_numbers=('NWC', 'WIO', 'NWC'),
                                  feature_group_count=CONV_CHANNELS) + p['conv_dw_b']
    conv_out = jax.nn.silu(layernorm(dw, p['conv_ln_g'], p['conv_ln_b']))
    new_conv = xpad[:, -(CONV_WIDTH - 1):]

    mix = jnp.concatenate([attn_out, conv_out.astype(attn_out.dtype)], axis=-1) @ p['w_out']
    x = x + g1 * rmsnorm(mix, p['g_post_mix'])

    h2 = rmsnorm(x, p['g_pre_ffn']) * (1 + sc2) + sh2
    f = moe_ffn(h2.reshape(b * t, d), p).reshape(b, t, d)
    x = x + g2 * rmsnorm(f, p['g_post_ffn'])
    return x, k, v, new_conv


def setup_inputs(seed: int = 0) -> dict:
    key = jax.random.key(seed)
    ks = jax.random.split(key, 40)
    f32 = jnp.float32
    D = D_MODEL

    def nrm(k, shape, s):
        return s * jax.random.normal(k, shape, f32)

    kv_shape = (DEPTH, DEC_BATCH, PAST_LEN, N_ATT_HEADS, 2 * HEAD_DIM)
    return {
        'x_prompt': nrm(ks[0], (BATCH, SEQ, D), 1.0),
        'x_sample': nrm(ks[1], (DEC_BATCH, DEC_SEQ, D), 1.0),
        'c_prompt': nrm(ks[2], (BATCH, D), 1.0),
        'c_sample': nrm(ks[3], (DEC_BATCH, D), 1.0),
        'cache_k': nrm(ks[4], kv_shape, 1.0),
        'cache_v': nrm(ks[5], kv_shape, 1.0),
        'state_conv': nrm(ks[6], (DEPTH, DEC_BATCH, CONV_WIDTH - 1, CONV_CHANNELS), 0.5),
        'rel_bias': nrm(ks[7], (NUM_BUCKETS, N_ATT_HEADS, 2), 0.3),
        'w_ada': nrm(ks[8], (DEPTH, D, N_MOD * D), 0.5 * D ** -0.5),
        'b_ada': nrm(ks[9], (DEPTH, N_MOD * D), 0.01),
        'g_pre_mix': 1.0 + nrm(ks[10], (DEPTH, D), 0.05),
        'g_post_mix': 1.0 + nrm(ks[11], (DEPTH, D), 0.05),
        'g_pre_ffn': 1.0 + nrm(ks[12], (DEPTH, D), 0.05),
        'g_post_ffn': 1.0 + nrm(ks[13], (DEPTH, D), 0.05),
        'w_in': nrm(ks[14], (DEPTH, D, IN_COLS), D ** -0.5),
        'lambda_q1': nrm(ks[15], (DEPTH, HEAD_DIM), 0.1),
        'lambda_k1': nrm(ks[16], (DEPTH, HEAD_DIM), 0.1),
        'lambda_q2': nrm(ks[17], (DEPTH, HEAD_DIM), 0.1),
        'lambda_k2': nrm(ks[18], (DEPTH, HEAD_DIM), 0.1),
        'subln_w': 1.0 + nrm(ks[19], (DEPTH, 2 * HEAD_DIM), 0.05),
        'conv_dw_w': nrm(ks[20], (DEPTH, CONV_WIDTH, CONV_CHANNELS), CONV_WIDTH ** -0.5),
        'conv_dw_b': nrm(ks[21], (DEPTH, CONV_CHANNELS), 0.01),
        'conv_ln_g': 1.0 + nrm(ks[22], (DEPTH, CONV_CHANNELS), 0.05),
        'conv_ln_b': nrm(ks[23], (DEPTH, CONV_CHANNELS), 0.01),
        'w_out': nrm(ks[24], (DEPTH, MIX_WIDTH, D), MIX_WIDTH ** -0.5),
        'w_router': nrm(ks[25], (DEPTH, D, N_EXPERTS), D ** -0.5),
        'b_router_corr': nrm(ks[26], (DEPTH, N_EXPERTS), 0.01),
        'w_exp_gate': nrm(ks[27], (DEPTH, N_EXPERTS, D, EXPERT_FF), D ** -0.5),
        'w_exp_up': nrm(ks[28], (DEPTH, N_EXPERTS, D, EXPERT_FF), D ** -0.5),
        'w_exp_down': nrm(ks[29], (DEPTH, N_EXPERTS, EXPERT_FF, D), EXPERT_FF ** -0.5),
        'w_sh_gate': nrm(ks[30], (DEPTH, D, SHARED_FF), D ** -0.5),
        'w_sh_up': nrm(ks[31], (DEPTH, D, SHARED_FF), D ** -0.5),
        'w_sh_down': nrm(ks[32], (DEPTH, SHARED_FF, D), SHARED_FF ** -0.5),
    }


def reference(x_prompt, x_sample, c_prompt, c_sample, cache_k, cache_v, state_conv, rel_bias,
              w_ada, b_ada, g_pre_mix, g_post_mix, g_pre_ffn, g_post_ffn, w_in,
              lambda_q1, lambda_k1, lambda_q2, lambda_k2, subln_w,
              conv_dw_w, conv_dw_b, conv_ln_g, conv_ln_b, w_out,
              w_router, b_router_corr, w_exp_gate, w_exp_up, w_exp_down,
              w_sh_gate, w_sh_up, w_sh_down):
    xp, xs = x_prompt, x_sample
    kp_l, vp_l, cp_l, ks_l, vs_l, cs_l = [], [], [], [], [], []
    for l in range(DEPTH):
        p = dict(w_ada=w_ada[l], b_ada=b_ada[l], g_pre_mix=g_pre_mix[l], g_post_mix=g_post_mix[l],
                 g_pre_ffn=g_pre_ffn[l], g_post_ffn=g_post_ffn[l], w_in=w_in[l],
                 lambda_q1=lambda_q1[l], lambda_k1=lambda_k1[l], lambda_q2=lambda_q2[l],
                 lambda_k2=lambda_k2[l], subln_w=subln_w[l], conv_dw_w=conv_dw_w[l],
                 conv_dw_b=conv_dw_b[l], conv_ln_g=conv_ln_g[l], conv_ln_b=conv_ln_b[l],
                 w_out=w_out[l], w_router=w_router[l], b_router_corr=b_router_corr[l],
                 w_exp_gate=w_exp_gate[l], w_exp_up=w_exp_up[l], w_exp_down=w_exp_down[l],
                 w_sh_gate=w_sh_gate[l], w_sh_up=w_sh_up[l], w_sh_down=w_sh_down[l])
        lam_init = 0.8 - 0.6 * math.exp(-0.3 * l)
        xp, kp, vp, cp = trunk_layer(xp, c_prompt, None, None, None, rel_bias, lam_init, p)
        xs, ksm, vsm, csm = trunk_layer(xs, c_sample, cache_k[l], cache_v[l], state_conv[l],
                                        rel_bias, lam_init, p)
        kp_l.append(kp); vp_l.append(vp); cp_l.append(cp)
        ks_l.append(ksm); vs_l.append(vsm); cs_l.append(csm)
    y_prompt, y_sample = xp, xs
    new_k_prompt = jnp.stack(kp_l)
    new_v_prompt = jnp.stack(vp_l)
    new_conv_prompt = jnp.stack(cp_l)
    new_k_sample = jnp.stack(ks_l)
    new_v_sample = jnp.stack(vs_l)
    new_conv_sample = jnp.stack(cs_l)
    return (y_prompt, y_sample, new_k_prompt, new_v_prompt, new_conv_prompt,
            new_k_sample, new_v_sample, new_conv_sample)
```

```python
import functools
import math

import numpy as np
import jax
import jax.numpy as jnp
from jax import lax
from jax.experimental import pallas as pl
from jax.experimental.pallas import tpu as pltpu

F32 = jnp.float32
BF16 = jnp.bfloat16
I32 = jnp.int32

CHUNK = 64
HEAD_DIM = 128
NUM_BUCKETS = 32
REL_MAX_DISTANCE = 128
TOP_K = 8
N_EXPERT_GROUPS = 8
TOPK_GROUPS = 4
ROUTED_SCALE = 2.5
RMS_EPS = 1e-6
SUBLN_EPS = 1e-5
LN_EPS = 1e-5
N_MOD = 6
NEG = -1e30
ATT_SCALE = HEAD_DIM ** -0.5

VMEM_LIMIT_BYTES = 56 * 1024 * 1024
ATT_TILE = 256
SEG_ROWS = 512
EXP_KCHUNKS = 4
EXP_NCHUNKS = 4
COMBINE_ROWS = 64
DISPATCH_ROWS = 512
ROW_GROUPS = 4


def _params(*sem):
    return pltpu.CompilerParams(dimension_semantics=sem, vmem_limit_bytes=VMEM_LIMIT_BYTES)


def _silu(x):
    return x * jax.nn.sigmoid(x)


def _dot_nt(a, b):
    return lax.dot_general(a, b, (((1,), (1,)), ((), ())), preferred_element_type=F32)


def _ada_kernel(c_ref, w_ref, b_ref, o_ref):
    a = _silu(c_ref[...]).astype(BF16)
    o_ref[...] = jnp.dot(a, w_ref[...].astype(BF16), preferred_element_type=F32) + b_ref[...]


def _ada(c_all, w_ada, b_ada):
    rows, d = c_all.shape
    n = w_ada.shape[1]
    tn = min(512, n)
    return pl.pallas_call(
        _ada_kernel,
        grid=(n // tn,),
        in_specs=[pl.BlockSpec((rows, d), lambda j: (0, 0)),
                  pl.BlockSpec((d, tn), lambda j: (0, j)),
                  pl.BlockSpec((1, tn), lambda j: (0, j))],
        out_specs=pl.BlockSpec((rows, tn), lambda j: (0, j)),
        out_shape=jax.ShapeDtypeStruct((rows, n), F32),
        compiler_params=_params("arbitrary"),
        name="ada",
    )(c_all, w_ada, b_ada)


class _RowPlan:
    def __init__(self, n_groups, group_rows, d, nb, mod_nb, mod_index, out_block_offset=0):
        self.n_groups, self.group_rows, self.d, self.nb = n_groups, group_rows, d, nb
        self.mod_nb, self.mod_index, self.out_block_offset = mod_nb, mod_index, out_block_offset
        self.grid = (n_groups // nb,)

    def act(self, offset=0):
        return pl.BlockSpec((self.nb, self.group_rows, self.d), lambda i: (i + offset, 0, 0))

    def mod(self, chunk):
        return pl.BlockSpec((self.mod_nb, 1, self.d), lambda i: (self.mod_index(i), 0, chunk))

    def vec(self):
        return pl.BlockSpec((1, self.d), lambda i: (0, 0))


def _rms(x, g, eps):
    return x * lax.rsqrt(jnp.mean(x * x, axis=-1, keepdims=True) + eps) * g


def _prenorm_kernel(x_ref, g_ref, sh_ref, sc_ref, o_ref):
    y = _rms(x_ref[...], g_ref[...], RMS_EPS)
    o_ref[...] = (y * (1.0 + sc_ref[...]) + sh_ref[...]).astype(o_ref.dtype)


def _prenorm(plan, x3, g, mod3):
    return pl.pallas_call(
        _prenorm_kernel,
        grid=plan.grid,
        in_specs=[plan.act(), plan.vec(), plan.mod(0), plan.mod(1)],
        out_specs=plan.act(),
        out_shape=jax.ShapeDtypeStruct(x3.shape, BF16),
        compiler_params=_params("arbitrary"),
        name="prenorm",
    )(x3, g, mod3, mod3)


def _postmix_kernel(x_ref, mix_ref, gpost_ref, gpre_ref, g1_ref, sh2_ref, sc2_ref, *rest):
    x1_ref, h2_ref, h2b_ref = rest[-3:]
    x1 = x_ref[...] + g1_ref[...] * _rms(mix_ref[...], gpost_ref[...], RMS_EPS)
    x1_ref[...] = x1
    h2 = _rms(x1, gpre_ref[...], RMS_EPS) * (1.0 + sc2_ref[...]) + sh2_ref[...]
    h2_ref[...] = h2
    h2b_ref[...] = h2.astype(BF16)


def _postmix(plan, x3, mix3, gpost, gpre, mod3, total_groups, prev=None):
    off = plan.out_block_offset
    in_specs = [plan.act(), plan.act(), plan.vec(), plan.vec(), plan.mod(2), plan.mod(3), plan.mod(4)]
    args = [x3, mix3, gpost, gpre, mod3, mod3, mod3]
    aliases = {}
    if prev is not None:
        in_specs += [pl.BlockSpec(memory_space=pl.ANY), pl.BlockSpec(memory_space=pl.ANY)]
        args += list(prev)
        aliases = {7: 1, 8: 2}
    shape_all = (total_groups, plan.group_rows, plan.d)
    return pl.pallas_call(
        _postmix_kernel,
        grid=plan.grid,
        in_specs=in_specs,
        out_specs=[plan.act(), plan.act(off), plan.act(off)],
        out_shape=[jax.ShapeDtypeStruct(x3.shape, F32),
                   jax.ShapeDtypeStruct(shape_all, F32),
                   jax.ShapeDtypeStruct(shape_all, BF16)],
        input_output_aliases=aliases,
        compiler_params=_params("arbitrary"),
        name="postmix",
    )(*args)


def _final_kernel(x1_ref, f_ref, gpost_ref, g2_ref, o_ref):
    o_ref[...] = x1_ref[...] + g2_ref[...] * _rms(f_ref[...], gpost_ref[...], RMS_EPS)


def _final(plan, x13, f3, gpost, mod3):
    return pl.pallas_call(
        _final_kernel,
        grid=plan.grid,
        in_specs=[plan.act(), plan.act(plan.out_block_offset), plan.vec(), plan.mod(5)],
        out_specs=plan.act(),
        out_shape=jax.ShapeDtypeStruct(x13.shape, F32),
        compiler_params=_params("arbitrary"),
        name="final",
    )(x13, f3, gpost, mod3)


def _mm_kernel(*refs, n_a, n_w, k_sizes, epilogue):
    a_refs = refs[:n_a]
    w_refs = refs[n_a:n_a + n_w]
    o_ref = refs[n_a + n_w]
    wb_refs = refs[n_a + n_w + 1:]

    @pl.when(pl.program_id(1) == 0)
    def _():
        for w_ref, wb_ref in zip(w_refs, wb_refs):
            wb_ref[...] = w_ref[...].astype(BF16)

    parts = []
    for wb_ref in wb_refs:
        acc, k0 = None, 0
        for a_ref, ka in zip(a_refs, k_sizes):
            d = jnp.dot(a_ref[...], wb_ref[k0:k0 + ka, :], preferred_element_type=F32)
            acc = d if acc is None else acc + d
            k0 += ka
        parts.append(acc)
    o_ref[...] = epilogue(*parts).astype(o_ref.dtype)


def _mm(a_list, w_cols, n_out, epilogue, out_dtype, tm, tn, name):
    m = a_list[0].shape[0]
    k = w_cols[0][0].shape[0]
    tm, tn = math.gcd(tm, m), math.gcd(tn, n_out)
    k_sizes = tuple(a.shape[1] for a in a_list)
    assert sum(k_sizes) == k and m % tm == 0 and n_out % tn == 0 and all(c % tn == 0 for _, c in w_cols)
    a_specs = [pl.BlockSpec((tm, ka), lambda j, i: (i, 0)) for ka in k_sizes]
    w_specs = [pl.BlockSpec((k, tn), functools.partial(lambda j, i, o: (0, o + j), o=c // tn)) for _, c in w_cols]
    kern = functools.partial(_mm_kernel, n_a=len(a_list), n_w=len(w_cols), k_sizes=k_sizes, epilogue=epilogue)
    return pl.pallas_call(
        kern,
        grid=(n_out // tn, m // tm),
        in_specs=a_specs + w_specs,
        out_specs=pl.BlockSpec((tm, tn), lambda j, i: (i, j)),
        out_shape=jax.ShapeDtypeStruct((m, n_out), out_dtype),
        scratch_shapes=[pltpu.VMEM((k, tn), BF16) for _ in w_cols],
        compiler_params=_params("arbitrary", "arbitrary"),
        name=name,
    )(*a_list, *[w for w, _ in w_cols])


def _identity(x):
    return x


def _glu(a, b):
    return a * jax.nn.sigmoid(b)


def _swiglu(a, b):
    return _silu(a) * b


def _bucket(rel):
    half = NUM_BUCKETS // 2
    max_exact = half // 2
    n = np.abs(rel)
    nf = np.maximum(n, 1).astype(np.float32)
    large = max_exact + (np.log(nf / np.float32(max_exact)) / np.float32(math.log(REL_MAX_DISTANCE / max_exact))
                         * np.float32(half - max_exact)).astype(np.int32)
    large = np.minimum(large, half - 1)
    return np.where(rel > 0, half, 0) + np.where(n < max_exact, n, large)


def _bias_table(rel_bias, q_pos, k_pos):
    rel = k_pos[None, :] - q_pos[:, None]
    visible = (k_pos // CHUNK)[None, :] <= (q_pos // CHUNK)[:, None]
    b = jnp.transpose(rel_bias[jnp.asarray(_bucket(rel))], (2, 3, 0, 1)).astype(F32)
    return jnp.where(jnp.asarray(visible)[None, None], b, NEG)


def _lambda(lq1, lk1, lq2, lk2, lam_init):
    return (jnp.exp(jnp.sum(lq1[...] * lk1[...], keepdims=True))
            - jnp.exp(jnp.sum(lq2[...] * lk2[...], keepdims=True)) + lam_init)


def _softmax_step(s, v, m_ref, l_ref, acc_ref, idx):
    m_old = m_ref[idx]
    m_new = jnp.maximum(m_old, jnp.max(s, axis=-1, keepdims=True))
    alpha = jnp.exp(m_old - m_new)
    p = jnp.exp(s - m_new)
    l_ref[idx] = alpha * l_ref[idx] + jnp.sum(p, axis=-1, keepdims=True)
    acc_ref[idx] = alpha * acc_ref[idx] + jnp.dot(p.astype(BF16), v, preferred_element_type=F32)
    m_ref[idx] = m_new


def _attn_finish(m_ref, l_ref, acc_ref, lam, sw, lam_init, o_ref):
    o = acc_ref[0] / l_ref[0] - lam * (acc_ref[1] / l_ref[1])
    o = o * lax.rsqrt(jnp.mean(o * o, axis=-1, keepdims=True) + SUBLN_EPS) * sw * (1.0 - lam_init)
    o_ref[...] = o.astype(o_ref.dtype)


def _attn_prompt_kernel(q_ref, k_ref, v_ref, b_ref, lq1, lk1, lq2, lk2, sw_ref, o_ref,
                        kb_ref, vb_ref, m_ref, l_ref, acc_ref, *, lam_init):
    qi = pl.program_id(2)

    @pl.when(qi == 0)
    def _():
        kb_ref[...] = k_ref[...].astype(BF16)
        vb_ref[...] = v_ref[...].astype(BF16)

    q = q_ref[...]
    qm = (q[:, :HEAD_DIM].astype(BF16), q[:, HEAD_DIM:].astype(BF16))
    m_ref[...] = jnp.full(m_ref.shape, NEG, F32)
    l_ref[...] = jnp.zeros(l_ref.shape, F32)
    acc_ref[...] = jnp.zeros(acc_ref.shape, F32)

    def body(j, carry):
        off = pl.multiple_of(j * ATT_TILE, ATT_TILE)
        kt = kb_ref[pl.ds(off, ATT_TILE), :]
        vt = vb_ref[pl.ds(off, ATT_TILE), :]
        bi = jnp.minimum(qi - j, 2)
        for mp in range(2):
            s = _dot_nt(qm[mp], kt[:, mp * HEAD_DIM:(mp + 1) * HEAD_DIM]) * ATT_SCALE + b_ref[mp, bi]
            _softmax_step(s, vt, m_ref, l_ref, acc_ref, mp)
        return carry

    lax.fori_loop(0, qi + 1, body, 0)
    _attn_finish(m_ref, l_ref, acc_ref, _lambda(lq1, lk1, lq2, lk2, lam_init), sw_ref[...], lam_init, o_ref)


def _attn_prompt(q, k, v, rel_bias, lams, subln_w, batch, seq, heads, lam_init):
    t = ATT_TILE
    assert seq % t == 0 and t % CHUNK == 0
    nq = seq // t
    pos = np.arange(t)
    tiles = [_bias_table(rel_bias, pos + dt * t, pos) for dt in range(3)]
    assert np.all(_bucket(np.arange(-3 * t + 1, -2 * t + t)) == NUM_BUCKETS // 2 - 1)
    bias = jnp.stack(tiles, axis=2)
    hw = 2 * HEAD_DIM
    vec = lambda n: pl.BlockSpec((1, n), lambda b, h, i: (0, 0))
    return pl.pallas_call(
        functools.partial(_attn_prompt_kernel, lam_init=lam_init),
        grid=(batch, heads, nq),
        in_specs=[pl.BlockSpec((t, hw), lambda b, h, i: (b * nq + i, h)),
                  pl.BlockSpec((seq, hw), lambda b, h, i: (b, h)),
                  pl.BlockSpec((seq, hw), lambda b, h, i: (b, h)),
                  pl.BlockSpec((None, 2, 3, t, t), lambda b, h, i: (h, 0, 0, 0, 0)),
                  vec(HEAD_DIM), vec(HEAD_DIM), vec(HEAD_DIM), vec(HEAD_DIM), vec(hw)],
        out_specs=pl.BlockSpec((t, hw), lambda b, h, i: (b * nq + i, h)),
        out_shape=jax.ShapeDtypeStruct((batch * seq, heads * hw), BF16),
        scratch_shapes=[pltpu.VMEM((seq, hw), BF16), pltpu.VMEM((seq, hw), BF16),
                        pltpu.VMEM((2, t, 1), F32), pltpu.VMEM((2, t, 1), F32), pltpu.VMEM((2, t, hw), F32)],
        compiler_params=_params("arbitrary", "arbitrary", "arbitrary"),
        name="attn_prompt",
    )(q, k, v, bias, *lams, subln_w)


def _attn_sample_kernel(q_ref, ck_ref, cv_ref, kn_ref, vn_ref, b_ref, bn_ref, lq1, lk1, lq2, lk2, sw_ref, o_ref,
                        m_ref, l_ref, acc_ref, *, lam_init, n_tiles):
    kt_i = pl.program_id(1)

    @pl.when(kt_i == 0)
    def _():
        m_ref[...] = jnp.full(m_ref.shape, NEG, F32)
        l_ref[...] = jnp.zeros(l_ref.shape, F32)
        acc_ref[...] = jnp.zeros(acc_ref.shape, F32)

    def step(k2d, v2d, bias_of_map):
        kb = k2d.astype(BF16)
        vb = v2d.astype(BF16)
        for mp in range(2):
            qm = q_ref[mp].astype(BF16)
            s = _dot_nt(qm, kb[:, mp * HEAD_DIM:(mp + 1) * HEAD_DIM]) * ATT_SCALE + bias_of_map(mp)
            _softmax_step(s, vb, m_ref, l_ref, acc_ref, mp)

    rows = ck_ref.shape[0] * ck_ref.shape[1]
    bi = jnp.where(kt_i == n_tiles - 1, 1, 0)
    step(ck_ref[...].reshape(rows, ck_ref.shape[2]), cv_ref[...].reshape(rows, cv_ref.shape[2]),
         lambda mp: b_ref[bi, mp])

    @pl.when(kt_i == n_tiles - 1)
    def _():
        step(kn_ref[...], vn_ref[...], lambda mp: bn_ref[mp])
        sw = sw_ref[...]
        _attn_finish(m_ref, l_ref, acc_ref, _lambda(lq1, lk1, lq2, lk2, lam_init), sw, lam_init, o_ref)


def _head_expand(bias, heads):
    h, _, tq, tk = bias.shape
    eye = jnp.asarray(np.eye(heads, dtype=bool))
    full = jnp.where(eye[:, None, None, None, :], bias[..., None], NEG)
    return jnp.transpose(full, (1, 0, 2, 3, 4)).reshape(2, h * tq, tk * heads)


def _attn_sample(q, cache_k, cache_v, k_new, v_new, rel_bias, lams, subln_w, lam_init):
    db, past, heads, hw = cache_k.shape
    tq = q.shape[2] // heads
    t = min(ATT_TILE, past)
    assert past % t == 0 and t >= REL_MAX_DISTANCE + tq
    n_tiles = past // t
    q_pos = past + np.arange(tq)
    far = _bias_table(rel_bias, q_pos, np.arange(t))
    assert n_tiles == 1 or np.all(_bucket(np.arange(past - t)[None, :] - q_pos[:, None]) == NUM_BUCKETS // 2 - 1)
    near = _bias_table(rel_bias, q_pos, past - t + np.arange(t))
    bias = jnp.stack([_head_expand(far, heads), _head_expand(near, heads)])
    bias_new = _head_expand(_bias_table(rel_bias, q_pos, q_pos), heads)
    rq = heads * tq
    vec = lambda n: pl.BlockSpec((1, n), lambda b, j: (0, 0))
    return pl.pallas_call(
        functools.partial(_attn_sample_kernel, lam_init=lam_init, n_tiles=n_tiles),
        grid=(db, n_tiles),
        in_specs=[pl.BlockSpec((None, 2, rq, HEAD_DIM), lambda b, j: (b, 0, 0, 0)),
                  pl.BlockSpec((None, t, heads, hw), lambda b, j: (b, j, 0, 0)),
                  pl.BlockSpec((None, t, heads, hw), lambda b, j: (b, j, 0, 0)),
                  pl.BlockSpec((None, tq * heads, hw), lambda b, j: (b, 0, 0)),
                  pl.BlockSpec((None, tq * heads, hw), lambda b, j: (b, 0, 0)),
                  pl.BlockSpec((2, 2, rq, t * heads), lambda b, j: (0, 0, 0, 0)),
                  pl.BlockSpec((2, rq, tq * heads), lambda b, j: (0, 0, 0)),
                  vec(HEAD_DIM), vec(HEAD_DIM), vec(HEAD_DIM), vec(HEAD_DIM), vec(hw)],
        out_specs=pl.BlockSpec((None, rq, hw), lambda b, j: (b, 0, 0)),
        out_shape=jax.ShapeDtypeStruct((db, rq, hw), BF16),
        scratch_shapes=[pltpu.VMEM((2, rq, 1), F32), pltpu.VMEM((2, rq, 1), F32), pltpu.VMEM((2, rq, hw), F32)],
        compiler_params=_params("arbitrary", "arbitrary"),
        name="attn_sample",
    )(q, cache_k, cache_v, k_new, v_new, bias, bias_new, *lams, subln_w)


CONV_HIST = 32


def _conv_kernel(cur_ref, hist_ref, w_ref, b_ref, g_ref, beta_ref, o_ref, xp_ref, *, width, zero_first):
    nb, tt, c = cur_ref.shape
    hist = hist_ref[...]
    if zero_first:
        hist = jnp.where(pl.program_id(1) == 0, 0.0, hist)
    xp_ref[:, 0:CONV_HIST, :] = hist
    xp_ref[:, CONV_HIST:, :] = cur_ref[...]
    lead = CONV_HIST - (width - 1)
    rows = 8
    for n in range(nb):
        for r0 in range(0, tt, rows):
            acc = jnp.zeros((rows, c), F32) + b_ref[...]
            for tap in range(width):
                a = lead + r0 + tap
                acc = acc + xp_ref[n, a:a + rows, :] * w_ref[tap:tap + 1, :]
            mu = jnp.mean(acc, axis=-1, keepdims=True)
            cen = acc - mu
            var = jnp.mean(cen * cen, axis=-1, keepdims=True)
            y = cen * lax.rsqrt(var + LN_EPS) * g_ref[...] + beta_ref[...]
            o_ref[n, r0:r0 + rows, :] = _silu(y).astype(o_ref.dtype)


def _conv(cur3, hist3, hist_index, nb, tt, w, b, g, beta, zero_first):
    nseq, seq, c = cur3.shape
    width = w.shape[0]
    assert width - 1 <= CONV_HIST and seq % tt == 0 and nseq % nb == 0 and tt % 8 == 0
    vec = lambda: pl.BlockSpec((1, c), lambda s, i: (0, 0))
    return pl.pallas_call(
        functools.partial(_conv_kernel, width=width, zero_first=zero_first),
        grid=(nseq // nb, seq // tt),
        in_specs=[pl.BlockSpec((nb, tt, c), lambda s, i: (s, i, 0)),
                  pl.BlockSpec((nb, CONV_HIST, c), hist_index),
                  pl.BlockSpec((width, c), lambda s, i: (0, 0)),
                  vec(), vec(), vec()],
        out_specs=pl.BlockSpec((nb, tt, c), lambda s, i: (s, i, 0)),
        out_shape=jax.ShapeDtypeStruct(cur3.shape, BF16),
        scratch_shapes=[pltpu.VMEM((nb, CONV_HIST + tt, c), F32)],
        compiler_params=_params("arbitrary", "arbitrary"),
        name="conv",
    )(cur3, hist3, w, b, g, beta)


def _first_argmax(v, iota, axis, size):
    m = jnp.max(v, axis=axis, keepdims=True)
    i = jnp.min(jnp.where(v == m, iota, size), axis=axis, keepdims=True)
    return m, i


def _router_kernel(h_ref, w_ref, bias_ref, idx_ref, gate_ref, rank_ref, cnt_ref, carry_ref):
    n_exp = w_ref.shape[0]
    tm = h_ref.shape[0]
    per_group = n_exp // N_EXPERT_GROUPS

    @pl.when(pl.program_id(0) == 0)
    def _():
        carry_ref[...] = jnp.zeros(carry_ref.shape, F32)

    w = w_ref[...]
    w1 = w.astype(BF16)
    r1 = w - w1.astype(F32)
    w2 = r1.astype(BF16)
    w3 = (r1 - w2.astype(F32)).astype(BF16)
    h = h_ref[...]
    logits = _dot_nt(w1, h) + _dot_nt(w2, h) + _dot_nt(w3, h)
    scores = jax.nn.sigmoid(logits)
    choice = scores + bias_ref[...]

    ch3 = choice.reshape(N_EXPERT_GROUPS, per_group, tm)
    io3 = lax.broadcasted_iota(I32, ch3.shape, 1)
    m1, i1 = _first_argmax(ch3, io3, 1, per_group)
    m2 = jnp.max(jnp.where(io3 == i1, -jnp.inf, ch3), axis=1, keepdims=True)
    grp = (m1 + m2).reshape(N_EXPERT_GROUPS, tm)
    gio = lax.broadcasted_iota(I32, grp.shape, 0)
    keep = jnp.zeros(grp.shape, jnp.bool_)
    for _ in range(TOPK_GROUPS):
        _, gi = _first_argmax(grp, gio, 0, N_EXPERT_GROUPS)
        hit = gio == gi
        keep = jnp.logical_or(keep, hit)
        grp = jnp.where(hit, -jnp.inf, grp)
    keep3 = jnp.broadcast_to(keep.reshape(N_EXPERT_GROUPS, 1, tm), ch3.shape)
    masked = jnp.where(keep3, ch3, -jnp.inf).reshape(n_exp, tm)

    eio = lax.broadcasted_iota(I32, masked.shape, 0)
    sel = jnp.zeros(masked.shape, jnp.bool_)
    picks, pick_scores = [], []
    for _ in range(TOP_K):
        _, ei = _first_argmax(masked, eio, 0, n_exp)
        hit = eio == ei
        picks.append((ei, hit))
        pick_scores.append(jnp.sum(jnp.where(hit, scores, 0.0), axis=0, keepdims=True))
        sel = jnp.logical_or(sel, hit)
        masked = jnp.where(hit, -jnp.inf, masked)
    total = pick_scores[0]
    for s in pick_scores[1:]:
        total = total + s

    sel_b = jnp.where(sel, 1.0, 0.0).astype(BF16)
    tri = (lax.broadcasted_iota(I32, (tm, tm), 0) < lax.broadcasted_iota(I32, (tm, tm), 1))
    rank = jnp.dot(sel_b, jnp.where(tri, 1.0, 0.0).astype(BF16), preferred_element_type=F32) + carry_ref[...]
    for kk, (ei, hit) in enumerate(picks):
        idx_ref[kk:kk + 1, :] = ei
        gate_ref[kk:kk + 1, :] = pick_scores[kk] / total * ROUTED_SCALE
        rank_ref[kk:kk + 1, :] = jnp.sum(jnp.where(hit, rank, 0.0), axis=0, keepdims=True).astype(I32)
    carry_ref[...] = carry_ref[...] + jnp.sum(jnp.where(sel, 1.0, 0.0), axis=1, keepdims=True)
    cnt_ref[...] = carry_ref[...].astype(I32)


def _router(h2b, w_router_t, bias_col):
    t, d = h2b.shape
    n_exp = w_router_t.shape[0]
    tm = math.gcd(512, t)
    assert t % tm == 0
    tok = lambda: pl.BlockSpec((TOP_K, tm), lambda i: (0, i))
    return pl.pallas_call(
        _router_kernel,
        grid=(t // tm,),
        in_specs=[pl.BlockSpec((tm, d), lambda i: (i, 0)),
                  pl.BlockSpec((n_exp, d), lambda i: (0, 0)),
                  pl.BlockSpec((n_exp, 1), lambda i: (0, 0))],
        out_specs=[tok(), tok(), tok(), pl.BlockSpec((n_exp, 1), lambda i: (0, 0))],
        out_shape=[jax.ShapeDtypeStruct((TOP_K, t), I32), jax.ShapeDtypeStruct((TOP_K, t), F32),
                   jax.ShapeDtypeStruct((TOP_K, t), I32), jax.ShapeDtypeStruct((n_exp, 1), I32)],
        scratch_shapes=[pltpu.VMEM((n_exp, 1), F32)],
        compiler_params=_params("arbitrary"),
        name="router",
    )(h2b, w_router_t, bias_col)


def _dispatch_kernel(rows_ref, h_ref, xs_ref, sem):
    n_tok = rows_ref.shape[1]
    base = pl.program_id(0) * n_tok

    def body(t, carry):
        for kk in range(TOP_K):
            pltpu.make_async_copy(h_ref.at[pl.ds(base + t, 1)], xs_ref.at[pl.ds(rows_ref[kk, t], 1)], sem).start()
        return carry

    lax.fori_loop(0, n_tok, body, 0)
    pltpu.make_async_copy(xs_ref.at[pl.ds(0, n_tok * TOP_K)], xs_ref.at[pl.ds(0, n_tok * TOP_K)], sem).wait()


def _dispatch(rows, h2, n_rows):
    t, d = h2.shape
    nt = math.gcd(DISPATCH_ROWS, t)
    assert t % nt == 0 and n_rows >= nt * TOP_K
    rows = jnp.transpose(rows.reshape(TOP_K, t // nt, nt), (1, 0, 2))
    return pl.pallas_call(
        _dispatch_kernel,
        grid=(t // nt,),
        in_specs=[pl.BlockSpec((None, TOP_K, nt), lambda i: (i, 0, 0), memory_space=pltpu.SMEM),
                  pl.BlockSpec(memory_space=pl.ANY)],
        out_specs=pl.BlockSpec(memory_space=pl.ANY),
        out_shape=jax.ShapeDtypeStruct((n_rows, d), h2.dtype),
        scratch_shapes=[pltpu.SemaphoreType.DMA(())],
        compiler_params=pltpu.CompilerParams(dimension_semantics=("arbitrary",), has_side_effects=True),
        name="dispatch",
    )(rows, h2)


def _expert_kernel(seg_e_ref, seg_cnt_ref, nseg_ref, x_ref, wg_ref, wu_ref, wd_ref, y_ref,
                   gacc_ref, uacc_ref, hid_ref):
    s = pl.program_id(0)
    c = pl.program_id(1)
    active = s < nseg_ref[0]

    @pl.when(jnp.logical_and(active, c < EXP_KCHUNKS))
    def _():
        rows = lax.broadcasted_iota(I32, (x_ref.shape[0], 1), 0)
        x = jnp.where(rows < seg_cnt_ref[s], x_ref[...], 0.0).astype(BF16)
        g = jnp.dot(x, wg_ref[...].astype(BF16), preferred_element_type=F32)
        u = jnp.dot(x, wu_ref[...].astype(BF16), preferred_element_type=F32)

        @pl.when(c == 0)
        def _():
            gacc_ref[...] = g
            uacc_ref[...] = u

        @pl.when(c > 0)
        def _():
            gacc_ref[...] += g
            uacc_ref[...] += u

        @pl.when(c == EXP_KCHUNKS - 1)
        def _():
            hid_ref[...] = (_silu(gacc_ref[...]) * uacc_ref[...]).astype(BF16)

    @pl.when(jnp.logical_and(active, c >= EXP_KCHUNKS))
    def _():
        y_ref[...] = jnp.dot(hid_ref[...], wd_ref[...].astype(BF16), preferred_element_type=F32)


def _experts(seg_e, seg_cnt, nseg, xs, w_gate, w_up, w_down, n_seg_max):
    n_exp, d, ff = w_gate.shape
    kc, nc = EXP_KCHUNKS, EXP_NCHUNKS
    dk, dn = d // kc, d // nc
    r = SEG_ROWS

    def seg(s, c, nseg_ref):
        on = s < nseg_ref[0]
        return jnp.where(on, s, nseg_ref[0] - 1), jnp.where(on, c, kc + nc - 1)

    def x_map(s, c, se, sc, ns):
        s2, c2 = seg(s, c, ns)
        return s2, jnp.minimum(c2, kc - 1)

    def w_in_map(s, c, se, sc, ns):
        s2, c2 = seg(s, c, ns)
        return se[s2], jnp.minimum(c2, kc - 1), 0

    def w_out_map(s, c, se, sc, ns):
        s2, c2 = seg(s, c, ns)
        return se[s2], 0, jnp.maximum(c2 - kc, 0)

    def y_map(s, c, se, sc, ns):
        s2, c2 = seg(s, c, ns)
        return s2, jnp.maximum(c2 - kc, 0)

    grid_spec = pltpu.PrefetchScalarGridSpec(
        num_scalar_prefetch=3,
        grid=(n_seg_max, kc + nc),
        in_specs=[pl.BlockSpec((r, dk), x_map),
                  pl.BlockSpec((None, dk, ff), w_in_map),
                  pl.BlockSpec((None, dk, ff), w_in_map),
                  pl.BlockSpec((None, ff, dn), w_out_map)],
        out_specs=pl.BlockSpec((r, dn), y_map),
        scratch_shapes=[pltpu.VMEM((r, ff), F32), pltpu.VMEM((r, ff), F32), pltpu.VMEM((r, ff), BF16)],
    )
    return pl.pallas_call(
        _expert_kernel,
        grid_spec=grid_spec,
        out_shape=jax.ShapeDtypeStruct(xs.shape, F32),
        compiler_params=_params("arbitrary", "arbitrary"),
        name="experts",
    )(seg_e, seg_cnt, nseg, xs, w_gate, w_up, w_down)


def _combine_kernel(rows_ref, gate_ref, hs_ref, wsd_ref, ys_ref, f_ref, buf_ref, wb_ref, sem):
    n_tok = gate_ref.shape[0]

    @pl.when(pl.program_id(0) == 0)
    def _():
        wb_ref[...] = wsd_ref[...].astype(BF16)

    def body(t, carry):
        for kk in range(TOP_K):
            pltpu.make_async_copy(ys_ref.at[pl.ds(rows_ref[kk, t], 1)], buf_ref.at[kk, pl.ds(t, 1)], sem).start()
        return carry

    lax.fori_loop(0, n_tok, body, 0)
    shared = jnp.dot(hs_ref[...], wb_ref[...], preferred_element_type=F32)
    for kk in range(TOP_K):
        pltpu.make_async_copy(ys_ref.at[pl.ds(0, n_tok)], buf_ref.at[kk], sem).wait()
    gates = gate_ref[...]
    acc = shared
    for kk in range(TOP_K):
        acc = acc + buf_ref[kk] * gates[:, kk:kk + 1]
    f_ref[...] = acc


def _combine(rows, gates_t, hid_sh, w_sh_down, ys):
    t, ff = hid_sh.shape
    d = w_sh_down.shape[1]
    nt = math.gcd(COMBINE_ROWS, t)
    assert t % nt == 0
    rows = jnp.transpose(rows.reshape(TOP_K, t // nt, nt), (1, 0, 2))
    return pl.pallas_call(
        _combine_kernel,
        grid=(t // nt,),
        in_specs=[pl.BlockSpec((None, TOP_K, nt), lambda i: (i, 0, 0), memory_space=pltpu.SMEM),
                  pl.BlockSpec((nt, TOP_K), lambda i: (i, 0)),
                  pl.BlockSpec((nt, ff), lambda i: (i, 0)),
                  pl.BlockSpec((ff, d), lambda i: (0, 0)),
                  pl.BlockSpec(memory_space=pl.ANY)],
        out_specs=pl.BlockSpec((nt, d), lambda i: (i, 0)),
        out_shape=jax.ShapeDtypeStruct((t, d), F32),
        scratch_shapes=[pltpu.VMEM((TOP_K, nt, d), F32), pltpu.VMEM((ff, d), BF16), pltpu.SemaphoreType.DMA(())],
        compiler_params=_params("arbitrary"),
        name="combine",
    )(rows, gates_t, hid_sh, w_sh_down, ys)


def _layer(l, lam_init, x_prompt, x_sample, c_prompt, c_sample, cache_k, cache_v, state_conv, rel_bias, p):
    batch, seq, d = x_prompt.shape
    db, dseq, _ = x_sample.shape
    past, heads = cache_k.shape[1], cache_k.shape[2]
    hw = 2 * HEAD_DIM
    aw = heads * hw
    cc = p["conv_dw_w"].shape[1]
    width = p["conv_dw_w"].shape[0]
    assert p["w_in"].shape[1] == 3 * aw + 2 * cc and seq % dseq == 0
    tp, ts = batch * seq, db * dseq
    row = lambda v: v.reshape(1, -1)

    n_mod_rows = -(-(db + batch) // 16) * 16
    c_all = jnp.concatenate([c_sample, c_prompt, jnp.zeros((n_mod_rows - db - batch, d), F32)], axis=0)
    mod3 = _ada(c_all, p["w_ada"], row(p["b_ada"])).reshape(n_mod_rows, 1, N_MOD * d)

    gp = tp // dseq
    gps = seq // dseq
    nb_p = math.gcd(ROW_GROUPS, gps)
    nb_s = math.gcd(ROW_GROUPS, db)
    assert gps % nb_p == 0 and db % nb_s == 0 and gp % nb_s == 0
    plan_p = _RowPlan(gp, dseq, d, nb_p, 1, lambda i: db + (i * nb_p) // gps, 0)
    plan_s = _RowPlan(db, dseq, d, nb_s, nb_s, lambda i: i, gp // nb_s)
    xp3 = x_prompt.reshape(gp, dseq, d)

    hp = _prenorm(plan_p, xp3, row(p["g_pre_mix"]), mod3).reshape(tp, d)
    hs = _prenorm(plan_s, x_sample, row(p["g_pre_mix"]), mod3).reshape(ts, d)

    w_in = p["w_in"]
    proj = lambda h, off, nm: _mm([h], [(w_in, off)], aw, _identity, F32, 512, 512, nm)
    qp, kp, vp = proj(hp, 0, "q_prompt"), proj(hp, aw, "k_prompt"), proj(hp, 2 * aw, "v_prompt")
    qs, ks, vs = proj(hs, 0, "q_sample"), proj(hs, aw, "k_sample"), proj(hs, 2 * aw, "v_sample")
    glu_cols = [(w_in, 3 * aw), (w_in, 3 * aw + cc)]
    glu_p = _mm([hp], glu_cols, cc, _glu, F32, 512, 256, "glu_prompt")
    glu_s = _mm([hs], glu_cols, cc, _glu, F32, 512, 256, "glu_sample")

    lams = [row(p[n]) for n in ("lambda_q1", "lambda_k1", "lambda_q2", "lambda_k2")]
    subln = row(p["subln_w"])
    attn_p = _attn_prompt(qp, kp, vp, rel_bias, lams, subln, batch, seq, heads, lam_init)

    new_k_s = ks.reshape(db, dseq, heads, hw)
    new_v_s = vs.reshape(db, dseq, heads, hw)
    q_s = jnp.transpose(qs.reshape(db, dseq, heads, 2, HEAD_DIM), (0, 3, 2, 1, 4)).reshape(db, 2, heads * dseq, HEAD_DIM)
    o_s = _attn_sample(q_s, cache_k, cache_v, new_k_s.reshape(db, dseq * heads, hw),
                       new_v_s.reshape(db, dseq * heads, hw), rel_bias, lams, subln, lam_init)
    attn_s = jnp.transpose(o_s.reshape(db, heads, dseq, hw), (0, 2, 1, 3)).reshape(ts, aw)

    conv_args = (p["conv_dw_w"], row(p["conv_dw_b"]), row(p["conv_ln_g"]), row(p["conv_ln_b"]))
    glu_p3 = glu_p.reshape(batch, seq, cc)
    tt = min(128, seq)
    per = tt // CONV_HIST
    conv_p = _conv(glu_p3, glu_p3, lambda s, i: (s, jnp.maximum(i * per - 1, 0), 0), 1, tt, *conv_args,
                   zero_first=True).reshape(tp, cc)
    glu_s3 = glu_s.reshape(db, dseq, cc)
    hist_s = jnp.concatenate([jnp.zeros((db, CONV_HIST - (width - 1), cc), F32), state_conv], axis=1)
    nb_c = min(4, db)
    conv_s = _conv(glu_s3, hist_s, lambda s, i: (s, 0, 0), nb_c, dseq, *conv_args, zero_first=False).reshape(ts, cc)
    new_conv_p = glu_p3[:, seq - (width - 1):]
    new_conv_s = jnp.concatenate([state_conv, glu_s3], axis=1)[:, -(width - 1):]

    mix_p = _mm([attn_p, conv_p], [(p["w_out"], 0)], d, _identity, F32, 512, 512, "out_prompt")
    mix_s = _mm([attn_s, conv_s], [(p["w_out"], 0)], d, _identity, F32, 512, 512, "out_sample")

    gt = gp + db
    gpost, gpre = row(p["g_post_mix"]), row(p["g_pre_ffn"])
    x1p, h2a, h2ba = _postmix(plan_p, xp3, mix_p.reshape(gp, dseq, d), gpost, gpre, mod3, gt)
    x1s, h2a, h2ba = _postmix(plan_s, x_sample, mix_s.reshape(db, dseq, d), gpost, gpre, mod3, gt, prev=(h2a, h2ba))
    t = tp + ts
    h2 = h2a.reshape(t, d)
    h2b = h2ba.reshape(t, d)

    n_exp = p["w_router"].shape[1]
    idx, gates, rank, counts = _router(h2b, p["w_router"].T, p["b_router_corr"].reshape(n_exp, 1))
    counts = counts.reshape(n_exp)
    n_seg_e = (counts + SEG_ROWS - 1) // SEG_ROWS
    seg_end = jnp.cumsum(n_seg_e)
    seg_start = seg_end - n_seg_e
    n_seg_max = (t * TOP_K) // SEG_ROWS + n_exp
    rows = seg_start[idx] * SEG_ROWS + rank
    sid = jnp.arange(n_seg_max, dtype=I32)
    seg_e = jnp.minimum(jnp.searchsorted(seg_end, sid, side="right"), n_exp - 1).astype(I32)
    seg_cnt = jnp.clip(counts[seg_e] - (sid - seg_start[seg_e]) * SEG_ROWS, 0, SEG_ROWS).astype(I32)
    nseg = seg_end[-1:].astype(I32)

    xs = _dispatch(rows.astype(I32), h2, n_seg_max * SEG_ROWS)
    ys = _experts(seg_e, seg_cnt, nseg, xs, p["w_exp_gate"], p["w_exp_up"], p["w_exp_down"], n_seg_max)
    ff_sh = p["w_sh_gate"].shape[1]
    hid_sh = _mm([h2b], [(p["w_sh_gate"], 0), (p["w_sh_up"], 0)], ff_sh, _swiglu, BF16, 512, 256, "shared_up")
    f = _combine(rows.astype(I32), gates.T, hid_sh, p["w_sh_down"], ys)

    f3 = f.reshape(gt, dseq, d)
    gpf = row(p["g_post_ffn"])
    yp = _final(plan_p, x1p, f3, gpf, mod3).reshape(batch, seq, d)
    ysmp = _final(plan_s, x1s, f3, gpf, mod3)
    new_k_p = kp.reshape(batch, seq, heads, hw)
    new_v_p = vp.reshape(batch, seq, heads, hw)
    return yp, ysmp, new_k_p, new_v_p, new_conv_p, new_k_s, new_v_s, new_conv_s


def kernel(x_prompt, x_sample, c_prompt, c_sample, cache_k, cache_v, state_conv, rel_bias, w_ada, b_ada, g_pre_mix, g_post_mix, g_pre_ffn, g_post_ffn, w_in, lambda_q1, lambda_k1, lambda_q2, lambda_k2, subln_w, conv_dw_w, conv_dw_b, conv_ln_g, conv_ln_b, w_out, w_router, b_router_corr, w_exp_gate, w_exp_up, w_exp_down, w_sh_gate, w_sh_up, w_sh_down):
    weights = dict(w_ada=w_ada, b_ada=b_ada, g_pre_mix=g_pre_mix, g_post_mix=g_post_mix, g_pre_ffn=g_pre_ffn,
                   g_post_ffn=g_post_ffn, w_in=w_in, lambda_q1=lambda_q1, lambda_k1=lambda_k1,
                   lambda_q2=lambda_q2, lambda_k2=lambda_k2, subln_w=subln_w, conv_dw_w=conv_dw_w,
                   conv_dw_b=conv_dw_b, conv_ln_g=conv_ln_g, conv_ln_b=conv_ln_b, w_out=w_out,
                   w_router=w_router, b_router_corr=b_router_corr, w_exp_gate=w_exp_gate, w_exp_up=w_exp_up,
                   w_exp_down=w_exp_down, w_sh_gate=w_sh_gate, w_sh_up=w_sh_up, w_sh_down=w_sh_down)
    depth = w_in.shape[0]
    xp, xs = x_prompt, x_sample
    outs = [[] for _ in range(6)]
    for l in range(depth):
        p = {k: (v.reshape(v.shape[1:]) if depth == 1 else v[l]) for k, v in weights.items()}
        lam_init = 0.8 - 0.6 * math.exp(-0.3 * l)
        ck, cv, sc = ((a.reshape(a.shape[1:]) if depth == 1 else a[l]) for a in (cache_k, cache_v, state_conv))
        xp, xs, *state = _layer(l, lam_init, xp, xs, c_prompt, c_sample, ck, cv, sc, rel_bias, p)
        for acc, s in zip(outs, state):
            acc.append(s)
    return (xp, xs) + tuple(jnp.stack(o) for o in outs)
```

```python
import functools
import math

import numpy as np
import jax
import jax.numpy as jnp
from jax import lax
from jax.experimental import pallas as pl
from jax.experimental.pallas import tpu as pltpu

F32 = jnp.float32
BF16 = jnp.bfloat16
I32 = jnp.int32
U32 = jnp.uint32

CHUNK = 64
HEAD_DIM = 128
NUM_BUCKETS = 32
REL_MAX_DISTANCE = 128
TOP_K = 8
N_EXPERT_GROUPS = 8
TOPK_GROUPS = 4
ROUTED_SCALE = 2.5
RMS_EPS = 1e-6
SUBLN_EPS = 1e-5
LN_EPS = 1e-5
N_MOD = 6
NEG = -1e30
ATT_SCALE = HEAD_DIM ** -0.5

VMEM_LIMIT_BYTES = 56 * 1024 * 1024
ATT_TILE = 256
SEG_ROWS = 384
EXP_KCHUNKS = 2
EXP_NCHUNKS = 2
COMBINE_ROWS = 64
DISPATCH_ROWS = 256
ROW_GROUPS = 4


def _params(*sem):
    return pltpu.CompilerParams(dimension_semantics=sem, vmem_limit_bytes=VMEM_LIMIT_BYTES)


def _silu(x):
    return x * jax.nn.sigmoid(x)


def _dot_nt(a, b):
    return lax.dot_general(a, b, (((1,), (1,)), ((), ())), preferred_element_type=F32)


def _ada_kernel(c_ref, w_ref, b_ref, o_ref):
    a = _silu(c_ref[...]).astype(BF16)
    o_ref[...] = jnp.dot(a, w_ref[...].astype(BF16), preferred_element_type=F32) + b_ref[...]


def _ada(c_all, w_ada, b_ada):
    rows, d = c_all.shape
    n = w_ada.shape[1]
    tn = min(512, n)
    return pl.pallas_call(
        _ada_kernel,
        grid=(n // tn,),
        in_specs=[pl.BlockSpec((rows, d), lambda j: (0, 0)),
                  pl.BlockSpec((d, tn), lambda j: (0, j)),
                  pl.BlockSpec((1, tn), lambda j: (0, j))],
        out_specs=pl.BlockSpec((rows, tn), lambda j: (0, j)),
        out_shape=jax.ShapeDtypeStruct((rows, n), F32),
        compiler_params=_params("arbitrary"),
        name="ada",
    )(c_all, w_ada, b_ada)


class _RowPlan:
    def __init__(self, n_groups, group_rows, d, nb, mod_nb, mod_index, out_block_offset=0):
        self.n_groups, self.group_rows, self.d, self.nb = n_groups, group_rows, d, nb
        self.mod_nb, self.mod_index, self.out_block_offset = mod_nb, mod_index, out_block_offset
        self.grid = (n_groups // nb,)

    def act(self, offset=0):
        return pl.BlockSpec((self.nb, self.group_rows, self.d), lambda i: (i + offset, 0, 0))

    def mod(self, chunk):
        return pl.BlockSpec((self.mod_nb, 1, self.d), lambda i: (self.mod_index(i), 0, chunk))

    def vec(self):
        return pl.BlockSpec((1, self.d), lambda i: (0, 0))


def _rms(x, g, eps):
    return x * lax.rsqrt(jnp.mean(x * x, axis=-1, keepdims=True) + eps) * g


def _prenorm_kernel(x_ref, g_ref, sh_ref, sc_ref, o_ref):
    y = _rms(x_ref[...], g_ref[...], RMS_EPS)
    o_ref[...] = (y * (1.0 + sc_ref[...]) + sh_ref[...]).astype(o_ref.dtype)


def _prenorm(plan, x3, g, mod3):
    return pl.pallas_call(
        _prenorm_kernel,
        grid=plan.grid,
        in_specs=[plan.act(), plan.vec(), plan.mod(0), plan.mod(1)],
        out_specs=plan.act(),
        out_shape=jax.ShapeDtypeStruct(x3.shape, BF16),
        compiler_params=_params("arbitrary"),
        name="prenorm",
    )(x3, g, mod3, mod3)


def _pack_halves(x):
    n = x.shape[-1] // 2
    lo = lax.bitcast_convert_type(x[..., :n].astype(BF16).astype(F32), U32) >> 16
    hi = lax.bitcast_convert_type(x[..., n:].astype(BF16).astype(F32), U32) & jnp.uint32(0xFFFF0000)
    return hi | lo


def _unpack_halves(w):
    lo = lax.bitcast_convert_type(w << 16, F32).astype(BF16)
    hi = lax.bitcast_convert_type(w & jnp.uint32(0xFFFF0000), F32).astype(BF16)
    return lo, hi


def _postmix_kernel(x_ref, mix_ref, gpost_ref, gpre_ref, g1_ref, sh2_ref, sc2_ref, *rest):
    x1_ref, h2b_ref, h2p_ref = rest[-3:]
    x1 = x_ref[...] + g1_ref[...] * _rms(mix_ref[...], gpost_ref[...], RMS_EPS)
    x1_ref[...] = x1
    h2 = _rms(x1, gpre_ref[...], RMS_EPS) * (1.0 + sc2_ref[...]) + sh2_ref[...]
    h2b_ref[...] = h2.astype(BF16)
    dk = h2.shape[-1] // EXP_KCHUNKS
    for c in range(EXP_KCHUNKS):
        h2p_ref[:, :, c * (dk // 2):(c + 1) * (dk // 2)] = _pack_halves(h2[:, :, c * dk:(c + 1) * dk])


def _postmix(plan, x3, mix3, gpost, gpre, mod3, total_groups, prev=None):
    off = plan.out_block_offset
    in_specs = [plan.act(), plan.act(), plan.vec(), plan.vec(), plan.mod(2), plan.mod(3), plan.mod(4)]
    args = [x3, mix3, gpost, gpre, mod3, mod3, mod3]
    aliases = {}
    if prev is not None:
        in_specs += [pl.BlockSpec(memory_space=pl.ANY), pl.BlockSpec(memory_space=pl.ANY)]
        args += list(prev)
        aliases = {7: 1, 8: 2}
    shape_all = (total_groups, plan.group_rows, plan.d)
    shape_packed = (total_groups, plan.group_rows, plan.d // 2)
    packed_spec = pl.BlockSpec((plan.nb, plan.group_rows, plan.d // 2), lambda i: (i + off, 0, 0))
    return pl.pallas_call(
        _postmix_kernel,
        grid=plan.grid,
        in_specs=in_specs,
        out_specs=[plan.act(), plan.act(off), packed_spec],
        out_shape=[jax.ShapeDtypeStruct(x3.shape, F32),
                   jax.ShapeDtypeStruct(shape_all, BF16),
                   jax.ShapeDtypeStruct(shape_packed, U32)],
        input_output_aliases=aliases,
        compiler_params=_params("arbitrary"),
        name="postmix",
    )(*args)


def _final_kernel(x1_ref, f_ref, gpost_ref, g2_ref, o_ref):
    o_ref[...] = x1_ref[...] + g2_ref[...] * _rms(f_ref[...], gpost_ref[...], RMS_EPS)


def _final(plan, x13, f3, gpost, mod3):
    return pl.pallas_call(
        _final_kernel,
        grid=plan.grid,
        in_specs=[plan.act(), plan.act(plan.out_block_offset), plan.vec(), plan.mod(5)],
        out_specs=plan.act(),
        out_shape=jax.ShapeDtypeStruct(x13.shape, F32),
        compiler_params=_params("arbitrary"),
        name="final",
    )(x13, f3, gpost, mod3)


def _mm_kernel(*refs, n_a, n_w, k_sizes, epilogue):
    a_refs = refs[:n_a]
    w_refs = refs[n_a:n_a + n_w]
    o_ref = refs[n_a + n_w]
    wb_refs = refs[n_a + n_w + 1:]

    @pl.when(pl.program_id(1) == 0)
    def _():
        for w_ref, wb_ref in zip(w_refs, wb_refs):
            wb_ref[...] = w_ref[...].astype(BF16)

    parts = []
    for wb_ref in wb_refs:
        acc, k0 = None, 0
        for a_ref, ka in zip(a_refs, k_sizes):
            d = jnp.dot(a_ref[...], wb_ref[k0:k0 + ka, :], preferred_element_type=F32)
            acc = d if acc is None else acc + d
            k0 += ka
        parts.append(acc)
    o_ref[...] = epilogue(*parts).astype(o_ref.dtype)


def _mm(a_list, w_cols, n_out, epilogue, out_dtype, tm, tn, name):
    m = a_list[0].shape[0]
    k = w_cols[0][0].shape[0]
    tm, tn = math.gcd(tm, m), math.gcd(tn, n_out)
    k_sizes = tuple(a.shape[1] for a in a_list)
    assert sum(k_sizes) == k and m % tm == 0 and n_out % tn == 0 and all(c % tn == 0 for _, c in w_cols)
    a_specs = [pl.BlockSpec((tm, ka), lambda j, i: (i, 0)) for ka in k_sizes]
    w_specs = [pl.BlockSpec((k, tn), functools.partial(lambda j, i, o: (0, o + j), o=c // tn)) for _, c in w_cols]
    kern = functools.partial(_mm_kernel, n_a=len(a_list), n_w=len(w_cols), k_sizes=k_sizes, epilogue=epilogue)
    return pl.pallas_call(
        kern,
        grid=(n_out // tn, m // tm),
        in_specs=a_specs + w_specs,
        out_specs=pl.BlockSpec((tm, tn), lambda j, i: (i, j)),
        out_shape=jax.ShapeDtypeStruct((m, n_out), out_dtype),
        scratch_shapes=[pltpu.VMEM((k, tn), BF16) for _ in w_cols],
        compiler_params=_params("arbitrary", "arbitrary"),
        name=name,
    )(*a_list, *[w for w, _ in w_cols])


def _identity(x):
    return x


def _glu(a, b):
    return a * jax.nn.sigmoid(b)


def _swiglu(a, b):
    return _silu(a) * b


def _bucket(rel):
    half = NUM_BUCKETS // 2
    max_exact = half // 2
    n = np.abs(rel)
    nf = np.maximum(n, 1).astype(np.float32)
    large = max_exact + (np.log(nf / np.float32(max_exact)) / np.float32(math.log(REL_MAX_DISTANCE / max_exact))
                         * np.float32(half - max_exact)).astype(np.int32)
    large = np.minimum(large, half - 1)
    return np.where(rel > 0, half, 0) + np.where(n < max_exact, n, large)


def _bias_table(rel_bias, q_pos, k_pos):
    rel = k_pos[None, :] - q_pos[:, None]
    visible = (k_pos // CHUNK)[None, :] <= (q_pos // CHUNK)[:, None]
    onehot = (jnp.asarray(_bucket(rel))[..., None] == jnp.arange(NUM_BUCKETS)).astype(F32)
    b = jnp.einsum("qkb,bhm->hmqk", onehot, rel_bias.astype(F32), precision=lax.Precision.HIGHEST)
    return jnp.where(jnp.asarray(visible)[None, None], b, NEG)


def _lambda(lq1, lk1, lq2, lk2, lam_init):
    return (jnp.exp(jnp.sum(lq1[...] * lk1[...], keepdims=True))
            - jnp.exp(jnp.sum(lq2[...] * lk2[...], keepdims=True)) + lam_init)


def _softmax_step(s, v, m_ref, l_ref, acc_ref, idx):
    m_old = m_ref[idx]
    m_new = jnp.maximum(m_old, jnp.max(s, axis=-1, keepdims=True))
    alpha = jnp.exp(m_old - m_new)
    p = jnp.exp(s - m_new)
    l_ref[idx] = alpha * l_ref[idx] + jnp.sum(p, axis=-1, keepdims=True)
    acc_ref[idx] = alpha * acc_ref[idx] + jnp.dot(p.astype(BF16), v, preferred_element_type=F32)
    m_ref[idx] = m_new


def _attn_finish(m_ref, l_ref, acc_ref, lam, sw, lam_init, o_ref):
    o = acc_ref[0] / l_ref[0] - lam * (acc_ref[1] / l_ref[1])
    o = o * lax.rsqrt(jnp.mean(o * o, axis=-1, keepdims=True) + SUBLN_EPS) * sw * (1.0 - lam_init)
    o_ref[...] = o.astype(o_ref.dtype)


def _attn_prompt_kernel(q_ref, k_ref, v_ref, b_ref, lq1, lk1, lq2, lk2, sw_ref, o_ref, kb_ref, vb_ref,
                        *, lam_init, nq):
    t = ATT_TILE
    kb_ref[...] = k_ref[...].astype(BF16)
    vb_ref[...] = v_ref[...].astype(BF16)
    lam = _lambda(lq1, lk1, lq2, lk2, lam_init)
    sw = sw_ref[...]
    for qi in range(nq):
        q = q_ref[qi * t:(qi + 1) * t, :]
        n_far = max(qi - 1, 0) * t
        exps, sums = [], []
        for mp in range(2):
            cols = slice(mp * HEAD_DIM, (mp + 1) * HEAD_DIM)
            s = _dot_nt(q[:, cols].astype(BF16), kb_ref[0:(qi + 1) * t, cols]) * ATT_SCALE
            pieces = []
            if n_far:
                pieces.append(s[:, :n_far] + b_ref[mp, 2, 0:1, 0:1])
            if qi >= 1:
                pieces.append(s[:, n_far:n_far + t] + b_ref[mp, 1])
            pieces.append(s[:, qi * t:(qi + 1) * t] + b_ref[mp, 0])
            m = functools.reduce(jnp.maximum, [jnp.max(p, axis=-1, keepdims=True) for p in pieces])
            es = [jnp.exp(p - m) for p in pieces]
            exps.append(es)
            sums.append(functools.reduce(jnp.add, [jnp.sum(e, axis=-1, keepdims=True) for e in es]))
        c1 = 1.0 / sums[0]
        c2 = lam / sums[1]
        o, col = None, 0
        for e1, e2 in zip(exps[0], exps[1]):
            a = (e1 * c1 - e2 * c2).astype(BF16)
            d = jnp.dot(a, vb_ref[col:col + a.shape[1], :], preferred_element_type=F32)
            o = d if o is None else o + d
            col += a.shape[1]
        o = o * lax.rsqrt(jnp.mean(o * o, axis=-1, keepdims=True) + SUBLN_EPS) * sw * (1.0 - lam_init)
        o_ref[qi * t:(qi + 1) * t, :] = o.astype(o_ref.dtype)


def _attn_prompt(q, k, v, rel_bias, lams, subln_w, batch, seq, heads, lam_init):
    t = ATT_TILE
    assert seq % t == 0 and t % CHUNK == 0
    pos = np.arange(t)
    tiles = [_bias_table(rel_bias, pos + dt * t, pos) for dt in range(3)]
    assert np.all(_bucket(np.arange(-3 * t + 1, -t)) == NUM_BUCKETS // 2 - 1)
    bias = jnp.stack(tiles, axis=2)
    hw = 2 * HEAD_DIM
    vec = lambda n: pl.BlockSpec((1, n), lambda b, h: (0, 0))
    seq_spec = lambda: pl.BlockSpec((seq, hw), lambda b, h: (b, h))
    return pl.pallas_call(
        functools.partial(_attn_prompt_kernel, lam_init=lam_init, nq=seq // t),
        grid=(batch, heads),
        in_specs=[seq_spec(), seq_spec(), seq_spec(),
                  pl.BlockSpec((None, 2, 3, t, t), lambda b, h: (h, 0, 0, 0, 0)),
                  vec(HEAD_DIM), vec(HEAD_DIM), vec(HEAD_DIM), vec(HEAD_DIM), vec(hw)],
        out_specs=seq_spec(),
        out_shape=jax.ShapeDtypeStruct((batch * seq, heads * hw), BF16),
        scratch_shapes=[pltpu.VMEM((seq, hw), BF16), pltpu.VMEM((seq, hw), BF16)],
        compiler_params=_params("arbitrary", "arbitrary"),
        name="attn_prompt",
    )(q, k, v, bias, *lams, subln_w)


def _attn_sample_kernel(q_ref, ck_ref, cv_ref, kn_ref, vn_ref, b_ref, bn_ref, lq1, lk1, lq2, lk2, sw_ref, o_ref,
                        m_ref, l_ref, acc_ref, *, lam_init, n_tiles):
    kt_i = pl.program_id(1)

    @pl.when(kt_i == 0)
    def _():
        m_ref[...] = jnp.full(m_ref.shape, NEG, F32)
        l_ref[...] = jnp.zeros(l_ref.shape, F32)
        acc_ref[...] = jnp.zeros(acc_ref.shape, F32)

    def step(k2d, v2d, bias_of_map):
        kb = k2d.astype(BF16)
        vb = v2d.astype(BF16)
        for mp in range(2):
            qm = q_ref[mp].astype(BF16)
            s = _dot_nt(qm, kb[:, mp * HEAD_DIM:(mp + 1) * HEAD_DIM]) * ATT_SCALE + bias_of_map(mp)
            _softmax_step(s, vb, m_ref, l_ref, acc_ref, mp)

    rows = ck_ref.shape[0] * ck_ref.shape[1]
    bi = jnp.where(kt_i == n_tiles - 1, 1, 0)
    step(ck_ref[...].reshape(rows, ck_ref.shape[2]), cv_ref[...].reshape(rows, cv_ref.shape[2]),
         lambda mp: b_ref[bi, mp])

    @pl.when(kt_i == n_tiles - 1)
    def _():
        step(kn_ref[...], vn_ref[...], lambda mp: bn_ref[mp])
        sw = sw_ref[...]
        _attn_finish(m_ref, l_ref, acc_ref, _lambda(lq1, lk1, lq2, lk2, lam_init), sw, lam_init, o_ref)


def _head_expand(bias, heads):
    h, _, tq, tk = bias.shape
    eye = jnp.asarray(np.eye(heads, dtype=bool))
    full = jnp.where(eye[:, None, None, None, :], bias[..., None], NEG)
    return jnp.transpose(full, (1, 0, 2, 3, 4)).reshape(2, h * tq, tk * heads)


def _attn_sample(q, cache_k, cache_v, k_new, v_new, rel_bias, lams, subln_w, lam_init):
    db, past, heads, hw = cache_k.shape
    tq = q.shape[2] // heads
    t = min(ATT_TILE, past)
    assert past % t == 0 and t >= REL_MAX_DISTANCE + tq
    n_tiles = past // t
    q_pos = past + np.arange(tq)
    far = _bias_table(rel_bias, q_pos, np.arange(t))
    assert n_tiles == 1 or np.all(_bucket(np.arange(past - t)[None, :] - q_pos[:, None]) == NUM_BUCKETS // 2 - 1)
    near = _bias_table(rel_bias, q_pos, past - t + np.arange(t))
    bias = jnp.stack([_head_expand(far, heads), _head_expand(near, heads)])
    bias_new = _head_expand(_bias_table(rel_bias, q_pos, q_pos), heads)
    rq = heads * tq
    vec = lambda n: pl.BlockSpec((1, n), lambda b, j: (0, 0))
    return pl.pallas_call(
        functools.partial(_attn_sample_kernel, lam_init=lam_init, n_tiles=n_tiles),
        grid=(db, n_tiles),
        in_specs=[pl.BlockSpec((None, 2, rq, HEAD_DIM), lambda b, j: (b, 0, 0, 0)),
                  pl.BlockSpec((None, t, heads, hw), lambda b, j: (b, j, 0, 0)),
                  pl.BlockSpec((None, t, heads, hw), lambda b, j: (b, j, 0, 0)),
                  pl.BlockSpec((None, tq * heads, hw), lambda b, j: (b, 0, 0)),
                  pl.BlockSpec((None, tq * heads, hw), lambda b, j: (b, 0, 0)),
                  pl.BlockSpec((2, 2, rq, t * heads), lambda b, j: (0, 0, 0, 0)),
                  pl.BlockSpec((2, rq, tq * heads), lambda b, j: (0, 0, 0)),
                  vec(HEAD_DIM), vec(HEAD_DIM), vec(HEAD_DIM), vec(HEAD_DIM), vec(hw)],
        out_specs=pl.BlockSpec((None, rq, hw), lambda b, j: (b, 0, 0)),
        out_shape=jax.ShapeDtypeStruct((db, rq, hw), BF16),
        scratch_shapes=[pltpu.VMEM((2, rq, 1), F32), pltpu.VMEM((2, rq, 1), F32), pltpu.VMEM((2, rq, hw), F32)],
        compiler_params=_params("arbitrary", "arbitrary"),
        name="attn_sample",
    )(q, cache_k, cache_v, k_new, v_new, bias, bias_new, *lams, subln_w)


CONV_HIST = 32


def _conv_kernel(cur_ref, hist_ref, w_ref, b_ref, g_ref, beta_ref, o_ref, xs_ref, wb_ref, *, width, zero_first):
    nb, tt, c = cur_ref.shape
    rows = 8
    length = CONV_HIST + tt

    @pl.when(jnp.logical_and(pl.program_id(0) == 0, pl.program_id(1) == 0))
    def _():
        for tap in range(width):
            wb_ref[tap] = jnp.broadcast_to(w_ref[tap:tap + 1, :], (rows, c))

    hist = hist_ref[...]
    if zero_first:
        hist = jnp.where(pl.program_id(1) == 0, 0.0, hist)
    xs_ref[0, :, 0:CONV_HIST, :] = hist
    xs_ref[0, :, CONV_HIST:, :] = cur_ref[...]
    for s in range(1, rows):
        xs_ref[s, :, 0:length - rows, :] = xs_ref[0, :, s:s + length - rows, :]
    lead = CONV_HIST - (width - 1)
    for n in range(nb):
        for r0 in range(0, tt, rows):
            acc = jnp.zeros((rows, c), F32) + b_ref[...]
            for tap in range(width):
                s = (lead + tap) % rows
                a = lead + tap - s + r0
                acc = acc + xs_ref[s, n, a:a + rows, :] * wb_ref[tap]
            mu = jnp.mean(acc, axis=-1, keepdims=True)
            cen = acc - mu
            var = jnp.mean(cen * cen, axis=-1, keepdims=True)
            y = cen * lax.rsqrt(var + LN_EPS) * g_ref[...] + beta_ref[...]
            o_ref[n, r0:r0 + rows, :] = _silu(y).astype(o_ref.dtype)


def _conv(cur3, hist3, hist_index, nb, tt, w, b, g, beta, zero_first):
    nseq, seq, c = cur3.shape
    width = w.shape[0]
    assert width - 1 <= CONV_HIST and seq % tt == 0 and nseq % nb == 0 and tt % 8 == 0
    vec = lambda: pl.BlockSpec((1, c), lambda s, i: (0, 0))
    return pl.pallas_call(
        functools.partial(_conv_kernel, width=width, zero_first=zero_first),
        grid=(nseq // nb, seq // tt),
        in_specs=[pl.BlockSpec((nb, tt, c), lambda s, i: (s, i, 0)),
                  pl.BlockSpec((nb, CONV_HIST, c), hist_index),
                  pl.BlockSpec((width, c), lambda s, i: (0, 0)),
                  vec(), vec(), vec()],
        out_specs=pl.BlockSpec((nb, tt, c), lambda s, i: (s, i, 0)),
        out_shape=jax.ShapeDtypeStruct(cur3.shape, BF16),
        scratch_shapes=[pltpu.VMEM((8, nb, CONV_HIST + tt, c), F32), pltpu.VMEM((width, 8, c), F32)],
        compiler_params=_params("arbitrary", "arbitrary"),
        name="conv",
    )(cur3, hist3, w, b, g, beta)


def _first_argmax(v, iota, axis, size):
    m = jnp.max(v, axis=axis, keepdims=True)
    i = jnp.min(jnp.where(v == m, iota, size), axis=axis, keepdims=True)
    return m, i


def _router_kernel(h_ref, w_ref, bias_ref, idx_ref, gate_ref, rank_ref, cnt_ref, carry_ref):
    n_exp = w_ref.shape[0]
    tm = h_ref.shape[0]
    per_group = n_exp // N_EXPERT_GROUPS

    @pl.when(pl.program_id(0) == 0)
    def _():
        carry_ref[...] = jnp.zeros(carry_ref.shape, F32)

    w = w_ref[...]
    w1 = w.astype(BF16)
    r1 = w - w1.astype(F32)
    w2 = r1.astype(BF16)
    w3 = (r1 - w2.astype(F32)).astype(BF16)
    h = h_ref[...]
    logits = _dot_nt(w1, h) + _dot_nt(w2, h) + _dot_nt(w3, h)
    scores = jax.nn.sigmoid(logits)
    choice = scores + bias_ref[...]

    ch3 = choice.reshape(N_EXPERT_GROUPS, per_group, tm)
    io3 = lax.broadcasted_iota(I32, ch3.shape, 1)
    m1, i1 = _first_argmax(ch3, io3, 1, per_group)
    m2 = jnp.max(jnp.where(io3 == i1, -jnp.inf, ch3), axis=1, keepdims=True)
    grp = (m1 + m2).reshape(N_EXPERT_GROUPS, tm)
    gio = lax.broadcasted_iota(I32, grp.shape, 0)
    keep = jnp.zeros(grp.shape, jnp.bool_)
    for _ in range(TOPK_GROUPS):
        _, gi = _first_argmax(grp, gio, 0, N_EXPERT_GROUPS)
        hit = gio == gi
        keep = jnp.logical_or(keep, hit)
        grp = jnp.where(hit, -jnp.inf, grp)
    keep3 = jnp.broadcast_to(keep.reshape(N_EXPERT_GROUPS, 1, tm), ch3.shape)
    masked = jnp.where(keep3, ch3, -jnp.inf).reshape(n_exp, tm)

    eio = lax.broadcasted_iota(I32, masked.shape, 0)
    sel = jnp.zeros(masked.shape, jnp.bool_)
    picks, pick_scores = [], []
    for _ in range(TOP_K):
        _, ei = _first_argmax(masked, eio, 0, n_exp)
        hit = eio == ei
        picks.append((ei, hit))
        pick_scores.append(jnp.sum(jnp.where(hit, scores, 0.0), axis=0, keepdims=True))
        sel = jnp.logical_or(sel, hit)
        masked = jnp.where(hit, -jnp.inf, masked)
    total = pick_scores[0]
    for s in pick_scores[1:]:
        total = total + s

    sel_b = jnp.where(sel, 1.0, 0.0).astype(BF16)
    tri = (lax.broadcasted_iota(I32, (tm, tm), 0) < lax.broadcasted_iota(I32, (tm, tm), 1))
    rank = jnp.dot(sel_b, jnp.where(tri, 1.0, 0.0).astype(BF16), preferred_element_type=F32) + carry_ref[...]
    for kk, (ei, hit) in enumerate(picks):
        idx_ref[kk:kk + 1, :] = ei
        gate_ref[kk:kk + 1, :] = pick_scores[kk] / total * ROUTED_SCALE
        rank_ref[kk:kk + 1, :] = jnp.sum(jnp.where(hit, rank, 0.0), axis=0, keepdims=True).astype(I32)
    carry_ref[...] = carry_ref[...] + jnp.sum(jnp.where(sel, 1.0, 0.0), axis=1, keepdims=True)
    cnt_ref[...] = carry_ref[...].astype(I32)


def _router(h2b, w_router_t, bias_col):
    t, d = h2b.shape
    n_exp = w_router_t.shape[0]
    tm = math.gcd(512, t)
    assert t % tm == 0
    tok = lambda: pl.BlockSpec((TOP_K, tm), lambda i: (0, i))
    return pl.pallas_call(
        _router_kernel,
        grid=(t // tm,),
        in_specs=[pl.BlockSpec((tm, d), lambda i: (i, 0)),
                  pl.BlockSpec((n_exp, d), lambda i: (0, 0)),
                  pl.BlockSpec((n_exp, 1), lambda i: (0, 0))],
        out_specs=[tok(), tok(), tok(), pl.BlockSpec((n_exp, 1), lambda i: (0, 0))],
        out_shape=[jax.ShapeDtypeStruct((TOP_K, t), I32), jax.ShapeDtypeStruct((TOP_K, t), F32),
                   jax.ShapeDtypeStruct((TOP_K, t), I32), jax.ShapeDtypeStruct((n_exp, 1), I32)],
        scratch_shapes=[pltpu.VMEM((n_exp, 1), F32)],
        compiler_params=_params("arbitrary"),
        name="router",
    )(h2b, w_router_t, bias_col)


def _dispatch_kernel(rows_ref, h_ref, xs_ref, sem):
    n_tok = h_ref.shape[0]

    def body(t, carry):
        for kk in range(TOP_K):
            pltpu.make_async_copy(h_ref.at[pl.ds(t, 1)], xs_ref.at[pl.ds(rows_ref[kk, t], 1)], sem).start()
        return carry

    lax.fori_loop(0, n_tok, body, 0)
    for _ in range(TOP_K):
        pltpu.make_async_copy(h_ref, xs_ref.at[pl.ds(0, n_tok)], sem).wait()


def _dispatch(rows, h2p, n_rows):
    t, w = h2p.shape
    nt = math.gcd(DISPATCH_ROWS, t)
    assert t % nt == 0 and n_rows >= nt
    rows = jnp.transpose(rows.reshape(TOP_K, t // nt, nt), (1, 0, 2))
    return pl.pallas_call(
        _dispatch_kernel,
        grid=(t // nt,),
        in_specs=[pl.BlockSpec((None, TOP_K, nt), lambda i: (i, 0, 0), memory_space=pltpu.SMEM),
                  pl.BlockSpec((nt, w), lambda i: (i, 0))],
        out_specs=pl.BlockSpec(memory_space=pl.ANY),
        out_shape=jax.ShapeDtypeStruct((n_rows, w), h2p.dtype),
        scratch_shapes=[pltpu.SemaphoreType.DMA(())],
        compiler_params=pltpu.CompilerParams(dimension_semantics=("arbitrary",), has_side_effects=True),
        name="dispatch",
    )(rows, h2p)


def _expert_kernel(seg_e_ref, seg_cnt_ref, nseg_ref, x_ref, wg_ref, wu_ref, wd_ref, y_ref,
                   gacc_ref, uacc_ref, hid_ref):
    s = pl.program_id(0)
    c = pl.program_id(1)
    active = s < nseg_ref[0]

    @pl.when(jnp.logical_and(active, c < EXP_KCHUNKS))
    def _():
        rows = lax.broadcasted_iota(I32, (x_ref.shape[0], 1), 0)
        lo, hi = _unpack_halves(jnp.where(rows < seg_cnt_ref[s], x_ref[...], jnp.uint32(0)))
        half = x_ref.shape[1]
        wg = wg_ref[...].astype(BF16)
        wu = wu_ref[...].astype(BF16)
        g = (jnp.dot(lo, wg[:half], preferred_element_type=F32) + jnp.dot(hi, wg[half:], preferred_element_type=F32))
        u = (jnp.dot(lo, wu[:half], preferred_element_type=F32) + jnp.dot(hi, wu[half:], preferred_element_type=F32))

        @pl.when(c == 0)
        def _():
            gacc_ref[...] = g
            uacc_ref[...] = u

        @pl.when(c > 0)
        def _():
            gacc_ref[...] += g
            uacc_ref[...] += u

        @pl.when(c == EXP_KCHUNKS - 1)
        def _():
            hid_ref[...] = (_silu(gacc_ref[...]) * uacc_ref[...]).astype(BF16)

    @pl.when(jnp.logical_and(active, c >= EXP_KCHUNKS))
    def _():
        y_ref[...] = jnp.dot(hid_ref[...], wd_ref[...].astype(BF16), preferred_element_type=F32)


def _experts(seg_e, seg_cnt, nseg, xs, w_gate, w_up, w_down, n_seg_max):
    n_exp, d, ff = w_gate.shape
    kc, nc = EXP_KCHUNKS, EXP_NCHUNKS
    dk, dn = d // kc, d // nc
    r = SEG_ROWS

    def seg(s, c, nseg_ref):
        on = s < nseg_ref[0]
        return jnp.where(on, s, nseg_ref[0] - 1), jnp.where(on, c, kc + nc - 1)

    def x_map(s, c, se, sc, ns):
        s2, c2 = seg(s, c, ns)
        return s2, jnp.minimum(c2, kc - 1)

    def w_in_map(s, c, se, sc, ns):
        s2, c2 = seg(s, c, ns)
        return se[s2], jnp.minimum(c2, kc - 1), 0

    def w_out_map(s, c, se, sc, ns):
        s2, c2 = seg(s, c, ns)
        return se[s2], 0, jnp.maximum(c2 - kc, 0)

    def y_map(s, c, se, sc, ns):
        s2, c2 = seg(s, c, ns)
        return s2, jnp.maximum(c2 - kc, 0)

    grid_spec = pltpu.PrefetchScalarGridSpec(
        num_scalar_prefetch=3,
        grid=(n_seg_max, kc + nc),
        in_specs=[pl.BlockSpec((r, dk // 2), x_map),
                  pl.BlockSpec((None, dk, ff), w_in_map),
                  pl.BlockSpec((None, dk, ff), w_in_map),
                  pl.BlockSpec((None, ff, dn), w_out_map)],
        out_specs=pl.BlockSpec((r, dn), y_map),
        scratch_shapes=[pltpu.VMEM((r, ff), F32), pltpu.VMEM((r, ff), F32), pltpu.VMEM((r, ff), BF16)],
    )
    return pl.pallas_call(
        _expert_kernel,
        grid_spec=grid_spec,
        out_shape=jax.ShapeDtypeStruct((xs.shape[0], d), F32),
        compiler_params=_params("arbitrary", "arbitrary"),
        name="experts",
    )(seg_e, seg_cnt, nseg, xs, w_gate, w_up, w_down)


def _combine_kernel(rows_ref, gate_ref, hs_ref, wsd_ref, ys_ref, f_ref, buf_ref, wb_ref, sem):
    n_tok = gate_ref.shape[0]

    @pl.when(pl.program_id(0) == 0)
    def _():
        wb_ref[...] = wsd_ref[...].astype(BF16)

    def body(t, carry):
        for kk in range(TOP_K):
            pltpu.make_async_copy(ys_ref.at[pl.ds(rows_ref[kk, t], 1)], buf_ref.at[kk, pl.ds(t, 1)], sem).start()
        return carry

    lax.fori_loop(0, n_tok, body, 0)
    shared = jnp.dot(hs_ref[...], wb_ref[...], preferred_element_type=F32)
    for kk in range(TOP_K):
        pltpu.make_async_copy(ys_ref.at[pl.ds(0, n_tok)], buf_ref.at[kk], sem).wait()
    gates = gate_ref[...]
    acc = shared
    for kk in range(TOP_K):
        acc = acc + buf_ref[kk] * gates[:, kk:kk + 1]
    f_ref[...] = acc


def _combine(rows, gates_t, hid_sh, w_sh_down, ys):
    t, ff = hid_sh.shape
    d = w_sh_down.shape[1]
    nt = math.gcd(COMBINE_ROWS, t)
    assert t % nt == 0
    rows = jnp.transpose(rows.reshape(TOP_K, t // nt, nt), (1, 0, 2))
    return pl.pallas_call(
        _combine_kernel,
        grid=(t // nt,),
        in_specs=[pl.BlockSpec((None, TOP_K, nt), lambda i: (i, 0, 0), memory_space=pltpu.SMEM),
                  pl.BlockSpec((nt, TOP_K), lambda i: (i, 0)),
                  pl.BlockSpec((nt, ff), lambda i: (i, 0)),
                  pl.BlockSpec((ff, d), lambda i: (0, 0)),
                  pl.BlockSpec(memory_space=pl.ANY)],
        out_specs=pl.BlockSpec((nt, d), lambda i: (i, 0)),
        out_shape=jax.ShapeDtypeStruct((t, d), F32),
        scratch_shapes=[pltpu.VMEM((TOP_K, nt, d), F32), pltpu.VMEM((ff, d), BF16), pltpu.SemaphoreType.DMA(())],
        compiler_params=_params("arbitrary"),
        name="combine",
    )(rows, gates_t, hid_sh, w_sh_down, ys)


def _layer(l, lam_init, x_prompt, x_sample, c_prompt, c_sample, cache_k, cache_v, state_conv, rel_bias, p):
    batch, seq, d = x_prompt.shape
    db, dseq, _ = x_sample.shape
    past, heads = cache_k.shape[1], cache_k.shape[2]
    hw = 2 * HEAD_DIM
    aw = heads * hw
    cc = p["conv_dw_w"].shape[1]
    width = p["conv_dw_w"].shape[0]
    assert p["w_in"].shape[1] == 3 * aw + 2 * cc and seq % dseq == 0
    tp, ts = batch * seq, db * dseq
    row = lambda v: v.reshape(1, -1)

    n_mod_rows = -(-(db + batch) // 16) * 16
    c_all = jnp.concatenate([c_sample, c_prompt, jnp.zeros((n_mod_rows - db - batch, d), F32)], axis=0)
    mod3 = _ada(c_all, p["w_ada"], row(p["b_ada"])).reshape(n_mod_rows, 1, N_MOD * d)

    gp = tp // dseq
    gps = seq // dseq
    nb_p = math.gcd(ROW_GROUPS, gps)
    nb_s = math.gcd(ROW_GROUPS, db)
    assert gps % nb_p == 0 and db % nb_s == 0 and gp % nb_s == 0
    plan_p = _RowPlan(gp, dseq, d, nb_p, 1, lambda i: db + (i * nb_p) // gps, 0)
    plan_s = _RowPlan(db, dseq, d, nb_s, nb_s, lambda i: i, gp // nb_s)
    xp3 = x_prompt.reshape(gp, dseq, d)

    hp = _prenorm(plan_p, xp3, row(p["g_pre_mix"]), mod3).reshape(tp, d)
    hs = _prenorm(plan_s, x_sample, row(p["g_pre_mix"]), mod3).reshape(ts, d)

    w_in = p["w_in"]
    proj = lambda h, off, nm: _mm([h], [(w_in, off)], aw, _identity, F32, 512, 512, nm)
    qp, kp, vp = proj(hp, 0, "q_prompt"), proj(hp, aw, "k_prompt"), proj(hp, 2 * aw, "v_prompt")
    qs, ks, vs = proj(hs, 0, "q_sample"), proj(hs, aw, "k_sample"), proj(hs, 2 * aw, "v_sample")
    glu_cols = [(w_in, 3 * aw), (w_in, 3 * aw + cc)]
    glu_p = _mm([hp], glu_cols, cc, _glu, F32, 512, 256, "glu_prompt")
    glu_s = _mm([hs], glu_cols, cc, _glu, F32, 512, 256, "glu_sample")

    lams = [row(p[n]) for n in ("lambda_q1", "lambda_k1", "lambda_q2", "lambda_k2")]
    subln = row(p["subln_w"])
    attn_p = _attn_prompt(qp, kp, vp, rel_bias, lams, subln, batch, seq, heads, lam_init)

    new_k_s = ks.reshape(db, dseq, heads, hw)
    new_v_s = vs.reshape(db, dseq, heads, hw)
    q_s = jnp.transpose(qs.reshape(db, dseq, heads, 2, HEAD_DIM), (0, 3, 2, 1, 4)).reshape(db, 2, heads * dseq, HEAD_DIM)
    o_s = _attn_sample(q_s, cache_k, cache_v, new_k_s.reshape(db, dseq * heads, hw),
                       new_v_s.reshape(db, dseq * heads, hw), rel_bias, lams, subln, lam_init)
    attn_s = jnp.transpose(o_s.reshape(db, heads, dseq, hw), (0, 2, 1, 3)).reshape(ts, aw)

    conv_args = (p["conv_dw_w"], row(p["conv_dw_b"]), row(p["conv_ln_g"]), row(p["conv_ln_b"]))
    glu_p3 = glu_p.reshape(batch, seq, cc)
    tt = min(128, seq)
    per = tt // CONV_HIST
    conv_p = _conv(glu_p3, glu_p3, lambda s, i: (s, jnp.maximum(i * per - 1, 0), 0), 1, tt, *conv_args,
                   zero_first=True).reshape(tp, cc)
    glu_s3 = glu_s.reshape(db, dseq, cc)
    hist_s = jnp.concatenate([jnp.zeros((db, CONV_HIST - (width - 1), cc), F32), state_conv], axis=1)
    nb_c = math.gcd(2, db)
    conv_s = _conv(glu_s3, hist_s, lambda s, i: (s, 0, 0), nb_c, dseq, *conv_args, zero_first=False).reshape(ts, cc)
    new_conv_p = glu_p3[:, seq - (width - 1):]
    new_conv_s = jnp.concatenate([state_conv, glu_s3], axis=1)[:, -(width - 1):]

    mix_p = _mm([attn_p, conv_p], [(p["w_out"], 0)], d, _identity, F32, 512, 512, "out_prompt")
    mix_s = _mm([attn_s, conv_s], [(p["w_out"], 0)], d, _identity, F32, 512, 512, "out_sample")

    gt = gp + db
    gpost, gpre = row(p["g_post_mix"]), row(p["g_pre_ffn"])
    x1p, h2ba, h2pa = _postmix(plan_p, xp3, mix_p.reshape(gp, dseq, d), gpost, gpre, mod3, gt)
    x1s, h2ba, h2pa = _postmix(plan_s, x_sample, mix_s.reshape(db, dseq, d), gpost, gpre, mod3, gt,
                               prev=(h2ba, h2pa))
    t = tp + ts
    h2b = h2ba.reshape(t, d)
    h2p = h2pa.reshape(t, d // 2)

    n_exp = p["w_router"].shape[1]
    idx, gates, rank, counts = _router(h2b, p["w_router"].T, p["b_router_corr"].reshape(n_exp, 1))
    counts = counts.reshape(n_exp)
    n_seg_e = (counts + SEG_ROWS - 1) // SEG_ROWS
    seg_end = jnp.cumsum(n_seg_e)
    seg_start = seg_end - n_seg_e
    n_seg_max = (t * TOP_K) // SEG_ROWS + n_exp
    rows = seg_start[idx] * SEG_ROWS + rank
    sid = jnp.arange(n_seg_max, dtype=I32)
    seg_e = jnp.minimum(jnp.searchsorted(seg_end, sid, side="right"), n_exp - 1).astype(I32)
    seg_cnt = jnp.clip(counts[seg_e] - (sid - seg_start[seg_e]) * SEG_ROWS, 0, SEG_ROWS).astype(I32)
    nseg = seg_end[-1:].astype(I32)

    xs = _dispatch(rows.astype(I32), h2p, n_seg_max * SEG_ROWS)
    ys = _experts(seg_e, seg_cnt, nseg, xs, p["w_exp_gate"], p["w_exp_up"], p["w_exp_down"], n_seg_max)
    ff_sh = p["w_sh_gate"].shape[1]
    hid_sh = _mm([h2b], [(p["w_sh_gate"], 0), (p["w_sh_up"], 0)], ff_sh, _swiglu, BF16, 512, 256, "shared_up")
    f = _combine(rows.astype(I32), gates.T, hid_sh, p["w_sh_down"], ys)

    f3 = f.reshape(gt, dseq, d)
    gpf = row(p["g_post_ffn"])
    yp = _final(plan_p, x1p, f3, gpf, mod3).reshape(batch, seq, d)
    ysmp = _final(plan_s, x1s, f3, gpf, mod3)
    new_k_p = kp.reshape(batch, seq, heads, hw)
    new_v_p = vp.reshape(batch, seq, heads, hw)
    return yp, ysmp, new_k_p, new_v_p, new_conv_p, new_k_s, new_v_s, new_conv_s


def kernel(x_prompt, x_sample, c_prompt, c_sample, cache_k, cache_v, state_conv, rel_bias, w_ada, b_ada, g_pre_mix, g_post_mix, g_pre_ffn, g_post_ffn, w_in, lambda_q1, lambda_k1, lambda_q2, lambda_k2, subln_w, conv_dw_w, conv_dw_b, conv_ln_g, conv_ln_b, w_out, w_router, b_router_corr, w_exp_gate, w_exp_up, w_exp_down, w_sh_gate, w_sh_up, w_sh_down):
    weights = dict(w_ada=w_ada, b_ada=b_ada, g_pre_mix=g_pre_mix, g_post_mix=g_post_mix, g_pre_ffn=g_pre_ffn,
                   g_post_ffn=g_post_ffn, w_in=w_in, lambda_q1=lambda_q1, lambda_k1=lambda_k1,
                   lambda_q2=lambda_q2, lambda_k2=lambda_k2, subln_w=subln_w, conv_dw_w=conv_dw_w,
                   conv_dw_b=conv_dw_b, conv_ln_g=conv_ln_g, conv_ln_b=conv_ln_b, w_out=w_out,
                   w_router=w_router, b_router_corr=b_router_corr, w_exp_gate=w_exp_gate, w_exp_up=w_exp_up,
                   w_exp_down=w_exp_down, w_sh_gate=w_sh_gate, w_sh_up=w_sh_up, w_sh_down=w_sh_down)
    depth = w_in.shape[0]
    xp, xs = x_prompt, x_sample
    outs = [[] for _ in range(6)]
    for l in range(depth):
        p = {k: (v.reshape(v.shape[1:]) if depth == 1 else v[l]) for k, v in weights.items()}
        lam_init = 0.8 - 0.6 * math.exp(-0.3 * l)
        ck, cv, sc = ((a.reshape(a.shape[1:]) if depth == 1 else a[l]) for a in (cache_k, cache_v, state_conv))
        xp, xs, *state = _layer(l, lam_init, xp, xs, c_prompt, c_sample, ck, cv, sc, rel_bias, p)
        for acc, s in zip(outs, state):
            acc.append(s)
    return (xp, xs) + tuple(jnp.stack(o) for o in outs)
```

```python
import functools
import math

import numpy as np
import jax
import jax.numpy as jnp
from jax import lax
from jax.experimental import pallas as pl
from jax.experimental.pallas import tpu as pltpu

F32 = jnp.float32
BF16 = jnp.bfloat16
I32 = jnp.int32
U32 = jnp.uint32

CHUNK = 64
HEAD_DIM = 128
NUM_BUCKETS = 32
REL_MAX_DISTANCE = 128
TOP_K = 8
N_EXPERT_GROUPS = 8
TOPK_GROUPS = 4
ROUTED_SCALE = 2.5
RMS_EPS = 1e-6
SUBLN_EPS = 1e-5
LN_EPS = 1e-5
N_MOD = 6
NEG = -1e30
ATT_SCALE = HEAD_DIM ** -0.5

VMEM_LIMIT_BYTES = 56 * 1024 * 1024
ATT_TILE = 256
EXP_ROWS = 128
SEG_BLOCKS = 8
EXP_KCHUNKS = 2
EXP_NCHUNKS = 2
COMBINE_ROWS = 128
DISPATCH_ROWS = 256
ROW_GROUPS = 4


def _params(*sem):
    return pltpu.CompilerParams(dimension_semantics=sem, vmem_limit_bytes=VMEM_LIMIT_BYTES)


def _silu(x):
    return x * jax.nn.sigmoid(x)


def _dot_nt(a, b):
    return lax.dot_general(a, b, (((1,), (1,)), ((), ())), preferred_element_type=F32)


def _ada_kernel(c_ref, w_ref, b_ref, o_ref):
    a = _silu(c_ref[...]).astype(BF16)
    o_ref[...] = jnp.dot(a, w_ref[...].astype(BF16), preferred_element_type=F32) + b_ref[...]


def _ada(c_all, w_ada, b_ada):
    rows, d = c_all.shape
    n = w_ada.shape[1]
    tn = min(512, n)
    return pl.pallas_call(
        _ada_kernel,
        grid=(n // tn,),
        in_specs=[pl.BlockSpec((rows, d), lambda j: (0, 0)),
                  pl.BlockSpec((d, tn), lambda j: (0, j)),
                  pl.BlockSpec((1, tn), lambda j: (0, j))],
        out_specs=pl.BlockSpec((rows, tn), lambda j: (0, j)),
        out_shape=jax.ShapeDtypeStruct((rows, n), F32),
        compiler_params=_params("arbitrary"),
        name="ada",
    )(c_all, w_ada, b_ada)


class _RowPlan:
    def __init__(self, n_groups, group_rows, d, nb, mod_nb, mod_index, out_block_offset=0):
        self.n_groups, self.group_rows, self.d, self.nb = n_groups, group_rows, d, nb
        self.mod_nb, self.mod_index, self.out_block_offset = mod_nb, mod_index, out_block_offset
        self.grid = (n_groups // nb,)

    def act(self, offset=0):
        return pl.BlockSpec((self.nb, self.group_rows, self.d), lambda i: (i + offset, 0, 0))

    def mod(self, chunk):
        return pl.BlockSpec((self.mod_nb, 1, self.d), lambda i: (self.mod_index(i), 0, chunk))

    def vec(self):
        return pl.BlockSpec((1, self.d), lambda i: (0, 0))


def _rms(x, g, eps):
    return x * lax.rsqrt(jnp.mean(x * x, axis=-1, keepdims=True) + eps) * g


def _prenorm_kernel(x_ref, g_ref, sh_ref, sc_ref, o_ref):
    y = _rms(x_ref[...], g_ref[...], RMS_EPS)
    o_ref[...] = (y * (1.0 + sc_ref[...]) + sh_ref[...]).astype(o_ref.dtype)


def _prenorm(plan, x3, g, mod3):
    return pl.pallas_call(
        _prenorm_kernel,
        grid=plan.grid,
        in_specs=[plan.act(), plan.vec(), plan.mod(0), plan.mod(1)],
        out_specs=plan.act(),
        out_shape=jax.ShapeDtypeStruct(x3.shape, BF16),
        compiler_params=_params("arbitrary"),
        name="prenorm",
    )(x3, g, mod3, mod3)


def _pack_halves(x):
    n = x.shape[-1] // 2
    lo = lax.bitcast_convert_type(x[..., :n].astype(BF16).astype(F32), U32) >> 16
    hi = lax.bitcast_convert_type(x[..., n:].astype(BF16).astype(F32), U32) & jnp.uint32(0xFFFF0000)
    return hi | lo


def _unpack_halves(w):
    lo = lax.bitcast_convert_type(w << 16, F32).astype(BF16)
    hi = lax.bitcast_convert_type(w & jnp.uint32(0xFFFF0000), F32).astype(BF16)
    return lo, hi


def _postmix_kernel(x_ref, mix_ref, gpost_ref, gpre_ref, g1_ref, sh2_ref, sc2_ref, *rest):
    x1_ref, h2b_ref, h2p_ref = rest[-3:]
    x1 = x_ref[...] + g1_ref[...] * _rms(mix_ref[...], gpost_ref[...], RMS_EPS)
    x1_ref[...] = x1
    h2 = _rms(x1, gpre_ref[...], RMS_EPS) * (1.0 + sc2_ref[...]) + sh2_ref[...]
    h2b_ref[...] = h2.astype(BF16)
    dk = h2.shape[-1] // EXP_KCHUNKS
    for c in range(EXP_KCHUNKS):
        h2p_ref[:, :, c * (dk // 2):(c + 1) * (dk // 2)] = _pack_halves(h2[:, :, c * dk:(c + 1) * dk])


def _postmix(plan, x3, mix3, gpost, gpre, mod3, total_groups, prev=None):
    off = plan.out_block_offset
    in_specs = [plan.act(), plan.act(), plan.vec(), plan.vec(), plan.mod(2), plan.mod(3), plan.mod(4)]
    args = [x3, mix3, gpost, gpre, mod3, mod3, mod3]
    aliases = {}
    if prev is not None:
        in_specs += [pl.BlockSpec(memory_space=pl.ANY), pl.BlockSpec(memory_space=pl.ANY)]
        args += list(prev)
        aliases = {7: 1, 8: 2}
    shape_all = (total_groups, plan.group_rows, plan.d)
    shape_packed = (total_groups, plan.group_rows, plan.d // 2)
    packed_spec = pl.BlockSpec((plan.nb, plan.group_rows, plan.d // 2), lambda i: (i + off, 0, 0))
    return pl.pallas_call(
        _postmix_kernel,
        grid=plan.grid,
        in_specs=in_specs,
        out_specs=[plan.act(), plan.act(off), packed_spec],
        out_shape=[jax.ShapeDtypeStruct(x3.shape, F32),
                   jax.ShapeDtypeStruct(shape_all, BF16),
                   jax.ShapeDtypeStruct(shape_packed, U32)],
        input_output_aliases=aliases,
        compiler_params=_params("arbitrary"),
        name="postmix",
    )(*args)


def _final_kernel(x1_ref, f_ref, gpost_ref, g2_ref, o_ref):
    o_ref[...] = x1_ref[...] + g2_ref[...] * _rms(f_ref[...], gpost_ref[...], RMS_EPS)


def _final(plan, x13, f3, gpost, mod3):
    return pl.pallas_call(
        _final_kernel,
        grid=plan.grid,
        in_specs=[plan.act(), plan.act(plan.out_block_offset), plan.vec(), plan.mod(5)],
        out_specs=plan.act(),
        out_shape=jax.ShapeDtypeStruct(x13.shape, F32),
        compiler_params=_params("arbitrary"),
        name="final",
    )(x13, f3, gpost, mod3)


def _mm_kernel(*refs, n_a, n_w, k_sizes, epilogue):
    a_refs = refs[:n_a]
    w_refs = refs[n_a:n_a + n_w]
    o_ref = refs[n_a + n_w]
    wb_refs = refs[n_a + n_w + 1:]

    @pl.when(pl.program_id(1) == 0)
    def _():
        for w_ref, wb_ref in zip(w_refs, wb_refs):
            wb_ref[...] = w_ref[...].astype(BF16)

    parts = []
    for wb_ref in wb_refs:
        acc, k0 = None, 0
        for a_ref, ka in zip(a_refs, k_sizes):
            d = jnp.dot(a_ref[...], wb_ref[k0:k0 + ka, :], preferred_element_type=F32)
            acc = d if acc is None else acc + d
            k0 += ka
        parts.append(acc)
    o_ref[...] = epilogue(*parts).astype(o_ref.dtype)


def _mm(a_list, w_cols, n_out, epilogue, out_dtype, tm, tn, name):
    m = a_list[0].shape[0]
    k = w_cols[0][0].shape[0]
    tm, tn = math.gcd(tm, m), math.gcd(tn, n_out)
    k_sizes = tuple(a.shape[1] for a in a_list)
    assert sum(k_sizes) == k and m % tm == 0 and n_out % tn == 0 and all(c % tn == 0 for _, c in w_cols)
    a_specs = [pl.BlockSpec((tm, ka), lambda j, i: (i, 0)) for ka in k_sizes]
    w_specs = [pl.BlockSpec((k, tn), functools.partial(lambda j, i, o: (0, o + j), o=c // tn)) for _, c in w_cols]
    kern = functools.partial(_mm_kernel, n_a=len(a_list), n_w=len(w_cols), k_sizes=k_sizes, epilogue=epilogue)
    return pl.pallas_call(
        kern,
        grid=(n_out // tn, m // tm),
        in_specs=a_specs + w_specs,
        out_specs=pl.BlockSpec((tm, tn), lambda j, i: (i, j)),
        out_shape=jax.ShapeDtypeStruct((m, n_out), out_dtype),
        scratch_shapes=[pltpu.VMEM((k, tn), BF16) for _ in w_cols],
        compiler_params=_params("arbitrary", "arbitrary"),
        name=name,
    )(*a_list, *[w for w, _ in w_cols])


def _identity(x):
    return x


def _glu(a, b):
    return a * jax.nn.sigmoid(b)


def _swiglu(a, b):
    return _silu(a) * b


def _bucket(rel):
    half = NUM_BUCKETS // 2
    max_exact = half // 2
    n = np.abs(rel)
    nf = np.maximum(n, 1).astype(np.float32)
    large = max_exact + (np.log(nf / np.float32(max_exact)) / np.float32(math.log(REL_MAX_DISTANCE / max_exact))
                         * np.float32(half - max_exact)).astype(np.int32)
    large = np.minimum(large, half - 1)
    return np.where(rel > 0, half, 0) + np.where(n < max_exact, n, large)


def _bias_table(rel_bias, q_pos, k_pos):
    rel = k_pos[None, :] - q_pos[:, None]
    visible = (k_pos // CHUNK)[None, :] <= (q_pos // CHUNK)[:, None]
    onehot = (jnp.asarray(_bucket(rel))[..., None] == jnp.arange(NUM_BUCKETS)).astype(F32)
    b = jnp.einsum("qkb,bhm->hmqk", onehot, rel_bias.astype(F32), precision=lax.Precision.HIGHEST)
    return jnp.where(jnp.asarray(visible)[None, None], b, NEG)


def _lambda(lq1, lk1, lq2, lk2, lam_init):
    return (jnp.exp(jnp.sum(lq1[...] * lk1[...], keepdims=True))
            - jnp.exp(jnp.sum(lq2[...] * lk2[...], keepdims=True)) + lam_init)


def _softmax_step(s, v, m_ref, l_ref, acc_ref, idx):
    m_old = m_ref[idx]
    m_new = jnp.maximum(m_old, jnp.max(s, axis=-1, keepdims=True))
    alpha = jnp.exp(m_old - m_new)
    p = jnp.exp(s - m_new)
    l_ref[idx] = alpha * l_ref[idx] + jnp.sum(p, axis=-1, keepdims=True)
    acc_ref[idx] = alpha * acc_ref[idx] + jnp.dot(p.astype(BF16), v, preferred_element_type=F32)
    m_ref[idx] = m_new


def _attn_finish(m_ref, l_ref, acc_ref, lam, sw, lam_init, o_ref):
    o = acc_ref[0] / l_ref[0] - lam * (acc_ref[1] / l_ref[1])
    o = o * lax.rsqrt(jnp.mean(o * o, axis=-1, keepdims=True) + SUBLN_EPS) * sw * (1.0 - lam_init)
    o_ref[...] = o.astype(o_ref.dtype)


def _attn_prompt_kernel(q_ref, k_ref, v_ref, b_ref, lq1, lk1, lq2, lk2, sw_ref, o_ref, kb_ref, vb_ref,
                        *, lam_init, nq):
    t = ATT_TILE
    kb_ref[...] = k_ref[...].astype(BF16)
    vb_ref[...] = v_ref[...].astype(BF16)
    lam = _lambda(lq1, lk1, lq2, lk2, lam_init)
    sw = sw_ref[...]
    for qi in range(nq):
        q = q_ref[qi * t:(qi + 1) * t, :]
        n_far = max(qi - 1, 0) * t
        exps, sums = [], []
        for mp in range(2):
            cols = slice(mp * HEAD_DIM, (mp + 1) * HEAD_DIM)
            s = _dot_nt(q[:, cols].astype(BF16), kb_ref[0:(qi + 1) * t, cols]) * ATT_SCALE
            pieces = []
            if n_far:
                pieces.append(s[:, :n_far] + b_ref[mp, 2, 0:1, 0:1])
            if qi >= 1:
                pieces.append(s[:, n_far:n_far + t] + b_ref[mp, 1])
            pieces.append(s[:, qi * t:(qi + 1) * t] + b_ref[mp, 0])
            m = functools.reduce(jnp.maximum, [jnp.max(p, axis=-1, keepdims=True) for p in pieces])
            es = [jnp.exp(p - m) for p in pieces]
            exps.append(es)
            sums.append(functools.reduce(jnp.add, [jnp.sum(e, axis=-1, keepdims=True) for e in es]))
        c1 = 1.0 / sums[0]
        c2 = lam / sums[1]
        o, col = None, 0
        for e1, e2 in zip(exps[0], exps[1]):
            a = (e1 * c1 - e2 * c2).astype(BF16)
            d = jnp.dot(a, vb_ref[col:col + a.shape[1], :], preferred_element_type=F32)
            o = d if o is None else o + d
            col += a.shape[1]
        o = o * lax.rsqrt(jnp.mean(o * o, axis=-1, keepdims=True) + SUBLN_EPS) * sw * (1.0 - lam_init)
        o_ref[qi * t:(qi + 1) * t, :] = o.astype(o_ref.dtype)


def _attn_prompt(q, k, v, rel_bias, lams, subln_w, batch, seq, heads, lam_init):
    t = ATT_TILE
    assert seq % t == 0 and t % CHUNK == 0
    pos = np.arange(t)
    tiles = [_bias_table(rel_bias, pos + dt * t, pos) for dt in range(3)]
    assert np.all(_bucket(np.arange(-3 * t + 1, -t)) == NUM_BUCKETS // 2 - 1)
    bias = jnp.stack(tiles, axis=2)
    hw = 2 * HEAD_DIM
    vec = lambda n: pl.BlockSpec((1, n), lambda b, h: (0, 0))
    seq_spec = lambda: pl.BlockSpec((seq, hw), lambda b, h: (b, h))
    return pl.pallas_call(
        functools.partial(_attn_prompt_kernel, lam_init=lam_init, nq=seq // t),
        grid=(batch, heads),
        in_specs=[seq_spec(), seq_spec(), seq_spec(),
                  pl.BlockSpec((None, 2, 3, t, t), lambda b, h: (h, 0, 0, 0, 0)),
                  vec(HEAD_DIM), vec(HEAD_DIM), vec(HEAD_DIM), vec(HEAD_DIM), vec(hw)],
        out_specs=seq_spec(),
        out_shape=jax.ShapeDtypeStruct((batch * seq, heads * hw), BF16),
        scratch_shapes=[pltpu.VMEM((seq, hw), BF16), pltpu.VMEM((seq, hw), BF16)],
        compiler_params=_params("arbitrary", "arbitrary"),
        name="attn_prompt",
    )(q, k, v, bias, *lams, subln_w)


def _attn_sample_kernel(q_ref, ck_ref, cv_ref, kn_ref, vn_ref, b_ref, bn_ref, lq1, lk1, lq2, lk2, sw_ref, o_ref,
                        m_ref, l_ref, acc_ref, *, lam_init, n_tiles):
    kt_i = pl.program_id(1)

    @pl.when(kt_i == 0)
    def _():
        m_ref[...] = jnp.full(m_ref.shape, NEG, F32)
        l_ref[...] = jnp.zeros(l_ref.shape, F32)
        acc_ref[...] = jnp.zeros(acc_ref.shape, F32)

    def step(k2d, v2d, bias_of_map):
        kb = k2d.astype(BF16)
        vb = v2d.astype(BF16)
        for mp in range(2):
            qm = q_ref[mp].astype(BF16)
            s = _dot_nt(qm, kb[:, mp * HEAD_DIM:(mp + 1) * HEAD_DIM]) * ATT_SCALE + bias_of_map(mp)
            _softmax_step(s, vb, m_ref, l_ref, acc_ref, mp)

    rows = ck_ref.shape[0] * ck_ref.shape[1]
    bi = jnp.where(kt_i == n_tiles - 1, 1, 0)
    step(ck_ref[...].reshape(rows, ck_ref.shape[2]), cv_ref[...].reshape(rows, cv_ref.shape[2]),
         lambda mp: b_ref[bi, mp])

    @pl.when(kt_i == n_tiles - 1)
    def _():
        step(kn_ref[...], vn_ref[...], lambda mp: bn_ref[mp])
        sw = sw_ref[...]
        _attn_finish(m_ref, l_ref, acc_ref, _lambda(lq1, lk1, lq2, lk2, lam_init), sw, lam_init, o_ref)


def _head_expand(bias, heads):
    h, _, tq, tk = bias.shape
    eye = jnp.asarray(np.eye(heads, dtype=bool))
    full = jnp.where(eye[:, None, None, None, :], bias[..., None], NEG)
    return jnp.transpose(full, (1, 0, 2, 3, 4)).reshape(2, h * tq, tk * heads)


def _attn_sample(q, cache_k, cache_v, k_new, v_new, rel_bias, lams, subln_w, lam_init):
    db, past, heads, hw = cache_k.shape
    tq = q.shape[2] // heads
    t = min(ATT_TILE, past)
    assert past % t == 0 and t >= REL_MAX_DISTANCE + tq
    n_tiles = past // t
    q_pos = past + np.arange(tq)
    far = _bias_table(rel_bias, q_pos, np.arange(t))
    assert n_tiles == 1 or np.all(_bucket(np.arange(past - t)[None, :] - q_pos[:, None]) == NUM_BUCKETS // 2 - 1)
    near = _bias_table(rel_bias, q_pos, past - t + np.arange(t))
    bias = jnp.stack([_head_expand(far, heads), _head_expand(near, heads)])
    bias_new = _head_expand(_bias_table(rel_bias, q_pos, q_pos), heads)
    rq = heads * tq
    vec = lambda n: pl.BlockSpec((1, n), lambda b, j: (0, 0))
    return pl.pallas_call(
        functools.partial(_attn_sample_kernel, lam_init=lam_init, n_tiles=n_tiles),
        grid=(db, n_tiles),
        in_specs=[pl.BlockSpec((None, 2, rq, HEAD_DIM), lambda b, j: (b, 0, 0, 0)),
                  pl.BlockSpec((None, t, heads, hw), lambda b, j: (b, j, 0, 0)),
                  pl.BlockSpec((None, t, heads, hw), lambda b, j: (b, j, 0, 0)),
                  pl.BlockSpec((None, tq * heads, hw), lambda b, j: (b, 0, 0)),
                  pl.BlockSpec((None, tq * heads, hw), lambda b, j: (b, 0, 0)),
                  pl.BlockSpec((2, 2, rq, t * heads), lambda b, j: (0, 0, 0, 0)),
                  pl.BlockSpec((2, rq, tq * heads), lambda b, j: (0, 0, 0)),
                  vec(HEAD_DIM), vec(HEAD_DIM), vec(HEAD_DIM), vec(HEAD_DIM), vec(hw)],
        out_specs=pl.BlockSpec((None, rq, hw), lambda b, j: (b, 0, 0)),
        out_shape=jax.ShapeDtypeStruct((db, rq, hw), BF16),
        scratch_shapes=[pltpu.VMEM((2, rq, 1), F32), pltpu.VMEM((2, rq, 1), F32), pltpu.VMEM((2, rq, hw), F32)],
        compiler_params=_params("arbitrary", "arbitrary"),
        name="attn_sample",
    )(q, cache_k, cache_v, k_new, v_new, bias, bias_new, *lams, subln_w)


CONV_HIST = 32


def _conv_kernel(cur_ref, hist_ref, w_ref, b_ref, g_ref, beta_ref, o_ref, xs_ref, wb_ref, *, width, zero_first):
    nb, tt, c = cur_ref.shape
    rows = 8
    length = CONV_HIST + tt

    @pl.when(jnp.logical_and(pl.program_id(0) == 0, pl.program_id(1) == 0))
    def _():
        for tap in range(width):
            wb_ref[tap] = jnp.broadcast_to(w_ref[tap:tap + 1, :], (rows, c))

    hist = hist_ref[...]
    if zero_first:
        hist = jnp.where(pl.program_id(1) == 0, 0.0, hist)
    xs_ref[0, :, 0:CONV_HIST, :] = hist
    xs_ref[0, :, CONV_HIST:, :] = cur_ref[...]
    for s in range(1, rows):
        xs_ref[s, :, 0:length - rows, :] = xs_ref[0, :, s:s + length - rows, :]
    lead = CONV_HIST - (width - 1)
    for n in range(nb):
        for r0 in range(0, tt, rows):
            acc = jnp.zeros((rows, c), F32) + b_ref[...]
            for tap in range(width):
                s = (lead + tap) % rows
                a = lead + tap - s + r0
                acc = acc + xs_ref[s, n, a:a + rows, :] * wb_ref[tap]
            mu = jnp.mean(acc, axis=-1, keepdims=True)
            cen = acc - mu
            var = jnp.mean(cen * cen, axis=-1, keepdims=True)
            y = cen * lax.rsqrt(var + LN_EPS) * g_ref[...] + beta_ref[...]
            o_ref[n, r0:r0 + rows, :] = _silu(y).astype(o_ref.dtype)


def _conv(cur3, hist3, hist_index, nb, tt, w, b, g, beta, zero_first):
    nseq, seq, c = cur3.shape
    width = w.shape[0]
    assert width - 1 <= CONV_HIST and seq % tt == 0 and nseq % nb == 0 and tt % 8 == 0
    vec = lambda: pl.BlockSpec((1, c), lambda s, i: (0, 0))
    return pl.pallas_call(
        functools.partial(_conv_kernel, width=width, zero_first=zero_first),
        grid=(nseq // nb, seq // tt),
        in_specs=[pl.BlockSpec((nb, tt, c), lambda s, i: (s, i, 0)),
                  pl.BlockSpec((nb, CONV_HIST, c), hist_index),
                  pl.BlockSpec((width, c), lambda s, i: (0, 0)),
                  vec(), vec(), vec()],
        out_specs=pl.BlockSpec((nb, tt, c), lambda s, i: (s, i, 0)),
        out_shape=jax.ShapeDtypeStruct(cur3.shape, BF16),
        scratch_shapes=[pltpu.VMEM((8, nb, CONV_HIST + tt, c), F32), pltpu.VMEM((width, 8, c), F32)],
        compiler_params=_params("arbitrary", "arbitrary"),
        name="conv",
    )(cur3, hist3, w, b, g, beta)


def _first_argmax(v, iota, axis, size):
    m = jnp.max(v, axis=axis, keepdims=True)
    i = jnp.min(jnp.where(v == m, iota, size), axis=axis, keepdims=True)
    return m, i


def _router_kernel(h_ref, w_ref, bias_ref, idx_ref, gate_ref, rank_ref, cnt_ref, carry_ref):
    n_exp = w_ref.shape[0]
    tm = h_ref.shape[0]
    per_group = n_exp // N_EXPERT_GROUPS

    @pl.when(pl.program_id(0) == 0)
    def _():
        carry_ref[...] = jnp.zeros(carry_ref.shape, F32)

    w = w_ref[...]
    w1 = w.astype(BF16)
    r1 = w - w1.astype(F32)
    w2 = r1.astype(BF16)
    w3 = (r1 - w2.astype(F32)).astype(BF16)
    h = h_ref[...]
    logits = _dot_nt(w1, h) + _dot_nt(w2, h) + _dot_nt(w3, h)
    scores = jax.nn.sigmoid(logits)
    choice = scores + bias_ref[...]

    ch3 = choice.reshape(N_EXPERT_GROUPS, per_group, tm)
    io3 = lax.broadcasted_iota(I32, ch3.shape, 1)
    m1, i1 = _first_argmax(ch3, io3, 1, per_group)
    m2 = jnp.max(jnp.where(io3 == i1, -jnp.inf, ch3), axis=1, keepdims=True)
    grp = (m1 + m2).reshape(N_EXPERT_GROUPS, tm)
    gio = lax.broadcasted_iota(I32, grp.shape, 0)
    keep = jnp.zeros(grp.shape, jnp.bool_)
    for _ in range(TOPK_GROUPS):
        _, gi = _first_argmax(grp, gio, 0, N_EXPERT_GROUPS)
        hit = gio == gi
        keep = jnp.logical_or(keep, hit)
        grp = jnp.where(hit, -jnp.inf, grp)
    keep3 = jnp.broadcast_to(keep.reshape(N_EXPERT_GROUPS, 1, tm), ch3.shape)
    masked = jnp.where(keep3, ch3, -jnp.inf).reshape(n_exp, tm)

    eio = lax.broadcasted_iota(I32, masked.shape, 0)
    sel = jnp.zeros(masked.shape, jnp.bool_)
    picks, pick_scores = [], []
    for _ in range(TOP_K):
        _, ei = _first_argmax(masked, eio, 0, n_exp)
        hit = eio == ei
        picks.append((ei, hit))
        pick_scores.append(jnp.sum(jnp.where(hit, scores, 0.0), axis=0, keepdims=True))
        sel = jnp.logical_or(sel, hit)
        masked = jnp.where(hit, -jnp.inf, masked)
    total = pick_scores[0]
    for s in pick_scores[1:]:
        total = total + s

    sel_b = jnp.where(sel, 1.0, 0.0).astype(BF16)
    tri = (lax.broadcasted_iota(I32, (tm, tm), 0) < lax.broadcasted_iota(I32, (tm, tm), 1))
    rank = jnp.dot(sel_b, jnp.where(tri, 1.0, 0.0).astype(BF16), preferred_element_type=F32) + carry_ref[...]
    for kk, (ei, hit) in enumerate(picks):
        idx_ref[kk:kk + 1, :] = ei
        gate_ref[kk:kk + 1, :] = pick_scores[kk] / total * ROUTED_SCALE
        rank_ref[kk:kk + 1, :] = jnp.sum(jnp.where(hit, rank, 0.0), axis=0, keepdims=True).astype(I32)
    carry_ref[...] = carry_ref[...] + jnp.sum(jnp.where(sel, 1.0, 0.0), axis=1, keepdims=True)
    cnt_ref[...] = carry_ref[...].astype(I32)


def _router(h2b, w_router_t, bias_col):
    t, d = h2b.shape
    n_exp = w_router_t.shape[0]
    tm = math.gcd(512, t)
    assert t % tm == 0
    tok = lambda: pl.BlockSpec((TOP_K, tm), lambda i: (0, i))
    return pl.pallas_call(
        _router_kernel,
        grid=(t // tm,),
        in_specs=[pl.BlockSpec((tm, d), lambda i: (i, 0)),
                  pl.BlockSpec((n_exp, d), lambda i: (0, 0)),
                  pl.BlockSpec((n_exp, 1), lambda i: (0, 0))],
        out_specs=[tok(), tok(), tok(), pl.BlockSpec((n_exp, 1), lambda i: (0, 0))],
        out_shape=[jax.ShapeDtypeStruct((TOP_K, t), I32), jax.ShapeDtypeStruct((TOP_K, t), F32),
                   jax.ShapeDtypeStruct((TOP_K, t), I32), jax.ShapeDtypeStruct((n_exp, 1), I32)],
        scratch_shapes=[pltpu.VMEM((n_exp, 1), F32)],
        compiler_params=_params("arbitrary"),
        name="router",
    )(h2b, w_router_t, bias_col)


def _dispatch_kernel(rows_ref, h_ref, xs_ref, sem):
    n_tok = h_ref.shape[0]

    def body(t, carry):
        for kk in range(TOP_K):
            pltpu.make_async_copy(h_ref.at[pl.ds(t, 1)], xs_ref.at[pl.ds(rows_ref[kk, t], 1)], sem).start()
        return carry

    lax.fori_loop(0, n_tok, body, 0)
    for _ in range(TOP_K):
        pltpu.make_async_copy(h_ref, xs_ref.at[pl.ds(0, n_tok)], sem).wait()


def _dispatch(rows, h2p, n_rows):
    t, w = h2p.shape
    nt = math.gcd(DISPATCH_ROWS, t)
    assert t % nt == 0 and n_rows >= nt
    rows = jnp.transpose(rows.reshape(TOP_K, t // nt, nt), (1, 0, 2))
    return pl.pallas_call(
        _dispatch_kernel,
        grid=(t // nt,),
        in_specs=[pl.BlockSpec((None, TOP_K, nt), lambda i: (i, 0, 0), memory_space=pltpu.SMEM),
                  pl.BlockSpec((nt, w), lambda i: (i, 0))],
        out_specs=pl.BlockSpec(memory_space=pl.ANY),
        out_shape=jax.ShapeDtypeStruct((n_rows, w), h2p.dtype),
        scratch_shapes=[pltpu.SemaphoreType.DMA(())],
        compiler_params=pltpu.CompilerParams(dimension_semantics=("arbitrary",), has_side_effects=True),
        name="dispatch",
    )(rows, h2p)


def _step_fields(code):
    return code & 15, (code >> 4) & 15, (code >> 8) & 15, code >> 12


def _expert_kernel(e_ref, rb_ref, code_ref, nact_ref, x_ref, wg_hbm, wu_hbm, wd_hbm, y_ref,
                   wgf_ref, wuf_ref, wdf_ref, wgb_ref, wub_ref, wdb_ref, gacc_ref, uacc_ref, hid_ref, sem):
    i = pl.program_id(0)
    p, j, nb, cnt = _step_fields(code_ref[i])
    active = i < nact_ref[0]
    first = jnp.logical_and(active, j == 0)
    rb = x_ref.shape[0]
    row0 = pl.multiple_of(j * rb, rb)
    dk, dn = wgf_ref.shape[1], wdf_ref.shape[2]

    def for_phase_of(step, action):
        ph = code_ref[step] & 15
        e = e_ref[step]
        slot = ph & 1

        @pl.when(ph < EXP_KCHUNKS)
        def _():
            k0 = pl.multiple_of(ph * dk, dk)
            action(pltpu.make_async_copy(wg_hbm.at[e, pl.ds(k0, dk), :], wgf_ref.at[slot], sem.at[0, slot]))
            action(pltpu.make_async_copy(wu_hbm.at[e, pl.ds(k0, dk), :], wuf_ref.at[slot], sem.at[1, slot]))

        @pl.when(ph >= EXP_KCHUNKS)
        def _():
            n0 = pl.multiple_of((ph - EXP_KCHUNKS) * dn, dn)
            action(pltpu.make_async_copy(wd_hbm.at[e, :, pl.ds(n0, dn)], wdf_ref.at[slot], sem.at[2, slot]))

    @pl.when(i == 0)
    def _():
        for_phase_of(0, lambda c: c.start())

    @pl.when(first)
    def _():
        nxt = i + nb
        @pl.when(nxt < nact_ref[0])
        def _():
            for_phase_of(nxt, lambda c: c.start())

        for_phase_of(i, lambda c: c.wait())
        slot = p & 1

        @pl.when(p < EXP_KCHUNKS)
        def _():
            wgb_ref[...] = wgf_ref[slot].astype(BF16)
            wub_ref[...] = wuf_ref[slot].astype(BF16)

        @pl.when(p >= EXP_KCHUNKS)
        def _():
            wdb_ref[...] = wdf_ref[slot].astype(BF16)

    @pl.when(jnp.logical_and(active, p < EXP_KCHUNKS))
    def _():
        rows = lax.broadcasted_iota(I32, (rb, 1), 0)
        lo, hi = _unpack_halves(jnp.where(rows < cnt, x_ref[...], jnp.uint32(0)))
        half = x_ref.shape[1]
        g = (jnp.dot(lo, wgb_ref[:half, :], preferred_element_type=F32)
             + jnp.dot(hi, wgb_ref[half:, :], preferred_element_type=F32))
        u = (jnp.dot(lo, wub_ref[:half, :], preferred_element_type=F32)
             + jnp.dot(hi, wub_ref[half:, :], preferred_element_type=F32))

        @pl.when(p == 0)
        def _():
            gacc_ref[pl.ds(row0, rb), :] = g
            uacc_ref[pl.ds(row0, rb), :] = u

        @pl.when(p > 0)
        def _():
            gacc_ref[pl.ds(row0, rb), :] += g
            uacc_ref[pl.ds(row0, rb), :] += u

        @pl.when(p == EXP_KCHUNKS - 1)
        def _():
            hid_ref[pl.ds(row0, rb), :] = (_silu(gacc_ref[pl.ds(row0, rb), :])
                                           * uacc_ref[pl.ds(row0, rb), :]).astype(BF16)

    @pl.when(jnp.logical_and(active, p >= EXP_KCHUNKS))
    def _():
        y_ref[...] = jnp.dot(hid_ref[pl.ds(row0, rb), :], wdb_ref[...], preferred_element_type=F32)


def _expert_steps(counts, n_blocks_max):
    n_exp = counts.shape[0]
    phases = EXP_KCHUNKS + EXP_NCHUNKS
    nb_e = (counts + EXP_ROWS - 1) // EXP_ROWS
    blk_end_e = jnp.cumsum(nb_e)
    blk_start_e = blk_end_e - nb_e
    nseg_e = (nb_e + SEG_BLOCKS - 1) // SEG_BLOCKS
    seg_end_e = jnp.cumsum(nseg_e)
    n_seg_max = n_exp + n_blocks_max // SEG_BLOCKS
    sid = jnp.arange(n_seg_max, dtype=I32)
    seg_e = jnp.minimum(jnp.searchsorted(seg_end_e, sid, side="right"), n_exp - 1).astype(I32)
    part = sid - (seg_end_e - nseg_e)[seg_e]
    seg_nb = jnp.clip(nb_e[seg_e] - part * SEG_BLOCKS, 0, SEG_BLOCKS)
    seg_blk0 = blk_start_e[seg_e] + part * SEG_BLOCKS
    step_end = jnp.cumsum(seg_nb) * phases
    n_active = step_end[-1]
    n_steps_max = n_blocks_max * phases
    step = jnp.minimum(jnp.arange(n_steps_max, dtype=I32), n_active - 1)
    s = jnp.minimum(jnp.searchsorted(step_end, step, side="right"), n_seg_max - 1).astype(I32)
    nb = jnp.maximum(seg_nb[s], 1)
    local = step - (step_end[s] - seg_nb[s] * phases)
    p, j = local // nb, local % nb
    e = seg_e[s]
    rb = seg_blk0[s] + j
    cnt = jnp.clip(counts[e] - (part[s] * SEG_BLOCKS + j) * EXP_ROWS, 0, EXP_ROWS)
    code = p | (j << 4) | (nb << 8) | (cnt << 12)
    return (e.astype(I32), rb.astype(I32), code.astype(I32), n_active.reshape(1).astype(I32),
            (blk_start_e * EXP_ROWS).astype(I32))


def _experts(e_tab, rb_tab, code_tab, n_active, xs, w_gate, w_up, w_down):
    n_exp, d, ff = w_gate.shape
    kc, nc = EXP_KCHUNKS, EXP_NCHUNKS
    dk, dn = d // kc, d // nc
    r = EXP_ROWS

    def x_map(i, e, rb, code, n):
        p, j, nb, _ = _step_fields(code[i])
        last = rb[i] - j + nb - 1
        return jnp.where(p < kc, rb[i], last), jnp.minimum(p, kc - 1)

    def y_map(i, e, rb, code, n):
        p, j, _, _ = _step_fields(code[i])
        return jnp.where(p < kc, rb[i] - j, rb[i]), jnp.maximum(p - kc, 0)

    assert (kc + nc) % 2 == 0
    hbm = lambda: pl.BlockSpec(memory_space=pl.ANY)
    grid_spec = pltpu.PrefetchScalarGridSpec(
        num_scalar_prefetch=4,
        grid=(e_tab.shape[0],),
        in_specs=[pl.BlockSpec((r, dk // 2), x_map), hbm(), hbm(), hbm()],
        out_specs=pl.BlockSpec((r, dn), y_map),
        scratch_shapes=[pltpu.VMEM((2, dk, ff), F32), pltpu.VMEM((2, dk, ff), F32), pltpu.VMEM((2, ff, dn), F32),
                        pltpu.VMEM((dk, ff), BF16), pltpu.VMEM((dk, ff), BF16), pltpu.VMEM((ff, dn), BF16),
                        pltpu.VMEM((SEG_BLOCKS * r, ff), F32), pltpu.VMEM((SEG_BLOCKS * r, ff), F32),
                        pltpu.VMEM((SEG_BLOCKS * r, ff), BF16),
                        pltpu.SemaphoreType.DMA((3, 2))],
    )
    return pl.pallas_call(
        _expert_kernel,
        grid_spec=grid_spec,
        out_shape=jax.ShapeDtypeStruct((xs.shape[0], d), F32),
        compiler_params=_params("arbitrary"),
        name="experts",
    )(e_tab, rb_tab, code_tab, n_active, xs, w_gate, w_up, w_down)


def _combine_kernel(rows_ref, gate_ref, hs_ref, wsd_ref, ys_ref, f_ref, buf_ref, wb_ref, sem):
    n_tok = gate_ref.shape[0]

    @pl.when(pl.program_id(0) == 0)
    def _():
        wb_ref[...] = wsd_ref[...].astype(BF16)

    def body(t, carry):
        for kk in range(TOP_K):
            pltpu.make_async_copy(ys_ref.at[pl.ds(rows_ref[kk, t], 1)], buf_ref.at[kk, pl.ds(t, 1)], sem).start()
        return carry

    lax.fori_loop(0, n_tok, body, 0)
    shared = jnp.dot(hs_ref[...], wb_ref[...], preferred_element_type=F32)
    for kk in range(TOP_K):
        pltpu.make_async_copy(ys_ref.at[pl.ds(0, n_tok)], buf_ref.at[kk], sem).wait()
    gates = gate_ref[...]
    acc = shared
    for kk in range(TOP_K):
        acc = acc + buf_ref[kk] * gates[:, kk:kk + 1]
    f_ref[...] = acc


def _combine(rows, gates_t, hid_sh, w_sh_down, ys):
    t, ff = hid_sh.shape
    d = w_sh_down.shape[1]
    nt = math.gcd(COMBINE_ROWS, t)
    assert t % nt == 0
    rows = jnp.transpose(rows.reshape(TOP_K, t // nt, nt), (1, 0, 2))
    return pl.pallas_call(
        _combine_kernel,
        grid=(t // nt,),
        in_specs=[pl.BlockSpec((None, TOP_K, nt), lambda i: (i, 0, 0), memory_space=pltpu.SMEM),
                  pl.BlockSpec((nt, TOP_K), lambda i: (i, 0)),
                  pl.BlockSpec((nt, ff), lambda i: (i, 0)),
                  pl.BlockSpec((ff, d), lambda i: (0, 0)),
                  pl.BlockSpec(memory_space=pl.ANY)],
        out_specs=pl.BlockSpec((nt, d), lambda i: (i, 0)),
        out_shape=jax.ShapeDtypeStruct((t, d), F32),
        scratch_shapes=[pltpu.VMEM((TOP_K, nt, d), F32), pltpu.VMEM((ff, d), BF16), pltpu.SemaphoreType.DMA(())],
        compiler_params=_params("arbitrary"),
        name="combine",
    )(rows, gates_t, hid_sh, w_sh_down, ys)


def _layer(l, lam_init, x_prompt, x_sample, c_prompt, c_sample, cache_k, cache_v, state_conv, rel_bias, p):
    batch, seq, d = x_prompt.shape
    db, dseq, _ = x_sample.shape
    past, heads = cache_k.shape[1], cache_k.shape[2]
    hw = 2 * HEAD_DIM
    aw = heads * hw
    cc = p["conv_dw_w"].shape[1]
    width = p["conv_dw_w"].shape[0]
    assert p["w_in"].shape[1] == 3 * aw + 2 * cc and seq % dseq == 0
    tp, ts = batch * seq, db * dseq
    row = lambda v: v.reshape(1, -1)

    n_mod_rows = -(-(db + batch) // 16) * 16
    c_all = jnp.concatenate([c_sample, c_prompt, jnp.zeros((n_mod_rows - db - batch, d), F32)], axis=0)
    mod3 = _ada(c_all, p["w_ada"], row(p["b_ada"])).reshape(n_mod_rows, 1, N_MOD * d)

    gp = tp // dseq
    gps = seq // dseq
    nb_p = math.gcd(ROW_GROUPS, gps)
    nb_s = math.gcd(ROW_GROUPS, db)
    assert gps % nb_p == 0 and db % nb_s == 0 and gp % nb_s == 0
    plan_p = _RowPlan(gp, dseq, d, nb_p, 1, lambda i: db + (i * nb_p) // gps, 0)
    plan_s = _RowPlan(db, dseq, d, nb_s, nb_s, lambda i: i, gp // nb_s)
    xp3 = x_prompt.reshape(gp, dseq, d)

    hp = _prenorm(plan_p, xp3, row(p["g_pre_mix"]), mod3).reshape(tp, d)
    hs = _prenorm(plan_s, x_sample, row(p["g_pre_mix"]), mod3).reshape(ts, d)

    w_in = p["w_in"]
    proj = lambda h, off, nm: _mm([h], [(w_in, off)], aw, _identity, F32, 512, 512, nm)
    qp, kp, vp = proj(hp, 0, "q_prompt"), proj(hp, aw, "k_prompt"), proj(hp, 2 * aw, "v_prompt")
    qs, ks, vs = proj(hs, 0, "q_sample"), proj(hs, aw, "k_sample"), proj(hs, 2 * aw, "v_sample")
    glu_cols = [(w_in, 3 * aw), (w_in, 3 * aw + cc)]
    glu_p = _mm([hp], glu_cols, cc, _glu, F32, 512, 256, "glu_prompt")
    glu_s = _mm([hs], glu_cols, cc, _glu, F32, 512, 256, "glu_sample")

    lams = [row(p[n]) for n in ("lambda_q1", "lambda_k1", "lambda_q2", "lambda_k2")]
    subln = row(p["subln_w"])
    attn_p = _attn_prompt(qp, kp, vp, rel_bias, lams, subln, batch, seq, heads, lam_init)

    new_k_s = ks.reshape(db, dseq, heads, hw)
    new_v_s = vs.reshape(db, dseq, heads, hw)
    q_s = jnp.transpose(qs.reshape(db, dseq, heads, 2, HEAD_DIM), (0, 3, 2, 1, 4)).reshape(db, 2, heads * dseq, HEAD_DIM)
    o_s = _attn_sample(q_s, cache_k, cache_v, new_k_s.reshape(db, dseq * heads, hw),
                       new_v_s.reshape(db, dseq * heads, hw), rel_bias, lams, subln, lam_init)
    attn_s = jnp.transpose(o_s.reshape(db, heads, dseq, hw), (0, 2, 1, 3)).reshape(ts, aw)

    conv_args = (p["conv_dw_w"], row(p["conv_dw_b"]), row(p["conv_ln_g"]), row(p["conv_ln_b"]))
    glu_p3 = glu_p.reshape(batch, seq, cc)
    tt = min(128, seq)
    per = tt // CONV_HIST
    conv_p = _conv(glu_p3, glu_p3, lambda s, i: (s, jnp.maximum(i * per - 1, 0), 0), 1, tt, *conv_args,
                   zero_first=True).reshape(tp, cc)
    glu_s3 = glu_s.reshape(db, dseq, cc)
    hist_s = jnp.concatenate([jnp.zeros((db, CONV_HIST - (width - 1), cc), F32), state_conv], axis=1)
    nb_c = math.gcd(2, db)
    conv_s = _conv(glu_s3, hist_s, lambda s, i: (s, 0, 0), nb_c, dseq, *conv_args, zero_first=False).reshape(ts, cc)
    new_conv_p = glu_p3[:, seq - (width - 1):]
    new_conv_s = jnp.concatenate([state_conv, glu_s3], axis=1)[:, -(width - 1):]

    mix_p = _mm([attn_p, conv_p], [(p["w_out"], 0)], d, _identity, F32, 512, 512, "out_prompt")
    mix_s = _mm([attn_s, conv_s], [(p["w_out"], 0)], d, _identity, F32, 512, 512, "out_sample")

    gt = gp + db
    gpost, gpre = row(p["g_post_mix"]), row(p["g_pre_ffn"])
    x1p, h2ba, h2pa = _postmix(plan_p, xp3, mix_p.reshape(gp, dseq, d), gpost, gpre, mod3, gt)
    x1s, h2ba, h2pa = _postmix(plan_s, x_sample, mix_s.reshape(db, dseq, d), gpost, gpre, mod3, gt,
                               prev=(h2ba, h2pa))
    t = tp + ts
    h2b = h2ba.reshape(t, d)
    h2p = h2pa.reshape(t, d // 2)

    n_exp = p["w_router"].shape[1]
    idx, gates, rank, counts = _router(h2b, p["w_router"].T, p["b_router_corr"].reshape(n_exp, 1))
    n_blocks_max = (t * TOP_K) // EXP_ROWS + n_exp
    e_tab, rb_tab, code_tab, n_active, row_start = _expert_steps(counts.reshape(n_exp), n_blocks_max)
    pick = idx[..., None] == jnp.arange(n_exp, dtype=I32)
    rows = jnp.sum(jnp.where(pick, row_start, 0), axis=-1) + rank

    xs = _dispatch(rows.astype(I32), h2p, n_blocks_max * EXP_ROWS)
    ys = _experts(e_tab, rb_tab, code_tab, n_active, xs, p["w_exp_gate"], p["w_exp_up"], p["w_exp_down"])
    ff_sh = p["w_sh_gate"].shape[1]
    hid_sh = _mm([h2b], [(p["w_sh_gate"], 0), (p["w_sh_up"], 0)], ff_sh, _swiglu, BF16, 512, 256, "shared_up")
    f = _combine(rows.astype(I32), gates.T, hid_sh, p["w_sh_down"], ys)

    f3 = f.reshape(gt, dseq, d)
    gpf = row(p["g_post_ffn"])
    yp = _final(plan_p, x1p, f3, gpf, mod3).reshape(batch, seq, d)
    ysmp = _final(plan_s, x1s, f3, gpf, mod3)
    new_k_p = kp.reshape(batch, seq, heads, hw)
    new_v_p = vp.reshape(batch, seq, heads, hw)
    return yp, ysmp, new_k_p, new_v_p, new_conv_p, new_k_s, new_v_s, new_conv_s


def kernel(x_prompt, x_sample, c_prompt, c_sample, cache_k, cache_v, state_conv, rel_bias, w_ada, b_ada, g_pre_mix, g_post_mix, g_pre_ffn, g_post_ffn, w_in, lambda_q1, lambda_k1, lambda_q2, lambda_k2, subln_w, conv_dw_w, conv_dw_b, conv_ln_g, conv_ln_b, w_out, w_router, b_router_corr, w_exp_gate, w_exp_up, w_exp_down, w_sh_gate, w_sh_up, w_sh_down):
    weights = dict(w_ada=w_ada, b_ada=b_ada, g_pre_mix=g_pre_mix, g_post_mix=g_post_mix, g_pre_ffn=g_pre_ffn,
                   g_post_ffn=g_post_ffn, w_in=w_in, lambda_q1=lambda_q1, lambda_k1=lambda_k1,
                   lambda_q2=lambda_q2, lambda_k2=lambda_k2, subln_w=subln_w, conv_dw_w=conv_dw_w,
                   conv_dw_b=conv_dw_b, conv_ln_g=conv_ln_g, conv_ln_b=conv_ln_b, w_out=w_out,
                   w_router=w_router, b_router_corr=b_router_corr, w_exp_gate=w_exp_gate, w_exp_up=w_exp_up,
                   w_exp_down=w_exp_down, w_sh_gate=w_sh_gate, w_sh_up=w_sh_up, w_sh_down=w_sh_down)
    depth = w_in.shape[0]
    xp, xs = x_prompt, x_sample
    outs = [[] for _ in range(6)]
    for l in range(depth):
        p = {k: (v.reshape(v.shape[1:]) if depth == 1 else v[l]) for k, v in weights.items()}
        lam_init = 0.8 - 0.6 * math.exp(-0.3 * l)
        ck, cv, sc = ((a.reshape(a.shape[1:]) if depth == 1 else a[l]) for a in (cache_k, cache_v, state_conv))
        xp, xs, *state = _layer(l, lam_init, xp, xs, c_prompt, c_sample, ck, cv, sc, rel_bias, p)
        for acc, s in zip(outs, state):
            acc.append(s)
    return (xp, xs) + tuple(jnp.stack(o) for o in outs)
```

```python
import functools
import math

import numpy as np
import jax
import jax.numpy as jnp
from jax import lax
from jax.experimental import pallas as pl
from jax.experimental.pallas import tpu as pltpu

F32 = jnp.float32
BF16 = jnp.bfloat16
I32 = jnp.int32
U32 = jnp.uint32

CHUNK = 64
HEAD_DIM = 128
NUM_BUCKETS = 32
REL_MAX_DISTANCE = 128
TOP_K = 8
N_EXPERT_GROUPS = 8
TOPK_GROUPS = 4
ROUTED_SCALE = 2.5
RMS_EPS = 1e-6
SUBLN_EPS = 1e-5
LN_EPS = 1e-5
N_MOD = 6
NEG = -1e30
ATT_SCALE = HEAD_DIM ** -0.5

VMEM_LIMIT_BYTES = 56 * 1024 * 1024
ATT_TILE = 256
EXP_ROWS = 128
SEG_BLOCKS = 4
EXP_KCHUNKS = 2
EXP_NCHUNKS = 2
COMBINE_ROWS = 128
DISPATCH_ROWS = 256
ROW_GROUPS = 4


def _params(*sem):
    return pltpu.CompilerParams(dimension_semantics=sem, vmem_limit_bytes=VMEM_LIMIT_BYTES)


def _silu(x):
    return x * jax.nn.sigmoid(x)


def _dot_nt(a, b):
    return lax.dot_general(a, b, (((1,), (1,)), ((), ())), preferred_element_type=F32)


def _ada_kernel(c_ref, w_ref, b_ref, o_ref):
    a = _silu(c_ref[...]).astype(BF16)
    o_ref[...] = jnp.dot(a, w_ref[...].astype(BF16), preferred_element_type=F32) + b_ref[...]


def _ada(c_all, w_ada, b_ada):
    rows, d = c_all.shape
    n = w_ada.shape[1]
    tn = min(512, n)
    return pl.pallas_call(
        _ada_kernel,
        grid=(n // tn,),
        in_specs=[pl.BlockSpec((rows, d), lambda j: (0, 0)),
                  pl.BlockSpec((d, tn), lambda j: (0, j)),
                  pl.BlockSpec((1, tn), lambda j: (0, j))],
        out_specs=pl.BlockSpec((rows, tn), lambda j: (0, j)),
        out_shape=jax.ShapeDtypeStruct((rows, n), F32),
        compiler_params=_params("arbitrary"),
        name="ada",
    )(c_all, w_ada, b_ada)


class _RowPlan:
    def __init__(self, n_groups, group_rows, d, nb, mod_nb, mod_index, out_block_offset=0):
        self.n_groups, self.group_rows, self.d, self.nb = n_groups, group_rows, d, nb
        self.mod_nb, self.mod_index, self.out_block_offset = mod_nb, mod_index, out_block_offset
        self.grid = (n_groups // nb,)

    def act(self, offset=0):
        return pl.BlockSpec((self.nb, self.group_rows, self.d), lambda i: (i + offset, 0, 0))

    def mod(self, chunk):
        return pl.BlockSpec((self.mod_nb, 1, self.d), lambda i: (self.mod_index(i), 0, chunk))

    def vec(self):
        return pl.BlockSpec((1, self.d), lambda i: (0, 0))


def _rms(x, g, eps):
    return x * lax.rsqrt(jnp.mean(x * x, axis=-1, keepdims=True) + eps) * g


def _prenorm_kernel(x_ref, g_ref, sh_ref, sc_ref, o_ref):
    y = _rms(x_ref[...], g_ref[...], RMS_EPS)
    o_ref[...] = (y * (1.0 + sc_ref[...]) + sh_ref[...]).astype(o_ref.dtype)


def _prenorm(plan, x3, g, mod3):
    return pl.pallas_call(
        _prenorm_kernel,
        grid=plan.grid,
        in_specs=[plan.act(), plan.vec(), plan.mod(0), plan.mod(1)],
        out_specs=plan.act(),
        out_shape=jax.ShapeDtypeStruct(x3.shape, BF16),
        compiler_params=_params("arbitrary"),
        name="prenorm",
    )(x3, g, mod3, mod3)


def _pack_halves(x):
    n = x.shape[-1] // 2
    lo = lax.bitcast_convert_type(x[..., :n].astype(BF16).astype(F32), U32) >> 16
    hi = lax.bitcast_convert_type(x[..., n:].astype(BF16).astype(F32), U32) & jnp.uint32(0xFFFF0000)
    return hi | lo


def _unpack_halves(w):
    lo = lax.bitcast_convert_type(w << 16, F32).astype(BF16)
    hi = lax.bitcast_convert_type(w & jnp.uint32(0xFFFF0000), F32).astype(BF16)
    return lo, hi


def _postmix_kernel(x_ref, mix_ref, gpost_ref, gpre_ref, g1_ref, sh2_ref, sc2_ref, *rest):
    x1_ref, h2b_ref, h2p_ref = rest[-3:]
    x1 = x_ref[...] + g1_ref[...] * _rms(mix_ref[...], gpost_ref[...], RMS_EPS)
    x1_ref[...] = x1
    h2 = _rms(x1, gpre_ref[...], RMS_EPS) * (1.0 + sc2_ref[...]) + sh2_ref[...]
    h2b_ref[...] = h2.astype(BF16)
    dk = h2.shape[-1] // EXP_KCHUNKS
    for c in range(EXP_KCHUNKS):
        h2p_ref[:, :, c * (dk // 2):(c + 1) * (dk // 2)] = _pack_halves(h2[:, :, c * dk:(c + 1) * dk])


def _postmix(plan, x3, mix3, gpost, gpre, mod3, total_groups, prev=None):
    off = plan.out_block_offset
    in_specs = [plan.act(), plan.act(), plan.vec(), plan.vec(), plan.mod(2), plan.mod(3), plan.mod(4)]
    args = [x3, mix3, gpost, gpre, mod3, mod3, mod3]
    aliases = {}
    if prev is not None:
        in_specs += [pl.BlockSpec(memory_space=pl.ANY), pl.BlockSpec(memory_space=pl.ANY)]
        args += list(prev)
        aliases = {7: 1, 8: 2}
    shape_all = (total_groups, plan.group_rows, plan.d)
    shape_packed = (total_groups, plan.group_rows, plan.d // 2)
    packed_spec = pl.BlockSpec((plan.nb, plan.group_rows, plan.d // 2), lambda i: (i + off, 0, 0))
    return pl.pallas_call(
        _postmix_kernel,
        grid=plan.grid,
        in_specs=in_specs,
        out_specs=[plan.act(), plan.act(off), packed_spec],
        out_shape=[jax.ShapeDtypeStruct(x3.shape, F32),
                   jax.ShapeDtypeStruct(shape_all, BF16),
                   jax.ShapeDtypeStruct(shape_packed, U32)],
        input_output_aliases=aliases,
        compiler_params=_params("arbitrary"),
        name="postmix",
    )(*args)


def _final_kernel(x1_ref, f_ref, gpost_ref, g2_ref, o_ref):
    o_ref[...] = x1_ref[...] + g2_ref[...] * _rms(f_ref[...], gpost_ref[...], RMS_EPS)


def _final(plan, x13, f3, gpost, mod3):
    return pl.pallas_call(
        _final_kernel,
        grid=plan.grid,
        in_specs=[plan.act(), plan.act(plan.out_block_offset), plan.vec(), plan.mod(5)],
        out_specs=plan.act(),
        out_shape=jax.ShapeDtypeStruct(x13.shape, F32),
        compiler_params=_params("arbitrary"),
        name="final",
    )(x13, f3, gpost, mod3)


def _mm_kernel(*refs, n_a, n_w, k_sizes, epilogue):
    a_refs = refs[:n_a]
    w_refs = refs[n_a:n_a + n_w]
    o_ref = refs[n_a + n_w]
    wb_refs = refs[n_a + n_w + 1:]

    @pl.when(pl.program_id(1) == 0)
    def _():
        for w_ref, wb_ref in zip(w_refs, wb_refs):
            wb_ref[...] = w_ref[...].astype(BF16)

    parts = []
    for wb_ref in wb_refs:
        acc, k0 = None, 0
        for a_ref, ka in zip(a_refs, k_sizes):
            d = jnp.dot(a_ref[...], wb_ref[k0:k0 + ka, :], preferred_element_type=F32)
            acc = d if acc is None else acc + d
            k0 += ka
        parts.append(acc)
    o_ref[...] = epilogue(*parts).astype(o_ref.dtype)


def _mm(a_list, w_cols, n_out, epilogue, out_dtype, tm, tn, name):
    m = a_list[0].shape[0]
    k = w_cols[0][0].shape[0]
    tm, tn = math.gcd(tm, m), math.gcd(tn, n_out)
    k_sizes = tuple(a.shape[1] for a in a_list)
    assert sum(k_sizes) == k and m % tm == 0 and n_out % tn == 0 and all(c % tn == 0 for _, c in w_cols)
    a_specs = [pl.BlockSpec((tm, ka), lambda j, i: (i, 0)) for ka in k_sizes]
    w_specs = [pl.BlockSpec((k, tn), functools.partial(lambda j, i, o: (0, o + j), o=c // tn)) for _, c in w_cols]
    kern = functools.partial(_mm_kernel, n_a=len(a_list), n_w=len(w_cols), k_sizes=k_sizes, epilogue=epilogue)
    return pl.pallas_call(
        kern,
        grid=(n_out // tn, m // tm),
        in_specs=a_specs + w_specs,
        out_specs=pl.BlockSpec((tm, tn), lambda j, i: (i, j)),
        out_shape=jax.ShapeDtypeStruct((m, n_out), out_dtype),
        scratch_shapes=[pltpu.VMEM((k, tn), BF16) for _ in w_cols],
        compiler_params=_params("arbitrary", "arbitrary"),
        name=name,
    )(*a_list, *[w for w, _ in w_cols])


def _identity(x):
    return x


def _glu(a, b):
    return a * jax.nn.sigmoid(b)


def _swiglu(a, b):
    return _silu(a) * b


def _bucket(rel):
    half = NUM_BUCKETS // 2
    max_exact = half // 2
    n = np.abs(rel)
    nf = np.maximum(n, 1).astype(np.float32)
    large = max_exact + (np.log(nf / np.float32(max_exact)) / np.float32(math.log(REL_MAX_DISTANCE / max_exact))
                         * np.float32(half - max_exact)).astype(np.int32)
    large = np.minimum(large, half - 1)
    return np.where(rel > 0, half, 0) + np.where(n < max_exact, n, large)


def _bias_table(rel_bias, q_pos, k_pos):
    rel = k_pos[None, :] - q_pos[:, None]
    visible = (k_pos // CHUNK)[None, :] <= (q_pos // CHUNK)[:, None]
    onehot = (jnp.asarray(_bucket(rel))[..., None] == jnp.arange(NUM_BUCKETS)).astype(F32)
    b = jnp.einsum("qkb,bhm->hmqk", onehot, rel_bias.astype(F32), precision=lax.Precision.HIGHEST)
    return jnp.where(jnp.asarray(visible)[None, None], b, NEG)


def _lambda(lq1, lk1, lq2, lk2, lam_init):
    return (jnp.exp(jnp.sum(lq1[...] * lk1[...], keepdims=True))
            - jnp.exp(jnp.sum(lq2[...] * lk2[...], keepdims=True)) + lam_init)


def _softmax_step(s, v, m_ref, l_ref, acc_ref, idx):
    m_old = m_ref[idx]
    m_new = jnp.maximum(m_old, jnp.max(s, axis=-1, keepdims=True))
    alpha = jnp.exp(m_old - m_new)
    p = jnp.exp(s - m_new)
    l_ref[idx] = alpha * l_ref[idx] + jnp.sum(p, axis=-1, keepdims=True)
    acc_ref[idx] = alpha * acc_ref[idx] + jnp.dot(p.astype(BF16), v, preferred_element_type=F32)
    m_ref[idx] = m_new


def _attn_finish(m_ref, l_ref, acc_ref, lam, sw, lam_init, o_ref):
    o = acc_ref[0] / l_ref[0] - lam * (acc_ref[1] / l_ref[1])
    o = o * lax.rsqrt(jnp.mean(o * o, axis=-1, keepdims=True) + SUBLN_EPS) * sw * (1.0 - lam_init)
    o_ref[...] = o.astype(o_ref.dtype)


def _attn_prompt_kernel(q_ref, k_ref, v_ref, b_ref, lq1, lk1, lq2, lk2, sw_ref, o_ref, kb_ref, vb_ref,
                        *, lam_init, nq):
    t = ATT_TILE
    kb_ref[...] = k_ref[...].astype(BF16)
    vb_ref[...] = v_ref[...].astype(BF16)
    lam = _lambda(lq1, lk1, lq2, lk2, lam_init)
    sw = sw_ref[...]
    for qi in range(nq):
        q = q_ref[qi * t:(qi + 1) * t, :]
        n_far = max(qi - 1, 0) * t
        exps, sums = [], []
        for mp in range(2):
            cols = slice(mp * HEAD_DIM, (mp + 1) * HEAD_DIM)
            s = _dot_nt(q[:, cols].astype(BF16), kb_ref[0:(qi + 1) * t, cols]) * ATT_SCALE
            pieces = []
            if n_far:
                pieces.append(s[:, :n_far] + b_ref[mp, 2, 0:1, 0:1])
            if qi >= 1:
                pieces.append(s[:, n_far:n_far + t] + b_ref[mp, 1])
            pieces.append(s[:, qi * t:(qi + 1) * t] + b_ref[mp, 0])
            m = functools.reduce(jnp.maximum, [jnp.max(p, axis=-1, keepdims=True) for p in pieces])
            es = [jnp.exp(p - m) for p in pieces]
            exps.append(es)
            sums.append(functools.reduce(jnp.add, [jnp.sum(e, axis=-1, keepdims=True) for e in es]))
        c1 = 1.0 / sums[0]
        c2 = lam / sums[1]
        o, col = None, 0
        for e1, e2 in zip(exps[0], exps[1]):
            a = (e1 * c1 - e2 * c2).astype(BF16)
            d = jnp.dot(a, vb_ref[col:col + a.shape[1], :], preferred_element_type=F32)
            o = d if o is None else o + d
            col += a.shape[1]
        o = o * lax.rsqrt(jnp.mean(o * o, axis=-1, keepdims=True) + SUBLN_EPS) * sw * (1.0 - lam_init)
        o_ref[qi * t:(qi + 1) * t, :] = o.astype(o_ref.dtype)


def _attn_prompt(q, k, v, rel_bias, lams, subln_w, batch, seq, heads, lam_init):
    t = ATT_TILE
    assert seq % t == 0 and t % CHUNK == 0
    pos = np.arange(t)
    tiles = [_bias_table(rel_bias, pos + dt * t, pos) for dt in range(3)]
    assert np.all(_bucket(np.arange(-3 * t + 1, -t)) == NUM_BUCKETS // 2 - 1)
    bias = jnp.stack(tiles, axis=2)
    hw = 2 * HEAD_DIM
    vec = lambda n: pl.BlockSpec((1, n), lambda b, h: (0, 0))
    seq_spec = lambda: pl.BlockSpec((seq, hw), lambda b, h: (b, h))
    return pl.pallas_call(
        functools.partial(_attn_prompt_kernel, lam_init=lam_init, nq=seq // t),
        grid=(batch, heads),
        in_specs=[seq_spec(), seq_spec(), seq_spec(),
                  pl.BlockSpec((None, 2, 3, t, t), lambda b, h: (h, 0, 0, 0, 0)),
                  vec(HEAD_DIM), vec(HEAD_DIM), vec(HEAD_DIM), vec(HEAD_DIM), vec(hw)],
        out_specs=seq_spec(),
        out_shape=jax.ShapeDtypeStruct((batch * seq, heads * hw), BF16),
        scratch_shapes=[pltpu.VMEM((seq, hw), BF16), pltpu.VMEM((seq, hw), BF16)],
        compiler_params=_params("arbitrary", "arbitrary"),
        name="attn_prompt",
    )(q, k, v, bias, *lams, subln_w)


def _attn_sample_kernel(q_ref, ck_ref, cv_ref, kn_ref, vn_ref, b_ref, bn_ref, lq1, lk1, lq2, lk2, sw_ref, o_ref,
                        m_ref, l_ref, acc_ref, *, lam_init, n_tiles):
    kt_i = pl.program_id(1)

    @pl.when(kt_i == 0)
    def _():
        m_ref[...] = jnp.full(m_ref.shape, NEG, F32)
        l_ref[...] = jnp.zeros(l_ref.shape, F32)
        acc_ref[...] = jnp.zeros(acc_ref.shape, F32)

    def step(k2d, v2d, bias_of_map):
        kb = k2d.astype(BF16)
        vb = v2d.astype(BF16)
        for mp in range(2):
            qm = q_ref[mp].astype(BF16)
            s = _dot_nt(qm, kb[:, mp * HEAD_DIM:(mp + 1) * HEAD_DIM]) * ATT_SCALE + bias_of_map(mp)
            _softmax_step(s, vb, m_ref, l_ref, acc_ref, mp)

    rows = ck_ref.shape[0] * ck_ref.shape[1]
    bi = jnp.where(kt_i == n_tiles - 1, 1, 0)
    step(ck_ref[...].reshape(rows, ck_ref.shape[2]), cv_ref[...].reshape(rows, cv_ref.shape[2]),
         lambda mp: b_ref[bi, mp])

    @pl.when(kt_i == n_tiles - 1)
    def _():
        step(kn_ref[...], vn_ref[...], lambda mp: bn_ref[mp])
        sw = sw_ref[...]
        _attn_finish(m_ref, l_ref, acc_ref, _lambda(lq1, lk1, lq2, lk2, lam_init), sw, lam_init, o_ref)


def _head_expand(bias, heads):
    h, _, tq, tk = bias.shape
    eye = jnp.asarray(np.eye(heads, dtype=bool))
    full = jnp.where(eye[:, None, None, None, :], bias[..., None], NEG)
    return jnp.transpose(full, (1, 0, 2, 3, 4)).reshape(2, h * tq, tk * heads)


def _attn_sample(q, cache_k, cache_v, k_new, v_new, rel_bias, lams, subln_w, lam_init):
    db, past, heads, hw = cache_k.shape
    tq = q.shape[2] // heads
    t = min(ATT_TILE, past)
    assert past % t == 0 and t >= REL_MAX_DISTANCE + tq
    n_tiles = past // t
    q_pos = past + np.arange(tq)
    far = _bias_table(rel_bias, q_pos, np.arange(t))
    assert n_tiles == 1 or np.all(_bucket(np.arange(past - t)[None, :] - q_pos[:, None]) == NUM_BUCKETS // 2 - 1)
    near = _bias_table(rel_bias, q_pos, past - t + np.arange(t))
    bias = jnp.stack([_head_expand(far, heads), _head_expand(near, heads)])
    bias_new = _head_expand(_bias_table(rel_bias, q_pos, q_pos), heads)
    rq = heads * tq
    vec = lambda n: pl.BlockSpec((1, n), lambda b, j: (0, 0))
    return pl.pallas_call(
        functools.partial(_attn_sample_kernel, lam_init=lam_init, n_tiles=n_tiles),
        grid=(db, n_tiles),
        in_specs=[pl.BlockSpec((None, 2, rq, HEAD_DIM), lambda b, j: (b, 0, 0, 0)),
                  pl.BlockSpec((None, t, heads, hw), lambda b, j: (b, j, 0, 0)),
                  pl.BlockSpec((None, t, heads, hw), lambda b, j: (b, j, 0, 0)),
                  pl.BlockSpec((None, tq * heads, hw), lambda b, j: (b, 0, 0)),
                  pl.BlockSpec((None, tq * heads, hw), lambda b, j: (b, 0, 0)),
                  pl.BlockSpec((2, 2, rq, t * heads), lambda b, j: (0, 0, 0, 0)),
                  pl.BlockSpec((2, rq, tq * heads), lambda b, j: (0, 0, 0)),
                  vec(HEAD_DIM), vec(HEAD_DIM), vec(HEAD_DIM), vec(HEAD_DIM), vec(hw)],
        out_specs=pl.BlockSpec((None, rq, hw), lambda b, j: (b, 0, 0)),
        out_shape=jax.ShapeDtypeStruct((db, rq, hw), BF16),
        scratch_shapes=[pltpu.VMEM((2, rq, 1), F32), pltpu.VMEM((2, rq, 1), F32), pltpu.VMEM((2, rq, hw), F32)],
        compiler_params=_params("arbitrary", "arbitrary"),
        name="attn_sample",
    )(q, cache_k, cache_v, k_new, v_new, bias, bias_new, *lams, subln_w)


CONV_HIST = 32


def _conv_kernel(cur_ref, hist_ref, w_ref, b_ref, g_ref, beta_ref, o_ref, xs_ref, wb_ref, *, width, zero_first):
    nb, tt, c = cur_ref.shape
    rows = 8
    length = CONV_HIST + tt

    @pl.when(jnp.logical_and(pl.program_id(0) == 0, pl.program_id(1) == 0))
    def _():
        for tap in range(width):
            wb_ref[tap] = jnp.broadcast_to(w_ref[tap:tap + 1, :], (rows, c))

    hist = hist_ref[...]
    if zero_first:
        hist = jnp.where(pl.program_id(1) == 0, 0.0, hist)
    xs_ref[0, :, 0:CONV_HIST, :] = hist
    xs_ref[0, :, CONV_HIST:, :] = cur_ref[...]
    for s in range(1, rows):
        xs_ref[s, :, 0:length - rows, :] = xs_ref[0, :, s:s + length - rows, :]
    lead = CONV_HIST - (width - 1)
    for n in range(nb):
        for r0 in range(0, tt, rows):
            acc = jnp.zeros((rows, c), F32) + b_ref[...]
            for tap in range(width):
                s = (lead + tap) % rows
                a = lead + tap - s + r0
                acc = acc + xs_ref[s, n, a:a + rows, :] * wb_ref[tap]
            mu = jnp.mean(acc, axis=-1, keepdims=True)
            cen = acc - mu
            var = jnp.mean(cen * cen, axis=-1, keepdims=True)
            y = cen * lax.rsqrt(var + LN_EPS) * g_ref[...] + beta_ref[...]
            o_ref[n, r0:r0 + rows, :] = _silu(y).astype(o_ref.dtype)


def _conv(cur3, hist3, hist_index, nb, tt, w, b, g, beta, zero_first):
    nseq, seq, c = cur3.shape
    width = w.shape[0]
    assert width - 1 <= CONV_HIST and seq % tt == 0 and nseq % nb == 0 and tt % 8 == 0
    vec = lambda: pl.BlockSpec((1, c), lambda s, i: (0, 0))
    return pl.pallas_call(
        functools.partial(_conv_kernel, width=width, zero_first=zero_first),
        grid=(nseq // nb, seq // tt),
        in_specs=[pl.BlockSpec((nb, tt, c), lambda s, i: (s, i, 0)),
                  pl.BlockSpec((nb, CONV_HIST, c), hist_index),
                  pl.BlockSpec((width, c), lambda s, i: (0, 0)),
                  vec(), vec(), vec()],
        out_specs=pl.BlockSpec((nb, tt, c), lambda s, i: (s, i, 0)),
        out_shape=jax.ShapeDtypeStruct(cur3.shape, BF16),
        scratch_shapes=[pltpu.VMEM((8, nb, CONV_HIST + tt, c), F32), pltpu.VMEM((width, 8, c), F32)],
        compiler_params=_params("arbitrary", "arbitrary"),
        name="conv",
    )(cur3, hist3, w, b, g, beta)


def _first_argmax(v, iota, axis, size):
    m = jnp.max(v, axis=axis, keepdims=True)
    i = jnp.min(jnp.where(v == m, iota, size), axis=axis, keepdims=True)
    return m, i


def _router_kernel(h_ref, w_ref, bias_ref, idx_ref, gate_ref, rank_ref, cnt_ref, carry_ref):
    n_exp = w_ref.shape[0]
    tm = h_ref.shape[0]
    per_group = n_exp // N_EXPERT_GROUPS

    @pl.when(pl.program_id(0) == 0)
    def _():
        carry_ref[...] = jnp.zeros(carry_ref.shape, F32)

    w = w_ref[...]
    w1 = w.astype(BF16)
    r1 = w - w1.astype(F32)
    w2 = r1.astype(BF16)
    w3 = (r1 - w2.astype(F32)).astype(BF16)
    h = h_ref[...]
    logits = _dot_nt(w1, h) + _dot_nt(w2, h) + _dot_nt(w3, h)
    scores = jax.nn.sigmoid(logits)
    choice = scores + bias_ref[...]

    ch3 = choice.reshape(N_EXPERT_GROUPS, per_group, tm)
    io3 = lax.broadcasted_iota(I32, ch3.shape, 1)
    m1, i1 = _first_argmax(ch3, io3, 1, per_group)
    m2 = jnp.max(jnp.where(io3 == i1, -jnp.inf, ch3), axis=1, keepdims=True)
    grp = (m1 + m2).reshape(N_EXPERT_GROUPS, tm)
    gio = lax.broadcasted_iota(I32, grp.shape, 0)
    keep = jnp.zeros(grp.shape, jnp.bool_)
    for _ in range(TOPK_GROUPS):
        _, gi = _first_argmax(grp, gio, 0, N_EXPERT_GROUPS)
        hit = gio == gi
        keep = jnp.logical_or(keep, hit)
        grp = jnp.where(hit, -jnp.inf, grp)
    keep3 = jnp.broadcast_to(keep.reshape(N_EXPERT_GROUPS, 1, tm), ch3.shape)
    masked = jnp.where(keep3, ch3, -jnp.inf).reshape(n_exp, tm)

    eio = lax.broadcasted_iota(I32, masked.shape, 0)
    sel = jnp.zeros(masked.shape, jnp.bool_)
    picks, pick_scores = [], []
    for _ in range(TOP_K):
        _, ei = _first_argmax(masked, eio, 0, n_exp)
        hit = eio == ei
        picks.append((ei, hit))
        pick_scores.append(jnp.sum(jnp.where(hit, scores, 0.0), axis=0, keepdims=True))
        sel = jnp.logical_or(sel, hit)
        masked = jnp.where(hit, -jnp.inf, masked)
    total = pick_scores[0]
    for s in pick_scores[1:]:
        total = total + s

    sel_b = jnp.where(sel, 1.0, 0.0).astype(BF16)
    tri = (lax.broadcasted_iota(I32, (tm, tm), 0) < lax.broadcasted_iota(I32, (tm, tm), 1))
    rank = jnp.dot(sel_b, jnp.where(tri, 1.0, 0.0).astype(BF16), preferred_element_type=F32) + carry_ref[...]
    for kk, (ei, hit) in enumerate(picks):
        idx_ref[kk:kk + 1, :] = ei
        gate_ref[kk:kk + 1, :] = pick_scores[kk] / total * ROUTED_SCALE
        rank_ref[kk:kk + 1, :] = jnp.sum(jnp.where(hit, rank, 0.0), axis=0, keepdims=True).astype(I32)
    carry_ref[...] = carry_ref[...] + jnp.sum(jnp.where(sel, 1.0, 0.0), axis=1, keepdims=True)
    cnt_ref[...] = carry_ref[...].astype(I32)


def _router(h2b, w_router_t, bias_col):
    t, d = h2b.shape
    n_exp = w_router_t.shape[0]
    tm = math.gcd(512, t)
    assert t % tm == 0
    tok = lambda: pl.BlockSpec((TOP_K, tm), lambda i: (0, i))
    return pl.pallas_call(
        _router_kernel,
        grid=(t // tm,),
        in_specs=[pl.BlockSpec((tm, d), lambda i: (i, 0)),
                  pl.BlockSpec((n_exp, d), lambda i: (0, 0)),
                  pl.BlockSpec((n_exp, 1), lambda i: (0, 0))],
        out_specs=[tok(), tok(), tok(), pl.BlockSpec((n_exp, 1), lambda i: (0, 0))],
        out_shape=[jax.ShapeDtypeStruct((TOP_K, t), I32), jax.ShapeDtypeStruct((TOP_K, t), F32),
                   jax.ShapeDtypeStruct((TOP_K, t), I32), jax.ShapeDtypeStruct((n_exp, 1), I32)],
        scratch_shapes=[pltpu.VMEM((n_exp, 1), F32)],
        compiler_params=_params("arbitrary"),
        name="router",
    )(h2b, w_router_t, bias_col)


def _dispatch_kernel(rows_ref, h_ref, xs_ref, sem):
    n_tok = h_ref.shape[0]

    def body(t, carry):
        for kk in range(TOP_K):
            pltpu.make_async_copy(h_ref.at[pl.ds(t, 1)], xs_ref.at[pl.ds(rows_ref[kk, t], 1)], sem).start()
        return carry

    lax.fori_loop(0, n_tok, body, 0)
    for _ in range(TOP_K):
        pltpu.make_async_copy(h_ref, xs_ref.at[pl.ds(0, n_tok)], sem).wait()


def _dispatch(rows, h2p, n_rows):
    t, w = h2p.shape
    nt = math.gcd(DISPATCH_ROWS, t)
    assert t % nt == 0 and n_rows >= nt
    rows = jnp.transpose(rows.reshape(TOP_K, t // nt, nt), (1, 0, 2))
    return pl.pallas_call(
        _dispatch_kernel,
        grid=(t // nt,),
        in_specs=[pl.BlockSpec((None, TOP_K, nt), lambda i: (i, 0, 0), memory_space=pltpu.SMEM),
                  pl.BlockSpec((nt, w), lambda i: (i, 0))],
        out_specs=pl.BlockSpec(memory_space=pl.ANY),
        out_shape=jax.ShapeDtypeStruct((n_rows, w), h2p.dtype),
        scratch_shapes=[pltpu.SemaphoreType.DMA(())],
        compiler_params=pltpu.CompilerParams(dimension_semantics=("arbitrary",), has_side_effects=True),
        name="dispatch",
    )(rows, h2p)


def _seg_kernel(se_ref, sb0_ref, snb_ref, srows_ref, nseg_ref, xs_hbm, wg_hbm, wu_hbm, wd_hbm, ys_hbm,
                xbuf, wgf, wuf, wdf, wgb, wub, wdb, gacc, uacc, hid, ybuf, sem_x, sem_w, sem_y, ycnt):
    s = pl.program_id(0)
    nseg = nseg_ref[0]
    kc, phases = EXP_KCHUNKS, EXP_KCHUNKS + EXP_NCHUNKS
    r = EXP_ROWS
    dk, dn = wgf.shape[1], wdf.shape[2]
    hw = dk // 2

    def w_copies(e, ph):
        slot = ph & 1
        if ph < kc:
            return [pltpu.make_async_copy(wg_hbm.at[e, pl.ds(ph * dk, dk), :], wgf.at[slot], sem_w.at[0, slot]),
                    pltpu.make_async_copy(wu_hbm.at[e, pl.ds(ph * dk, dk), :], wuf.at[slot], sem_w.at[1, slot])]
        return [pltpu.make_async_copy(wd_hbm.at[e, :, pl.ds((ph - kc) * dn, dn)], wdf.at[slot], sem_w.at[2, slot])]

    def for_x_blocks(seg, slot, action):
        for b in range(SEG_BLOCKS):
            @pl.when(b < snb_ref[seg])
            def _():
                row = pl.multiple_of((sb0_ref[seg] + b) * r, r)
                action(pltpu.make_async_copy(xs_hbm.at[pl.ds(row, r), :], xbuf.at[slot, pl.ds(b * r, r), :],
                                             sem_x.at[slot]))

    def y_copy(slot, row, col):
        return pltpu.make_async_copy(ybuf.at[slot], ys_hbm.at[pl.ds(row, r), pl.ds(col, dn)], sem_y.at[slot])

    @pl.when(s < nseg)
    def _():
        e, nb, par = se_ref[s], snb_ref[s], s & 1
        row_base = sb0_ref[s] * r
        valid = srows_ref[s]

        @pl.when(s == 0)
        def _():
            ycnt[0] = 0
            for_x_blocks(0, 0, lambda c: c.start())
            for c in w_copies(e, 0):
                c.start()

        @pl.when(s + 1 < nseg)
        def _():
            for_x_blocks(s + 1, 1 - par, lambda c: c.start())

        for_x_blocks(s, par, lambda c: c.wait())

        for ph in range(phases):
            if ph + 1 < phases:
                for c in w_copies(e, ph + 1):
                    c.start()
            else:
                @pl.when(s + 1 < nseg)
                def _():
                    for c in w_copies(se_ref[s + 1], 0):
                        c.start()
            for c in w_copies(e, ph):
                c.wait()
            slot = ph & 1
            if ph < kc:
                wgb[...] = wgf[slot].astype(BF16)
                wub[...] = wuf[slot].astype(BF16)

                def body(b, carry, ph=ph):
                    r0 = pl.multiple_of(b * r, r)
                    rows = r0 + lax.broadcasted_iota(I32, (r, 1), 0)
                    words = xbuf[par, pl.ds(r0, r), ph * hw:(ph + 1) * hw]
                    lo, hi = _unpack_halves(jnp.where(rows < valid, words, jnp.uint32(0)))
                    g = (jnp.dot(lo, wgb[:hw, :], preferred_element_type=F32)
                         + jnp.dot(hi, wgb[hw:, :], preferred_element_type=F32))
                    u = (jnp.dot(lo, wub[:hw, :], preferred_element_type=F32)
                         + jnp.dot(hi, wub[hw:, :], preferred_element_type=F32))
                    if ph > 0:
                        g = g + gacc[pl.ds(r0, r), :]
                        u = u + uacc[pl.ds(r0, r), :]
                    if ph < kc - 1:
                        gacc[pl.ds(r0, r), :] = g
                        uacc[pl.ds(r0, r), :] = u
                    else:
                        hid[pl.ds(r0, r), :] = (_silu(g) * u).astype(BF16)
                    return carry
            else:
                wdb[...] = wdf[slot].astype(BF16)

                def body(b, carry, ph=ph):
                    r0 = pl.multiple_of(b * r, r)
                    n = ycnt[0]
                    ys = n & 1

                    @pl.when(n >= 2)
                    def _():
                        y_copy(ys, 0, 0).wait()

                    ybuf[ys] = jnp.dot(hid[pl.ds(r0, r), :], wdb[...], preferred_element_type=F32)
                    y_copy(ys, pl.multiple_of(row_base + r0, r), (ph - kc) * dn).start()
                    ycnt[0] = n + 1
                    return carry

            lax.fori_loop(0, nb, body, 0)

        @pl.when(s == nseg - 1)
        def _():
            n = ycnt[0]
            for back in (1, 2):
                @pl.when(n >= back)
                def _():
                    y_copy((n - back) & 1, 0, 0).wait()


def _expert_segments(counts, n_blocks_max):
    n_exp = counts.shape[0]
    span = SEG_BLOCKS * EXP_ROWS
    nb_e = (counts + EXP_ROWS - 1) // EXP_ROWS
    blk_start_e = jnp.cumsum(nb_e) - nb_e
    nseg_e = (nb_e + SEG_BLOCKS - 1) // SEG_BLOCKS
    seg_end_e = jnp.cumsum(nseg_e)
    n_seg_max = n_exp + n_blocks_max // SEG_BLOCKS
    sid = jnp.arange(n_seg_max, dtype=I32)
    seg_e = jnp.minimum(jnp.sum(seg_end_e[None, :] <= sid[:, None], axis=1), n_exp - 1).astype(I32)
    mine = seg_e[:, None] == jnp.arange(n_exp, dtype=I32)[None, :]
    of_expert = lambda v: jnp.sum(jnp.where(mine, v[None, :], 0), axis=1)
    part = sid - of_expert(seg_end_e - nseg_e)
    seg_nb = jnp.clip(of_expert(nb_e) - part * SEG_BLOCKS, 0, SEG_BLOCKS)
    seg_blk0 = of_expert(blk_start_e) + part * SEG_BLOCKS
    seg_rows = jnp.clip(of_expert(counts) - part * span, 0, span)
    to_i32 = lambda v: v.astype(I32)
    return (seg_e, to_i32(seg_blk0), to_i32(seg_nb), to_i32(seg_rows), to_i32(seg_end_e[-1:]),
            to_i32(blk_start_e * EXP_ROWS))


def _experts_by_segment(seg_tables, xs, w_gate, w_up, w_down):
    n_exp, d, ff = w_gate.shape
    kc, nc = EXP_KCHUNKS, EXP_NCHUNKS
    dk, dn = d // kc, d // nc
    span = SEG_BLOCKS * EXP_ROWS
    assert (kc + nc) % 2 == 0
    hbm = lambda: pl.BlockSpec(memory_space=pl.ANY)
    grid_spec = pltpu.PrefetchScalarGridSpec(
        num_scalar_prefetch=5,
        grid=(seg_tables[0].shape[0],),
        in_specs=[hbm(), hbm(), hbm(), hbm()],
        out_specs=hbm(),
        scratch_shapes=[pltpu.VMEM((2, span, d // 2), U32),
                        pltpu.VMEM((2, dk, ff), F32), pltpu.VMEM((2, dk, ff), F32), pltpu.VMEM((2, ff, dn), F32),
                        pltpu.VMEM((dk, ff), BF16), pltpu.VMEM((dk, ff), BF16), pltpu.VMEM((ff, dn), BF16),
                        pltpu.VMEM((span, ff), F32), pltpu.VMEM((span, ff), F32), pltpu.VMEM((span, ff), BF16),
                        pltpu.VMEM((2, EXP_ROWS, dn), F32),
                        pltpu.SemaphoreType.DMA((2,)), pltpu.SemaphoreType.DMA((3, 2)),
                        pltpu.SemaphoreType.DMA((2,)), pltpu.SMEM((1,), I32)],
    )
    return pl.pallas_call(
        _seg_kernel,
        grid_spec=grid_spec,
        out_shape=jax.ShapeDtypeStruct((xs.shape[0], d), F32),
        compiler_params=_params("arbitrary"),
        name="experts",
    )(*seg_tables, xs, w_gate, w_up, w_down)


def _combine_kernel(rows_ref, gate_ref, hs_ref, wsd_ref, ys_ref, f_ref, buf_ref, wb_ref, sem):
    n_tok = gate_ref.shape[0]

    @pl.when(pl.program_id(0) == 0)
    def _():
        wb_ref[...] = wsd_ref[...].astype(BF16)

    def body(t, carry):
        for kk in range(TOP_K):
            pltpu.make_async_copy(ys_ref.at[pl.ds(rows_ref[kk, t], 1)], buf_ref.at[kk, pl.ds(t, 1)], sem).start()
        return carry

    lax.fori_loop(0, n_tok, body, 0)
    shared = jnp.dot(hs_ref[...], wb_ref[...], preferred_element_type=F32)
    for kk in range(TOP_K):
        pltpu.make_async_copy(ys_ref.at[pl.ds(0, n_tok)], buf_ref.at[kk], sem).wait()
    gates = gate_ref[...]
    acc = shared
    for kk in range(TOP_K):
        acc = acc + buf_ref[kk] * gates[:, kk:kk + 1]
    f_ref[...] = acc


def _combine(rows, gates_t, hid_sh, w_sh_down, ys):
    t, ff = hid_sh.shape
    d = w_sh_down.shape[1]
    nt = math.gcd(COMBINE_ROWS, t)
    assert t % nt == 0
    rows = jnp.transpose(rows.reshape(TOP_K, t // nt, nt), (1, 0, 2))
    return pl.pallas_call(
        _combine_kernel,
        grid=(t // nt,),
        in_specs=[pl.BlockSpec((None, TOP_K, nt), lambda i: (i, 0, 0), memory_space=pltpu.SMEM),
                  pl.BlockSpec((nt, TOP_K), lambda i: (i, 0)),
                  pl.BlockSpec((nt, ff), lambda i: (i, 0)),
                  pl.BlockSpec((ff, d), lambda i: (0, 0)),
                  pl.BlockSpec(memory_space=pl.ANY)],
        out_specs=pl.BlockSpec((nt, d), lambda i: (i, 0)),
        out_shape=jax.ShapeDtypeStruct((t, d), F32),
        scratch_shapes=[pltpu.VMEM((TOP_K, nt, d), F32), pltpu.VMEM((ff, d), BF16), pltpu.SemaphoreType.DMA(())],
        compiler_params=_params("arbitrary"),
        name="combine",
    )(rows, gates_t, hid_sh, w_sh_down, ys)


def _layer(l, lam_init, x_prompt, x_sample, c_prompt, c_sample, cache_k, cache_v, state_conv, rel_bias, p):
    batch, seq, d = x_prompt.shape
    db, dseq, _ = x_sample.shape
    past, heads = cache_k.shape[1], cache_k.shape[2]
    hw = 2 * HEAD_DIM
    aw = heads * hw
    cc = p["conv_dw_w"].shape[1]
    width = p["conv_dw_w"].shape[0]
    assert p["w_in"].shape[1] == 3 * aw + 2 * cc and seq % dseq == 0
    tp, ts = batch * seq, db * dseq
    row = lambda v: v.reshape(1, -1)

    n_mod_rows = -(-(db + batch) // 16) * 16
    c_all = jnp.concatenate([c_sample, c_prompt, jnp.zeros((n_mod_rows - db - batch, d), F32)], axis=0)
    mod3 = _ada(c_all, p["w_ada"], row(p["b_ada"])).reshape(n_mod_rows, 1, N_MOD * d)

    gp = tp // dseq
    gps = seq // dseq
    nb_p = math.gcd(ROW_GROUPS, gps)
    nb_s = math.gcd(ROW_GROUPS, db)
    assert gps % nb_p == 0 and db % nb_s == 0 and gp % nb_s == 0
    plan_p = _RowPlan(gp, dseq, d, nb_p, 1, lambda i: db + (i * nb_p) // gps, 0)
    plan_s = _RowPlan(db, dseq, d, nb_s, nb_s, lambda i: i, gp // nb_s)
    xp3 = x_prompt.reshape(gp, dseq, d)

    hp = _prenorm(plan_p, xp3, row(p["g_pre_mix"]), mod3).reshape(tp, d)
    hs = _prenorm(plan_s, x_sample, row(p["g_pre_mix"]), mod3).reshape(ts, d)

    w_in = p["w_in"]
    proj = lambda h, off, nm: _mm([h], [(w_in, off)], aw, _identity, F32, 512, 512, nm)
    qp, kp, vp = proj(hp, 0, "q_prompt"), proj(hp, aw, "k_prompt"), proj(hp, 2 * aw, "v_prompt")
    qs, ks, vs = proj(hs, 0, "q_sample"), proj(hs, aw, "k_sample"), proj(hs, 2 * aw, "v_sample")
    glu_cols = [(w_in, 3 * aw), (w_in, 3 * aw + cc)]
    glu_p = _mm([hp], glu_cols, cc, _glu, F32, 512, 256, "glu_prompt")
    glu_s = _mm([hs], glu_cols, cc, _glu, F32, 512, 256, "glu_sample")

    lams = [row(p[n]) for n in ("lambda_q1", "lambda_k1", "lambda_q2", "lambda_k2")]
    subln = row(p["subln_w"])
    attn_p = _attn_prompt(qp, kp, vp, rel_bias, lams, subln, batch, seq, heads, lam_init)

    new_k_s = ks.reshape(db, dseq, heads, hw)
    new_v_s = vs.reshape(db, dseq, heads, hw)
    q_s = jnp.transpose(qs.reshape(db, dseq, heads, 2, HEAD_DIM), (0, 3, 2, 1, 4)).reshape(db, 2, heads * dseq, HEAD_DIM)
    o_s = _attn_sample(q_s, cache_k, cache_v, new_k_s.reshape(db, dseq * heads, hw),
                       new_v_s.reshape(db, dseq * heads, hw), rel_bias, lams, subln, lam_init)
    attn_s = jnp.transpose(o_s.reshape(db, heads, dseq, hw), (0, 2, 1, 3)).reshape(ts, aw)

    conv_args = (p["conv_dw_w"], row(p["conv_dw_b"]), row(p["conv_ln_g"]), row(p["conv_ln_b"]))
    glu_p3 = glu_p.reshape(batch, seq, cc)
    tt = min(128, seq)
    per = tt // CONV_HIST
    conv_p = _conv(glu_p3, glu_p3, lambda s, i: (s, jnp.maximum(i * per - 1, 0), 0), 1, tt, *conv_args,
                   zero_first=True).reshape(tp, cc)
    glu_s3 = glu_s.reshape(db, dseq, cc)
    hist_s = jnp.concatenate([jnp.zeros((db, CONV_HIST - (width - 1), cc), F32), state_conv], axis=1)
    nb_c = math.gcd(2, db)
    conv_s = _conv(glu_s3, hist_s, lambda s, i: (s, 0, 0), nb_c, dseq, *conv_args, zero_first=False).reshape(ts, cc)
    new_conv_p = glu_p3[:, seq - (width - 1):]
    new_conv_s = jnp.concatenate([state_conv, glu_s3], axis=1)[:, -(width - 1):]

    mix_p = _mm([attn_p, conv_p], [(p["w_out"], 0)], d, _identity, F32, 512, 512, "out_prompt")
    mix_s = _mm([attn_s, conv_s], [(p["w_out"], 0)], d, _identity, F32, 512, 512, "out_sample")

    gt = gp + db
    gpost, gpre = row(p["g_post_mix"]), row(p["g_pre_ffn"])
    x1p, h2ba, h2pa = _postmix(plan_p, xp3, mix_p.reshape(gp, dseq, d), gpost, gpre, mod3, gt)
    x1s, h2ba, h2pa = _postmix(plan_s, x_sample, mix_s.reshape(db, dseq, d), gpost, gpre, mod3, gt,
                               prev=(h2ba, h2pa))
    t = tp + ts
    h2b = h2ba.reshape(t, d)
    h2p = h2pa.reshape(t, d // 2)

    n_exp = p["w_router"].shape[1]
    idx, gates, rank, counts = _router(h2b, p["w_router"].T, p["b_router_corr"].reshape(n_exp, 1))
    n_blocks_max = (t * TOP_K) // EXP_ROWS + n_exp
    *seg_tables, row_start = _expert_segments(counts.reshape(n_exp), n_blocks_max)
    pick = idx[..., None] == jnp.arange(n_exp, dtype=I32)
    rows = jnp.sum(jnp.where(pick, row_start, 0), axis=-1) + rank

    xs = _dispatch(rows.astype(I32), h2p, n_blocks_max * EXP_ROWS)
    ys = _experts_by_segment(seg_tables, xs, p["w_exp_gate"], p["w_exp_up"], p["w_exp_down"])
    ff_sh = p["w_sh_gate"].shape[1]
    hid_sh = _mm([h2b], [(p["w_sh_gate"], 0), (p["w_sh_up"], 0)], ff_sh, _swiglu, BF16, 512, 256, "shared_up")
    f = _combine(rows.astype(I32), gates.T, hid_sh, p["w_sh_down"], ys)

    f3 = f.reshape(gt, dseq, d)
    gpf = row(p["g_post_ffn"])
    yp = _final(plan_p, x1p, f3, gpf, mod3).reshape(batch, seq, d)
    ysmp = _final(plan_s, x1s, f3, gpf, mod3)
    new_k_p = kp.reshape(batch, seq, heads, hw)
    new_v_p = vp.reshape(batch, seq, heads, hw)
    return yp, ysmp, new_k_p, new_v_p, new_conv_p, new_k_s, new_v_s, new_conv_s


def kernel(x_prompt, x_sample, c_prompt, c_sample, cache_k, cache_v, state_conv, rel_bias, w_ada, b_ada, g_pre_mix, g_post_mix, g_pre_ffn, g_post_ffn, w_in, lambda_q1, lambda_k1, lambda_q2, lambda_k2, subln_w, conv_dw_w, conv_dw_b, conv_ln_g, conv_ln_b, w_out, w_router, b_router_corr, w_exp_gate, w_exp_up, w_exp_down, w_sh_gate, w_sh_up, w_sh_down):
    weights = dict(w_ada=w_ada, b_ada=b_ada, g_pre_mix=g_pre_mix, g_post_mix=g_post_mix, g_pre_ffn=g_pre_ffn,
                   g_post_ffn=g_post_ffn, w_in=w_in, lambda_q1=lambda_q1, lambda_k1=lambda_k1,
                   lambda_q2=lambda_q2, lambda_k2=lambda_k2, subln_w=subln_w, conv_dw_w=conv_dw_w,
                   conv_dw_b=conv_dw_b, conv_ln_g=conv_ln_g, conv_ln_b=conv_ln_b, w_out=w_out,
                   w_router=w_router, b_router_corr=b_router_corr, w_exp_gate=w_exp_gate, w_exp_up=w_exp_up,
                   w_exp_down=w_exp_down, w_sh_gate=w_sh_gate, w_sh_up=w_sh_up, w_sh_down=w_sh_down)
    depth = w_in.shape[0]
    xp, xs = x_prompt, x_sample
    outs = [[] for _ in range(6)]
    for l in range(depth):
        p = {k: (v.reshape(v.shape[1:]) if depth == 1 else v[l]) for k, v in weights.items()}
        lam_init = 0.8 - 0.6 * math.exp(-0.3 * l)
        ck, cv, sc = ((a.reshape(a.shape[1:]) if depth == 1 else a[l]) for a in (cache_k, cache_v, state_conv))
        xp, xs, *state = _layer(l, lam_init, xp, xs, c_prompt, c_sample, ck, cv, sc, rel_bias, p)
        for acc, s in zip(outs, state):
            acc.append(s)
    return (xp, xs) + tuple(jnp.stack(o) for o in outs)
```

```python
import functools
import math

import numpy as np
import jax
import jax.numpy as jnp
from jax import lax
from jax.experimental import pallas as pl
from jax.experimental.pallas import tpu as pltpu

F32 = jnp.float32
BF16 = jnp.bfloat16
I32 = jnp.int32
U32 = jnp.uint32

CHUNK = 64
HEAD_DIM = 128
NUM_BUCKETS = 32
REL_MAX_DISTANCE = 128
TOP_K = 8
N_EXPERT_GROUPS = 8
TOPK_GROUPS = 4
ROUTED_SCALE = 2.5
RMS_EPS = 1e-6
SUBLN_EPS = 1e-5
LN_EPS = 1e-5
N_MOD = 6
NEG = -1e30
ATT_SCALE = HEAD_DIM ** -0.5

VMEM_LIMIT_BYTES = 56 * 1024 * 1024
ATT_TILE = 256
EXP_ROWS = 128
SEG_BLOCKS = 4
W_DMA_PARTS = 4
EXP_KCHUNKS = 2
EXP_NCHUNKS = 2
COMBINE_ROWS = 128
DISPATCH_ROWS = 256
ROW_GROUPS = 4


def _params(*sem):
    return pltpu.CompilerParams(dimension_semantics=sem, vmem_limit_bytes=VMEM_LIMIT_BYTES)


def _silu(x):
    return x * jax.nn.sigmoid(x)


def _dot_nt(a, b):
    return lax.dot_general(a, b, (((1,), (1,)), ((), ())), preferred_element_type=F32)


def _ada_kernel(c_ref, w_ref, b_ref, o_ref):
    a = _silu(c_ref[...]).astype(BF16)
    o_ref[...] = jnp.dot(a, w_ref[...].astype(BF16), preferred_element_type=F32) + b_ref[...]


def _ada(c_all, w_ada, b_ada):
    rows, d = c_all.shape
    n = w_ada.shape[1]
    tn = min(512, n)
    return pl.pallas_call(
        _ada_kernel,
        grid=(n // tn,),
        in_specs=[pl.BlockSpec((rows, d), lambda j: (0, 0)),
                  pl.BlockSpec((d, tn), lambda j: (0, j)),
                  pl.BlockSpec((1, tn), lambda j: (0, j))],
        out_specs=pl.BlockSpec((rows, tn), lambda j: (0, j)),
        out_shape=jax.ShapeDtypeStruct((rows, n), F32),
        compiler_params=_params("arbitrary"),
        name="ada",
    )(c_all, w_ada, b_ada)


class _RowPlan:
    def __init__(self, n_groups, group_rows, d, nb, mod_nb, mod_index, out_block_offset=0):
        self.n_groups, self.group_rows, self.d, self.nb = n_groups, group_rows, d, nb
        self.mod_nb, self.mod_index, self.out_block_offset = mod_nb, mod_index, out_block_offset
        self.grid = (n_groups // nb,)

    def act(self, offset=0):
        return pl.BlockSpec((self.nb, self.group_rows, self.d), lambda i: (i + offset, 0, 0))

    def mod(self, chunk):
        return pl.BlockSpec((self.mod_nb, 1, self.d), lambda i: (self.mod_index(i), 0, chunk))

    def vec(self):
        return pl.BlockSpec((1, self.d), lambda i: (0, 0))


def _rms(x, g, eps):
    return x * lax.rsqrt(jnp.mean(x * x, axis=-1, keepdims=True) + eps) * g


def _prenorm_kernel(x_ref, g_ref, sh_ref, sc_ref, o_ref):
    y = _rms(x_ref[...], g_ref[...], RMS_EPS)
    o_ref[...] = (y * (1.0 + sc_ref[...]) + sh_ref[...]).astype(o_ref.dtype)


def _prenorm(plan, x3, g, mod3):
    return pl.pallas_call(
        _prenorm_kernel,
        grid=plan.grid,
        in_specs=[plan.act(), plan.vec(), plan.mod(0), plan.mod(1)],
        out_specs=plan.act(),
        out_shape=jax.ShapeDtypeStruct(x3.shape, BF16),
        compiler_params=_params("arbitrary"),
        name="prenorm",
    )(x3, g, mod3, mod3)


def _pack_halves(x):
    n = x.shape[-1] // 2
    lo = lax.bitcast_convert_type(x[..., :n].astype(BF16).astype(F32), U32) >> 16
    hi = lax.bitcast_convert_type(x[..., n:].astype(BF16).astype(F32), U32) & jnp.uint32(0xFFFF0000)
    return hi | lo


def _unpack_halves(w):
    lo = lax.bitcast_convert_type(w << 16, F32).astype(BF16)
    hi = lax.bitcast_convert_type(w & jnp.uint32(0xFFFF0000), F32).astype(BF16)
    return lo, hi


def _postmix_kernel(x_ref, mix_ref, gpost_ref, gpre_ref, g1_ref, sh2_ref, sc2_ref, *rest):
    x1_ref, h2b_ref, h2p_ref = rest[-3:]
    x1 = x_ref[...] + g1_ref[...] * _rms(mix_ref[...], gpost_ref[...], RMS_EPS)
    x1_ref[...] = x1
    h2 = _rms(x1, gpre_ref[...], RMS_EPS) * (1.0 + sc2_ref[...]) + sh2_ref[...]
    h2b_ref[...] = h2.astype(BF16)
    dk = h2.shape[-1] // EXP_KCHUNKS
    for c in range(EXP_KCHUNKS):
        h2p_ref[:, :, c * (dk // 2):(c + 1) * (dk // 2)] = _pack_halves(h2[:, :, c * dk:(c + 1) * dk])


def _postmix(plan, x3, mix3, gpost, gpre, mod3, total_groups, prev=None):
    off = plan.out_block_offset
    in_specs = [plan.act(), plan.act(), plan.vec(), plan.vec(), plan.mod(2), plan.mod(3), plan.mod(4)]
    args = [x3, mix3, gpost, gpre, mod3, mod3, mod3]
    aliases = {}
    if prev is not None:
        in_specs += [pl.BlockSpec(memory_space=pl.ANY), pl.BlockSpec(memory_space=pl.ANY)]
        args += list(prev)
        aliases = {7: 1, 8: 2}
    shape_all = (total_groups, plan.group_rows, plan.d)
    shape_packed = (total_groups, plan.group_rows, plan.d // 2)
    packed_spec = pl.BlockSpec((plan.nb, plan.group_rows, plan.d // 2), lambda i: (i + off, 0, 0))
    return pl.pallas_call(
        _postmix_kernel,
        grid=plan.grid,
        in_specs=in_specs,
        out_specs=[plan.act(), plan.act(off), packed_spec],
        out_shape=[jax.ShapeDtypeStruct(x3.shape, F32),
                   jax.ShapeDtypeStruct(shape_all, BF16),
                   jax.ShapeDtypeStruct(shape_packed, U32)],
        input_output_aliases=aliases,
        compiler_params=_params("arbitrary"),
        name="postmix",
    )(*args)


def _final_kernel(x1_ref, f_ref, gpost_ref, g2_ref, o_ref):
    o_ref[...] = x1_ref[...] + g2_ref[...] * _rms(f_ref[...], gpost_ref[...], RMS_EPS)


def _final(plan, x13, f3, gpost, mod3):
    return pl.pallas_call(
        _final_kernel,
        grid=plan.grid,
        in_specs=[plan.act(), plan.act(plan.out_block_offset), plan.vec(), plan.mod(5)],
        out_specs=plan.act(),
        out_shape=jax.ShapeDtypeStruct(x13.shape, F32),
        compiler_params=_params("arbitrary"),
        name="final",
    )(x13, f3, gpost, mod3)


def _mm_kernel(*refs, n_a, n_w, k_sizes, epilogue):
    a_refs = refs[:n_a]
    w_refs = refs[n_a:n_a + n_w]
    o_ref = refs[n_a + n_w]
    wb_refs = refs[n_a + n_w + 1:]

    @pl.when(pl.program_id(1) == 0)
    def _():
        for w_ref, wb_ref in zip(w_refs, wb_refs):
            wb_ref[...] = w_ref[...].astype(BF16)

    parts = []
    for wb_ref in wb_refs:
        acc, k0 = None, 0
        for a_ref, ka in zip(a_refs, k_sizes):
            d = jnp.dot(a_ref[...], wb_ref[k0:k0 + ka, :], preferred_element_type=F32)
            acc = d if acc is None else acc + d
            k0 += ka
        parts.append(acc)
    o_ref[...] = epilogue(*parts).astype(o_ref.dtype)


def _mm(a_list, w_cols, n_out, epilogue, out_dtype, tm, tn, name):
    m = a_list[0].shape[0]
    k = w_cols[0][0].shape[0]
    tm, tn = math.gcd(tm, m), math.gcd(tn, n_out)
    k_sizes = tuple(a.shape[1] for a in a_list)
    assert sum(k_sizes) == k and m % tm == 0 and n_out % tn == 0 and all(c % tn == 0 for _, c in w_cols)
    a_specs = [pl.BlockSpec((tm, ka), lambda j, i: (i, 0)) for ka in k_sizes]
    w_specs = [pl.BlockSpec((k, tn), functools.partial(lambda j, i, o: (0, o + j), o=c // tn)) for _, c in w_cols]
    kern = functools.partial(_mm_kernel, n_a=len(a_list), n_w=len(w_cols), k_sizes=k_sizes, epilogue=epilogue)
    return pl.pallas_call(
        kern,
        grid=(n_out // tn, m // tm),
        in_specs=a_specs + w_specs,
        out_specs=pl.BlockSpec((tm, tn), lambda j, i: (i, j)),
        out_shape=jax.ShapeDtypeStruct((m, n_out), out_dtype),
        scratch_shapes=[pltpu.VMEM((k, tn), BF16) for _ in w_cols],
        compiler_params=_params("arbitrary", "arbitrary"),
        name=name,
    )(*a_list, *[w for w, _ in w_cols])


def _identity(x):
    return x


def _glu(a, b):
    return a * jax.nn.sigmoid(b)


def _swiglu(a, b):
    return _silu(a) * b


def _bucket(rel):
    half = NUM_BUCKETS // 2
    max_exact = half // 2
    n = np.abs(rel)
    nf = np.maximum(n, 1).astype(np.float32)
    large = max_exact + (np.log(nf / np.float32(max_exact)) / np.float32(math.log(REL_MAX_DISTANCE / max_exact))
                         * np.float32(half - max_exact)).astype(np.int32)
    large = np.minimum(large, half - 1)
    return np.where(rel > 0, half, 0) + np.where(n < max_exact, n, large)


def _bias_table(rel_bias, q_pos, k_pos):
    rel = k_pos[None, :] - q_pos[:, None]
    visible = (k_pos // CHUNK)[None, :] <= (q_pos // CHUNK)[:, None]
    onehot = (jnp.asarray(_bucket(rel))[..., None] == jnp.arange(NUM_BUCKETS)).astype(F32)
    b = jnp.einsum("qkb,bhm->hmqk", onehot, rel_bias.astype(F32), precision=lax.Precision.HIGHEST)
    return jnp.where(jnp.asarray(visible)[None, None], b, NEG)


def _lambda(lq1, lk1, lq2, lk2, lam_init):
    return (jnp.exp(jnp.sum(lq1[...] * lk1[...], keepdims=True))
            - jnp.exp(jnp.sum(lq2[...] * lk2[...], keepdims=True)) + lam_init)


def _softmax_step(s, v, m_ref, l_ref, acc_ref, idx):
    m_old = m_ref[idx]
    m_new = jnp.maximum(m_old, jnp.max(s, axis=-1, keepdims=True))
    alpha = jnp.exp(m_old - m_new)
    p = jnp.exp(s - m_new)
    l_ref[idx] = alpha * l_ref[idx] + jnp.sum(p, axis=-1, keepdims=True)
    acc_ref[idx] = alpha * acc_ref[idx] + jnp.dot(p.astype(BF16), v, preferred_element_type=F32)
    m_ref[idx] = m_new


def _attn_finish(m_ref, l_ref, acc_ref, lam, sw, lam_init, o_ref):
    o = acc_ref[0] / l_ref[0] - lam * (acc_ref[1] / l_ref[1])
    o = o * lax.rsqrt(jnp.mean(o * o, axis=-1, keepdims=True) + SUBLN_EPS) * sw * (1.0 - lam_init)
    o_ref[...] = o.astype(o_ref.dtype)


def _attn_prompt_kernel(q_ref, k_ref, v_ref, b_ref, lq1, lk1, lq2, lk2, sw_ref, o_ref, kb_ref, vb_ref,
                        *, lam_init, nq):
    t = ATT_TILE
    kb_ref[...] = k_ref[...].astype(BF16)
    vb_ref[...] = v_ref[...].astype(BF16)
    lam = _lambda(lq1, lk1, lq2, lk2, lam_init)
    sw = sw_ref[...]
    for qi in range(nq):
        q = q_ref[qi * t:(qi + 1) * t, :]
        n_far = max(qi - 1, 0) * t
        exps, sums = [], []
        for mp in range(2):
            cols = slice(mp * HEAD_DIM, (mp + 1) * HEAD_DIM)
            s = _dot_nt(q[:, cols].astype(BF16), kb_ref[0:(qi + 1) * t, cols]) * ATT_SCALE
            pieces = []
            if n_far:
                pieces.append(s[:, :n_far] + b_ref[mp, 2, 0:1, 0:1])
            if qi >= 1:
                pieces.append(s[:, n_far:n_far + t] + b_ref[mp, 1])
            pieces.append(s[:, qi * t:(qi + 1) * t] + b_ref[mp, 0])
            m = functools.reduce(jnp.maximum, [jnp.max(p, axis=-1, keepdims=True) for p in pieces])
            es = [jnp.exp(p - m) for p in pieces]
            exps.append(es)
            sums.append(functools.reduce(jnp.add, [jnp.sum(e, axis=-1, keepdims=True) for e in es]))
        c1 = 1.0 / sums[0]
        c2 = lam / sums[1]
        o, col = None, 0
        for e1, e2 in zip(exps[0], exps[1]):
            a = (e1 * c1 - e2 * c2).astype(BF16)
            d = jnp.dot(a, vb_ref[col:col + a.shape[1], :], preferred_element_type=F32)
            o = d if o is None else o + d
            col += a.shape[1]
        o = o * lax.rsqrt(jnp.mean(o * o, axis=-1, keepdims=True) + SUBLN_EPS) * sw * (1.0 - lam_init)
        o_ref[qi * t:(qi + 1) * t, :] = o.astype(o_ref.dtype)


def _attn_prompt(q, k, v, rel_bias, lams, subln_w, batch, seq, heads, lam_init):
    t = ATT_TILE
    assert seq % t == 0 and t % CHUNK == 0
    pos = np.arange(t)
    tiles = [_bias_table(rel_bias, pos + dt * t, pos) for dt in range(3)]
    assert np.all(_bucket(np.arange(-3 * t + 1, -t)) == NUM_BUCKETS // 2 - 1)
    bias = jnp.stack(tiles, axis=2)
    hw = 2 * HEAD_DIM
    vec = lambda n: pl.BlockSpec((1, n), lambda b, h: (0, 0))
    seq_spec = lambda: pl.BlockSpec((seq, hw), lambda b, h: (b, h))
    return pl.pallas_call(
        functools.partial(_attn_prompt_kernel, lam_init=lam_init, nq=seq // t),
        grid=(batch, heads),
        in_specs=[seq_spec(), seq_spec(), seq_spec(),
                  pl.BlockSpec((None, 2, 3, t, t), lambda b, h: (h, 0, 0, 0, 0)),
                  vec(HEAD_DIM), vec(HEAD_DIM), vec(HEAD_DIM), vec(HEAD_DIM), vec(hw)],
        out_specs=seq_spec(),
        out_shape=jax.ShapeDtypeStruct((batch * seq, heads * hw), BF16),
        scratch_shapes=[pltpu.VMEM((seq, hw), BF16), pltpu.VMEM((seq, hw), BF16)],
        compiler_params=_params("arbitrary", "arbitrary"),
        name="attn_prompt",
    )(q, k, v, bias, *lams, subln_w)


def _attn_sample_kernel(q_ref, ck_ref, cv_ref, kn_ref, vn_ref, b_ref, bn_ref, lq1, lk1, lq2, lk2, sw_ref, o_ref,
                        m_ref, l_ref, acc_ref, *, lam_init, n_tiles):
    kt_i = pl.program_id(1)

    @pl.when(kt_i == 0)
    def _():
        m_ref[...] = jnp.full(m_ref.shape, NEG, F32)
        l_ref[...] = jnp.zeros(l_ref.shape, F32)
        acc_ref[...] = jnp.zeros(acc_ref.shape, F32)

    def step(k2d, v2d, bias_of_map):
        kb = k2d.astype(BF16)
        vb = v2d.astype(BF16)
        for mp in range(2):
            qm = q_ref[mp].astype(BF16)
            s = _dot_nt(qm, kb[:, mp * HEAD_DIM:(mp + 1) * HEAD_DIM]) * ATT_SCALE + bias_of_map(mp)
            _softmax_step(s, vb, m_ref, l_ref, acc_ref, mp)

    t = b_ref.shape[3] // ck_ref.shape[1]
    n_sub = ck_ref.shape[0] // t
    for sub in range(n_sub):
        keys = slice(sub * t, (sub + 1) * t)
        bi = jnp.where(kt_i == n_tiles - 1, 1, 0) if sub == n_sub - 1 else 0
        step(ck_ref[keys].reshape(t * ck_ref.shape[1], ck_ref.shape[2]),
             cv_ref[keys].reshape(t * cv_ref.shape[1], cv_ref.shape[2]), lambda mp, bi=bi: b_ref[bi, mp])

    @pl.when(kt_i == n_tiles - 1)
    def _():
        step(kn_ref[...], vn_ref[...], lambda mp: bn_ref[mp])
        sw = sw_ref[...]
        _attn_finish(m_ref, l_ref, acc_ref, _lambda(lq1, lk1, lq2, lk2, lam_init), sw, lam_init, o_ref)


def _head_expand(bias, heads):
    h, _, tq, tk = bias.shape
    eye = jnp.asarray(np.eye(heads, dtype=bool))
    full = jnp.where(eye[:, None, None, None, :], bias[..., None], NEG)
    return jnp.transpose(full, (1, 0, 2, 3, 4)).reshape(2, h * tq, tk * heads)


def _attn_sample(q, cache_k, cache_v, k_new, v_new, rel_bias, lams, subln_w, lam_init):
    db, past, heads, hw = cache_k.shape
    tq = q.shape[2] // heads
    t = min(ATT_TILE, past)
    assert past % t == 0 and t >= REL_MAX_DISTANCE + tq
    n_tiles = past // t
    q_pos = past + np.arange(tq)
    far = _bias_table(rel_bias, q_pos, np.arange(t))
    assert n_tiles == 1 or np.all(_bucket(np.arange(past - t)[None, :] - q_pos[:, None]) == NUM_BUCKETS // 2 - 1)
    near = _bias_table(rel_bias, q_pos, past - t + np.arange(t))
    bias = jnp.stack([_head_expand(far, heads), _head_expand(near, heads)])
    bias_new = _head_expand(_bias_table(rel_bias, q_pos, q_pos), heads)
    rq = heads * tq
    n_sub = 2 if n_tiles % 2 == 0 else 1
    tb = t * n_sub
    vec = lambda n: pl.BlockSpec((1, n), lambda b, j: (0, 0))
    return pl.pallas_call(
        functools.partial(_attn_sample_kernel, lam_init=lam_init, n_tiles=n_tiles // n_sub),
        grid=(db, n_tiles // n_sub),
        in_specs=[pl.BlockSpec((None, 2, rq, HEAD_DIM), lambda b, j: (b, 0, 0, 0)),
                  pl.BlockSpec((None, tb, heads, hw), lambda b, j: (b, j, 0, 0)),
                  pl.BlockSpec((None, tb, heads, hw), lambda b, j: (b, j, 0, 0)),
                  pl.BlockSpec((None, tq * heads, hw), lambda b, j: (b, 0, 0)),
                  pl.BlockSpec((None, tq * heads, hw), lambda b, j: (b, 0, 0)),
                  pl.BlockSpec((2, 2, rq, t * heads), lambda b, j: (0, 0, 0, 0)),
                  pl.BlockSpec((2, rq, tq * heads), lambda b, j: (0, 0, 0)),
                  vec(HEAD_DIM), vec(HEAD_DIM), vec(HEAD_DIM), vec(HEAD_DIM), vec(hw)],
        out_specs=pl.BlockSpec((None, rq, hw), lambda b, j: (b, 0, 0)),
        out_shape=jax.ShapeDtypeStruct((db, rq, hw), BF16),
        scratch_shapes=[pltpu.VMEM((2, rq, 1), F32), pltpu.VMEM((2, rq, 1), F32), pltpu.VMEM((2, rq, hw), F32)],
        compiler_params=_params("arbitrary", "arbitrary"),
        name="attn_sample",
    )(q, cache_k, cache_v, k_new, v_new, bias, bias_new, *lams, subln_w)


CONV_HIST = 32


def _conv_kernel(cur_ref, hist_ref, w_ref, b_ref, g_ref, beta_ref, o_ref, xs_ref, wb_ref, *, width, zero_first):
    nb, tt, c = cur_ref.shape
    rows = 8
    length = CONV_HIST + tt

    @pl.when(jnp.logical_and(pl.program_id(0) == 0, pl.program_id(1) == 0))
    def _():
        for tap in range(width):
            wb_ref[tap] = jnp.broadcast_to(w_ref[tap:tap + 1, :], (rows, c))

    hist = hist_ref[...]
    if zero_first:
        hist = jnp.where(pl.program_id(1) == 0, 0.0, hist)
    xs_ref[0, :, 0:CONV_HIST, :] = hist
    xs_ref[0, :, CONV_HIST:, :] = cur_ref[...]
    for s in range(1, rows):
        xs_ref[s, :, 0:length - rows, :] = xs_ref[0, :, s:s + length - rows, :]
    lead = CONV_HIST - (width - 1)
    for n in range(nb):
        for r0 in range(0, tt, rows):
            acc = jnp.zeros((rows, c), F32) + b_ref[...]
            for tap in range(width):
                s = (lead + tap) % rows
                a = lead + tap - s + r0
                acc = acc + xs_ref[s, n, a:a + rows, :] * wb_ref[tap]
            mu = jnp.mean(acc, axis=-1, keepdims=True)
            cen = acc - mu
            var = jnp.mean(cen * cen, axis=-1, keepdims=True)
            y = cen * lax.rsqrt(var + LN_EPS) * g_ref[...] + beta_ref[...]
            o_ref[n, r0:r0 + rows, :] = _silu(y).astype(o_ref.dtype)


def _conv(cur3, hist3, hist_index, nb, tt, w, b, g, beta, zero_first):
    nseq, seq, c = cur3.shape
    width = w.shape[0]
    assert width - 1 <= CONV_HIST and seq % tt == 0 and nseq % nb == 0 and tt % 8 == 0
    vec = lambda: pl.BlockSpec((1, c), lambda s, i: (0, 0))
    return pl.pallas_call(
        functools.partial(_conv_kernel, width=width, zero_first=zero_first),
        grid=(nseq // nb, seq // tt),
        in_specs=[pl.BlockSpec((nb, tt, c), lambda s, i: (s, i, 0)),
                  pl.BlockSpec((nb, CONV_HIST, c), hist_index),
                  pl.BlockSpec((width, c), lambda s, i: (0, 0)),
                  vec(), vec(), vec()],
        out_specs=pl.BlockSpec((nb, tt, c), lambda s, i: (s, i, 0)),
        out_shape=jax.ShapeDtypeStruct(cur3.shape, BF16),
        scratch_shapes=[pltpu.VMEM((8, nb, CONV_HIST + tt, c), F32), pltpu.VMEM((width, 8, c), F32)],
        compiler_params=_params("arbitrary", "arbitrary"),
        name="conv",
    )(cur3, hist3, w, b, g, beta)


def _first_argmax(v, iota, axis, size):
    m = jnp.max(v, axis=axis, keepdims=True)
    i = jnp.min(jnp.where(v == m, iota, size), axis=axis, keepdims=True)
    return m, i


def _router_kernel(h_ref, w_ref, bias_ref, idx_ref, gate_ref, rank_ref, cnt_ref, carry_ref):
    n_exp = w_ref.shape[0]
    tm = h_ref.shape[0]
    per_group = n_exp // N_EXPERT_GROUPS

    @pl.when(pl.program_id(0) == 0)
    def _():
        carry_ref[...] = jnp.zeros(carry_ref.shape, F32)

    w = w_ref[...]
    w1 = w.astype(BF16)
    r1 = w - w1.astype(F32)
    w2 = r1.astype(BF16)
    w3 = (r1 - w2.astype(F32)).astype(BF16)
    h = h_ref[...]
    logits = _dot_nt(w1, h) + _dot_nt(w2, h) + _dot_nt(w3, h)
    scores = jax.nn.sigmoid(logits)
    choice = scores + bias_ref[...]

    ch3 = choice.reshape(N_EXPERT_GROUPS, per_group, tm)
    io3 = lax.broadcasted_iota(I32, ch3.shape, 1)
    m1, i1 = _first_argmax(ch3, io3, 1, per_group)
    m2 = jnp.max(jnp.where(io3 == i1, -jnp.inf, ch3), axis=1, keepdims=True)
    grp = (m1 + m2).reshape(N_EXPERT_GROUPS, tm)
    gio = lax.broadcasted_iota(I32, grp.shape, 0)
    keep = jnp.zeros(grp.shape, jnp.bool_)
    for _ in range(TOPK_GROUPS):
        _, gi = _first_argmax(grp, gio, 0, N_EXPERT_GROUPS)
        hit = gio == gi
        keep = jnp.logical_or(keep, hit)
        grp = jnp.where(hit, -jnp.inf, grp)
    keep3 = jnp.broadcast_to(keep.reshape(N_EXPERT_GROUPS, 1, tm), ch3.shape)
    masked = jnp.where(keep3, ch3, -jnp.inf).reshape(n_exp, tm)

    eio = lax.broadcasted_iota(I32, masked.shape, 0)
    sel = jnp.zeros(masked.shape, jnp.bool_)
    picks, pick_scores = [], []
    for _ in range(TOP_K):
        _, ei = _first_argmax(masked, eio, 0, n_exp)
        hit = eio == ei
        picks.append((ei, hit))
        pick_scores.append(jnp.sum(jnp.where(hit, scores, 0.0), axis=0, keepdims=True))
        sel = jnp.logical_or(sel, hit)
        masked = jnp.where(hit, -jnp.inf, masked)
    total = pick_scores[0]
    for s in pick_scores[1:]:
        total = total + s

    sel_b = jnp.where(sel, 1.0, 0.0).astype(BF16)
    tri = (lax.broadcasted_iota(I32, (tm, tm), 0) < lax.broadcasted_iota(I32, (tm, tm), 1))
    rank = jnp.dot(sel_b, jnp.where(tri, 1.0, 0.0).astype(BF16), preferred_element_type=F32) + carry_ref[...]
    for kk, (ei, hit) in enumerate(picks):
        idx_ref[kk:kk + 1, :] = ei
        gate_ref[kk:kk + 1, :] = pick_scores[kk] / total * ROUTED_SCALE
        rank_ref[kk:kk + 1, :] = jnp.sum(jnp.where(hit, rank, 0.0), axis=0, keepdims=True).astype(I32)
    carry_ref[...] = carry_ref[...] + jnp.sum(jnp.where(sel, 1.0, 0.0), axis=1, keepdims=True)
    cnt_ref[...] = carry_ref[...].astype(I32)


def _router(h2b, w_router_t, bias_col):
    t, d = h2b.shape
    n_exp = w_router_t.shape[0]
    tm = math.gcd(512, t)
    assert t % tm == 0
    tok = lambda: pl.BlockSpec((TOP_K, tm), lambda i: (0, i))
    return pl.pallas_call(
        _router_kernel,
        grid=(t // tm,),
        in_specs=[pl.BlockSpec((tm, d), lambda i: (i, 0)),
                  pl.BlockSpec((n_exp, d), lambda i: (0, 0)),
                  pl.BlockSpec((n_exp, 1), lambda i: (0, 0))],
        out_specs=[tok(), tok(), tok(), pl.BlockSpec((n_exp, 1), lambda i: (0, 0))],
        out_shape=[jax.ShapeDtypeStruct((TOP_K, t), I32), jax.ShapeDtypeStruct((TOP_K, t), F32),
                   jax.ShapeDtypeStruct((TOP_K, t), I32), jax.ShapeDtypeStruct((n_exp, 1), I32)],
        scratch_shapes=[pltpu.VMEM((n_exp, 1), F32)],
        compiler_params=_params("arbitrary"),
        name="router",
    )(h2b, w_router_t, bias_col)


def _dispatch_kernel(rows_ref, h_ref, xs_ref, sem):
    n_tok = h_ref.shape[0]

    def body(t, carry):
        for kk in range(TOP_K):
            pltpu.make_async_copy(h_ref.at[pl.ds(t, 1)], xs_ref.at[pl.ds(rows_ref[kk, t], 1)], sem).start()
        return carry

    lax.fori_loop(0, n_tok, body, 0)
    for _ in range(TOP_K):
        pltpu.make_async_copy(h_ref, xs_ref.at[pl.ds(0, n_tok)], sem).wait()


def _dispatch(rows, h2p, n_rows):
    t, w = h2p.shape
    nt = math.gcd(DISPATCH_ROWS, t)
    assert t % nt == 0 and n_rows >= nt
    rows = jnp.transpose(rows.reshape(TOP_K, t // nt, nt), (1, 0, 2))
    return pl.pallas_call(
        _dispatch_kernel,
        grid=(t // nt,),
        in_specs=[pl.BlockSpec((None, TOP_K, nt), lambda i: (i, 0, 0), memory_space=pltpu.SMEM),
                  pl.BlockSpec((nt, w), lambda i: (i, 0))],
        out_specs=pl.BlockSpec(memory_space=pl.ANY),
        out_shape=jax.ShapeDtypeStruct((n_rows, w), h2p.dtype),
        scratch_shapes=[pltpu.SemaphoreType.DMA(())],
        compiler_params=pltpu.CompilerParams(dimension_semantics=("arbitrary",), has_side_effects=True),
        name="dispatch",
    )(rows, h2p)


def _seg_kernel(se_ref, sb0_ref, snb_ref, srows_ref, nseg_ref, xs_hbm, wg_hbm, wu_hbm, wd_hbm, ys_hbm,
                xbuf, wgf, wuf, wdf, wgb, wub, wdb, gacc, uacc, hid, ybuf, sem_x, sem_w, sem_y, ycnt):
    s = pl.program_id(0)
    nseg = nseg_ref[0]
    kc, phases = EXP_KCHUNKS, EXP_KCHUNKS + EXP_NCHUNKS
    r = EXP_ROWS
    dk, dn = wgf.shape[1], wdf.shape[2]
    hw = dk // 2

    def w_copies(e, ph):
        slot = ph & 1
        out = []
        if ph < kc:
            step = dk // W_DMA_PARTS
            for part in range(W_DMA_PARTS):
                dst = pl.ds(part * step, step)
                src = pl.ds(ph * dk + part * step, step)
                out.append(pltpu.make_async_copy(wg_hbm.at[e, src, :], wgf.at[slot, dst, :], sem_w.at[0, slot]))
                out.append(pltpu.make_async_copy(wu_hbm.at[e, src, :], wuf.at[slot, dst, :], sem_w.at[1, slot]))
            return out
        step = wdf.shape[1] // W_DMA_PARTS
        for part in range(W_DMA_PARTS):
            rows = pl.ds(part * step, step)
            out.append(pltpu.make_async_copy(wd_hbm.at[e, rows, pl.ds((ph - kc) * dn, dn)], wdf.at[slot, rows, :],
                                             sem_w.at[2, slot]))
        return out

    def for_x_blocks(seg, slot, action):
        for b in range(SEG_BLOCKS):
            @pl.when(b < snb_ref[seg])
            def _():
                row = pl.multiple_of((sb0_ref[seg] + b) * r, r)
                action(pltpu.make_async_copy(xs_hbm.at[pl.ds(row, r), :], xbuf.at[slot, pl.ds(b * r, r), :],
                                             sem_x.at[slot]))

    def y_copy(slot, row, chunk):
        return pltpu.make_async_copy(ybuf.at[slot], ys_hbm.at[pl.ds(row, r), pl.ds(chunk * (dn // 2), dn // 2)],
                                     sem_y.at[slot])

    @pl.when(s < nseg)
    def _():
        e, nb, par = se_ref[s], snb_ref[s], s & 1
        row_base = sb0_ref[s] * r
        valid = srows_ref[s]

        @pl.when(s == 0)
        def _():
            ycnt[0] = 0
            for_x_blocks(0, 0, lambda c: c.start())
            for c in w_copies(e, 0):
                c.start()

        @pl.when(s + 1 < nseg)
        def _():
            for_x_blocks(s + 1, 1 - par, lambda c: c.start())

        for_x_blocks(s, par, lambda c: c.wait())

        for ph in range(phases):
            if ph + 1 < phases:
                for c in w_copies(e, ph + 1):
                    c.start()
            else:
                @pl.when(s + 1 < nseg)
                def _():
                    for c in w_copies(se_ref[s + 1], 0):
                        c.start()
            for c in w_copies(e, ph):
                c.wait()
            slot = ph & 1
            if ph < kc:
                wgb[...] = wgf[slot].astype(BF16)
                wub[...] = wuf[slot].astype(BF16)

                def body(b, carry, ph=ph):
                    r0 = pl.multiple_of(b * r, r)
                    rows = r0 + lax.broadcasted_iota(I32, (r, 1), 0)
                    words = xbuf[par, pl.ds(r0, r), ph * hw:(ph + 1) * hw]
                    lo, hi = _unpack_halves(jnp.where(rows < valid, words, jnp.uint32(0)))
                    g = (jnp.dot(lo, wgb[:hw, :], preferred_element_type=F32)
                         + jnp.dot(hi, wgb[hw:, :], preferred_element_type=F32))
                    u = (jnp.dot(lo, wub[:hw, :], preferred_element_type=F32)
                         + jnp.dot(hi, wub[hw:, :], preferred_element_type=F32))
                    if ph > 0:
                        g = g + gacc[pl.ds(r0, r), :]
                        u = u + uacc[pl.ds(r0, r), :]
                    if ph < kc - 1:
                        gacc[pl.ds(r0, r), :] = g
                        uacc[pl.ds(r0, r), :] = u
                    else:
                        hid[pl.ds(r0, r), :] = (_silu(g) * u).astype(BF16)
                    return carry
            else:
                wdb[...] = wdf[slot].astype(BF16)

                def body(b, carry, ph=ph):
                    r0 = pl.multiple_of(b * r, r)
                    n = ycnt[0]
                    ys = n & 1

                    @pl.when(n >= 2)
                    def _():
                        y_copy(ys, 0, 0).wait()

                    ybuf[ys] = _pack_halves(jnp.dot(hid[pl.ds(r0, r), :], wdb[...], preferred_element_type=F32))
                    y_copy(ys, pl.multiple_of(row_base + r0, r), ph - kc).start()
                    ycnt[0] = n + 1
                    return carry

            lax.fori_loop(0, nb, body, 0)

        @pl.when(s == nseg - 1)
        def _():
            n = ycnt[0]
            for back in (1, 2):
                @pl.when(n >= back)
                def _():
                    y_copy((n - back) & 1, 0, 0).wait()


def _expert_segments(counts, n_blocks_max):
    n_exp = counts.shape[0]
    span = SEG_BLOCKS * EXP_ROWS
    nb_e = (counts + EXP_ROWS - 1) // EXP_ROWS
    blk_start_e = jnp.cumsum(nb_e) - nb_e
    nseg_e = (nb_e + SEG_BLOCKS - 1) // SEG_BLOCKS
    seg_end_e = jnp.cumsum(nseg_e)
    n_seg_max = n_exp + n_blocks_max // SEG_BLOCKS
    sid = jnp.arange(n_seg_max, dtype=I32)
    seg_e = jnp.minimum(jnp.sum(seg_end_e[None, :] <= sid[:, None], axis=1), n_exp - 1).astype(I32)
    mine = seg_e[:, None] == jnp.arange(n_exp, dtype=I32)[None, :]
    of_expert = lambda v: jnp.sum(jnp.where(mine, v[None, :], 0), axis=1)
    part = sid - of_expert(seg_end_e - nseg_e)
    seg_nb = jnp.clip(of_expert(nb_e) - part * SEG_BLOCKS, 0, SEG_BLOCKS)
    seg_blk0 = of_expert(blk_start_e) + part * SEG_BLOCKS
    seg_rows = jnp.clip(of_expert(counts) - part * span, 0, span)
    to_i32 = lambda v: v.astype(I32)
    return (seg_e, to_i32(seg_blk0), to_i32(seg_nb), to_i32(seg_rows), to_i32(seg_end_e[-1:]),
            to_i32(blk_start_e * EXP_ROWS))


def _experts_by_segment(seg_tables, xs, w_gate, w_up, w_down):
    n_exp, d, ff = w_gate.shape
    kc, nc = EXP_KCHUNKS, EXP_NCHUNKS
    dk, dn = d // kc, d // nc
    span = SEG_BLOCKS * EXP_ROWS
    assert (kc + nc) % 2 == 0
    hbm = lambda: pl.BlockSpec(memory_space=pl.ANY)
    grid_spec = pltpu.PrefetchScalarGridSpec(
        num_scalar_prefetch=5,
        grid=(seg_tables[0].shape[0],),
        in_specs=[hbm(), hbm(), hbm(), hbm()],
        out_specs=hbm(),
        scratch_shapes=[pltpu.VMEM((2, span, d // 2), U32),
                        pltpu.VMEM((2, dk, ff), F32), pltpu.VMEM((2, dk, ff), F32), pltpu.VMEM((2, ff, dn), F32),
                        pltpu.VMEM((dk, ff), BF16), pltpu.VMEM((dk, ff), BF16), pltpu.VMEM((ff, dn), BF16),
                        pltpu.VMEM((span, ff), F32), pltpu.VMEM((span, ff), F32), pltpu.VMEM((span, ff), BF16),
                        pltpu.VMEM((2, EXP_ROWS, dn // 2), U32),
                        pltpu.SemaphoreType.DMA((2,)), pltpu.SemaphoreType.DMA((3, 2)),
                        pltpu.SemaphoreType.DMA((2,)), pltpu.SMEM((1,), I32)],
    )
    return pl.pallas_call(
        _seg_kernel,
        grid_spec=grid_spec,
        out_shape=jax.ShapeDtypeStruct((xs.shape[0], d // 2), U32),
        compiler_params=_params("arbitrary"),
        name="experts",
    )(*seg_tables, xs, w_gate, w_up, w_down)


def _combine_kernel(rows_ref, next_rows_ref, gate_ref, hs_ref, wsd_ref, ys_ref, f_ref, buf_ref, wb_ref, sem):
    i = pl.program_id(0)
    n_tok = gate_ref.shape[0]
    cur = i & 1

    def gather(table_ref, slot, action):
        def body(t, carry):
            for kk in range(TOP_K):
                action(pltpu.make_async_copy(ys_ref.at[pl.ds(table_ref[kk, t], 1)],
                                             buf_ref.at[slot, kk, pl.ds(t, 1)], sem.at[slot]))
            return carry
        lax.fori_loop(0, n_tok, body, 0)

    @pl.when(i == 0)
    def _():
        wb_ref[...] = wsd_ref[...].astype(BF16)
        gather(rows_ref, 0, lambda c: c.start())

    @pl.when(i + 1 < pl.num_programs(0))
    def _():
        gather(next_rows_ref, 1 - cur, lambda c: c.start())

    shared = jnp.dot(hs_ref[...], wb_ref[...], preferred_element_type=F32)
    for kk in range(TOP_K):
        pltpu.make_async_copy(ys_ref.at[pl.ds(0, n_tok)], buf_ref.at[cur, kk], sem.at[cur]).wait()
    gates = gate_ref[...]
    dn = f_ref.shape[1] // EXP_NCHUNKS
    hw = dn // 2
    for c in range(EXP_NCHUNKS):
        lo = shared[:, c * dn:c * dn + hw]
        hi = shared[:, c * dn + hw:(c + 1) * dn]
        for kk in range(TOP_K):
            words = buf_ref[cur, kk, :, c * hw:(c + 1) * hw]
            g = gates[:, kk:kk + 1]
            lo = lo + lax.bitcast_convert_type(words << 16, F32) * g
            hi = hi + lax.bitcast_convert_type(words & jnp.uint32(0xFFFF0000), F32) * g
        f_ref[:, c * dn:c * dn + hw] = lo
        f_ref[:, c * dn + hw:(c + 1) * dn] = hi


def _combine(rows, gates_t, hid_sh, w_sh_down, ys):
    t, ff = hid_sh.shape
    d = w_sh_down.shape[1]
    nt = math.gcd(COMBINE_ROWS, t)
    n_steps = t // nt
    rows = jnp.transpose(rows.reshape(TOP_K, n_steps, nt), (1, 0, 2))
    rows_spec = lambda ahead: pl.BlockSpec((None, TOP_K, nt), lambda i: (jnp.minimum(i + ahead, n_steps - 1), 0, 0),
                                           memory_space=pltpu.SMEM)
    return pl.pallas_call(
        _combine_kernel,
        grid=(n_steps,),
        in_specs=[rows_spec(0), rows_spec(1),
                  pl.BlockSpec((nt, TOP_K), lambda i: (i, 0)),
                  pl.BlockSpec((nt, ff), lambda i: (i, 0)),
                  pl.BlockSpec((ff, d), lambda i: (0, 0)),
                  pl.BlockSpec(memory_space=pl.ANY)],
        out_specs=pl.BlockSpec((nt, d), lambda i: (i, 0)),
        out_shape=jax.ShapeDtypeStruct((t, d), F32),
        scratch_shapes=[pltpu.VMEM((2, TOP_K, nt, d // 2), U32), pltpu.VMEM((ff, d), BF16),
                        pltpu.SemaphoreType.DMA((2,))],
        compiler_params=_params("arbitrary"),
        name="combine",
    )(rows, rows, gates_t, hid_sh, w_sh_down, ys)


def _layer(l, lam_init, x_prompt, x_sample, c_prompt, c_sample, cache_k, cache_v, state_conv, rel_bias, p):
    batch, seq, d = x_prompt.shape
    db, dseq, _ = x_sample.shape
    past, heads = cache_k.shape[1], cache_k.shape[2]
    hw = 2 * HEAD_DIM
    aw = heads * hw
    cc = p["conv_dw_w"].shape[1]
    width = p["conv_dw_w"].shape[0]
    assert p["w_in"].shape[1] == 3 * aw + 2 * cc and seq % dseq == 0
    tp, ts = batch * seq, db * dseq
    row = lambda v: v.reshape(1, -1)

    n_mod_rows = -(-(db + batch) // 16) * 16
    c_all = jnp.concatenate([c_sample, c_prompt, jnp.zeros((n_mod_rows - db - batch, d), F32)], axis=0)
    mod3 = _ada(c_all, p["w_ada"], row(p["b_ada"])).reshape(n_mod_rows, 1, N_MOD * d)

    gp = tp // dseq
    gps = seq // dseq
    nb_p = math.gcd(ROW_GROUPS, gps)
    nb_s = math.gcd(ROW_GROUPS, db)
    assert gps % nb_p == 0 and db % nb_s == 0 and gp % nb_s == 0
    plan_p = _RowPlan(gp, dseq, d, nb_p, 1, lambda i: db + (i * nb_p) // gps, 0)
    plan_s = _RowPlan(db, dseq, d, nb_s, nb_s, lambda i: i, gp // nb_s)
    xp3 = x_prompt.reshape(gp, dseq, d)

    hp = _prenorm(plan_p, xp3, row(p["g_pre_mix"]), mod3).reshape(tp, d)
    hs = _prenorm(plan_s, x_sample, row(p["g_pre_mix"]), mod3).reshape(ts, d)

    w_in = p["w_in"]
    proj = lambda h, off, nm: _mm([h], [(w_in, off)], aw, _identity, F32, 512, 512, nm)
    qp, kp, vp = proj(hp, 0, "q_prompt"), proj(hp, aw, "k_prompt"), proj(hp, 2 * aw, "v_prompt")
    qs, ks, vs = proj(hs, 0, "q_sample"), proj(hs, aw, "k_sample"), proj(hs, 2 * aw, "v_sample")
    glu_cols = [(w_in, 3 * aw), (w_in, 3 * aw + cc)]
    glu_p = _mm([hp], glu_cols, cc, _glu, F32, 512, 256, "glu_prompt")
    glu_s = _mm([hs], glu_cols, cc, _glu, F32, 512, 256, "glu_sample")

    lams = [row(p[n]) for n in ("lambda_q1", "lambda_k1", "lambda_q2", "lambda_k2")]
    subln = row(p["subln_w"])
    attn_p = _attn_prompt(qp, kp, vp, rel_bias, lams, subln, batch, seq, heads, lam_init)

    new_k_s = ks.reshape(db, dseq, heads, hw)
    new_v_s = vs.reshape(db, dseq, heads, hw)
    q_s = jnp.transpose(qs.reshape(db, dseq, heads, 2, HEAD_DIM), (0, 3, 2, 1, 4)).reshape(db, 2, heads * dseq, HEAD_DIM)
    o_s = _attn_sample(q_s, cache_k, cache_v, new_k_s.reshape(db, dseq * heads, hw),
                       new_v_s.reshape(db, dseq * heads, hw), rel_bias, lams, subln, lam_init)
    attn_s = jnp.transpose(o_s.reshape(db, heads, dseq, hw), (0, 2, 1, 3)).reshape(ts, aw)

    conv_args = (p["conv_dw_w"], row(p["conv_dw_b"]), row(p["conv_ln_g"]), row(p["conv_ln_b"]))
    glu_p3 = glu_p.reshape(batch, seq, cc)
    tt = min(128, seq)
    per = tt // CONV_HIST
    conv_p = _conv(glu_p3, glu_p3, lambda s, i: (s, jnp.maximum(i * per - 1, 0), 0), 1, tt, *conv_args,
                   zero_first=True).reshape(tp, cc)
    glu_s3 = glu_s.reshape(db, dseq, cc)
    hist_s = jnp.concatenate([jnp.zeros((db, CONV_HIST - (width - 1), cc), F32), state_conv], axis=1)
    nb_c = math.gcd(2, db)
    conv_s = _conv(glu_s3, hist_s, lambda s, i: (s, 0, 0), nb_c, dseq, *conv_args, zero_first=False).reshape(ts, cc)
    new_conv_p = glu_p3[:, seq - (width - 1):]
    new_conv_s = jnp.concatenate([state_conv, glu_s3], axis=1)[:, -(width - 1):]

    mix_p = _mm([attn_p, conv_p], [(p["w_out"], 0)], d, _identity, F32, 512, 512, "out_prompt")
    mix_s = _mm([attn_s, conv_s], [(p["w_out"], 0)], d, _identity, F32, 512, 512, "out_sample")

    gt = gp + db
    gpost, gpre = row(p["g_post_mix"]), row(p["g_pre_ffn"])
    x1p, h2ba, h2pa = _postmix(plan_p, xp3, mix_p.reshape(gp, dseq, d), gpost, gpre, mod3, gt)
    x1s, h2ba, h2pa = _postmix(plan_s, x_sample, mix_s.reshape(db, dseq, d), gpost, gpre, mod3, gt,
                               prev=(h2ba, h2pa))
    t = tp + ts
    h2b = h2ba.reshape(t, d)
    h2p = h2pa.reshape(t, d // 2)

    n_exp = p["w_router"].shape[1]
    idx, gates, rank, counts = _router(h2b, p["w_router"].T, p["b_router_corr"].reshape(n_exp, 1))
    n_blocks_max = (t * TOP_K) // EXP_ROWS + n_exp
    *seg_tables, row_start = _expert_segments(counts.reshape(n_exp), n_blocks_max)
    pick = idx[..., None] == jnp.arange(n_exp, dtype=I32)
    rows = jnp.sum(jnp.where(pick, row_start, 0), axis=-1) + rank

    xs = _dispatch(rows.astype(I32), h2p, n_blocks_max * EXP_ROWS)
    ys = _experts_by_segment(seg_tables, xs, p["w_exp_gate"], p["w_exp_up"], p["w_exp_down"])
    ff_sh = p["w_sh_gate"].shape[1]
    hid_sh = _mm([h2b], [(p["w_sh_gate"], 0), (p["w_sh_up"], 0)], ff_sh, _swiglu, BF16, 512, 256, "shared_up")
    f = _combine(rows.astype(I32), gates.T, hid_sh, p["w_sh_down"], ys)

    f3 = f.reshape(gt, dseq, d)
    gpf = row(p["g_post_ffn"])
    yp = _final(plan_p, x1p, f3, gpf, mod3).reshape(batch, seq, d)
    ysmp = _final(plan_s, x1s, f3, gpf, mod3)
    new_k_p = kp.reshape(batch, seq, heads, hw)
    new_v_p = vp.reshape(batch, seq, heads, hw)
    return yp, ysmp, new_k_p, new_v_p, new_conv_p, new_k_s, new_v_s, new_conv_s


def kernel(x_prompt, x_sample, c_prompt, c_sample, cache_k, cache_v, state_conv, rel_bias, w_ada, b_ada, g_pre_mix, g_post_mix, g_pre_ffn, g_post_ffn, w_in, lambda_q1, lambda_k1, lambda_q2, lambda_k2, subln_w, conv_dw_w, conv_dw_b, conv_ln_g, conv_ln_b, w_out, w_router, b_router_corr, w_exp_gate, w_exp_up, w_exp_down, w_sh_gate, w_sh_up, w_sh_down):
    weights = dict(w_ada=w_ada, b_ada=b_ada, g_pre_mix=g_pre_mix, g_post_mix=g_post_mix, g_pre_ffn=g_pre_ffn,
                   g_post_ffn=g_post_ffn, w_in=w_in, lambda_q1=lambda_q1, lambda_k1=lambda_k1,
                   lambda_q2=lambda_q2, lambda_k2=lambda_k2, subln_w=subln_w, conv_dw_w=conv_dw_w,
                   conv_dw_b=conv_dw_b, conv_ln_g=conv_ln_g, conv_ln_b=conv_ln_b, w_out=w_out,
                   w_router=w_router, b_router_corr=b_router_corr, w_exp_gate=w_exp_gate, w_exp_up=w_exp_up,
                   w_exp_down=w_exp_down, w_sh_gate=w_sh_gate, w_sh_up=w_sh_up, w_sh_down=w_sh_down)
    depth = w_in.shape[0]
    xp, xs = x_prompt, x_sample
    outs = [[] for _ in range(6)]
    for l in range(depth):
        p = {k: (v.reshape(v.shape[1:]) if depth == 1 else v[l]) for k, v in weights.items()}
        lam_init = 0.8 - 0.6 * math.exp(-0.3 * l)
        ck, cv, sc = ((a.reshape(a.shape[1:]) if depth == 1 else a[l]) for a in (cache_k, cache_v, state_conv))
        xp, xs, *state = _layer(l, lam_init, xp, xs, c_prompt, c_sample, ck, cv, sc, rel_bias, p)
        for acc, s in zip(outs, state):
            acc.append(s)
    return (xp, xs) + tuple(jnp.stack(o) for o in outs)
```

```python
import functools
import math

import numpy as np
import jax
import jax.numpy as jnp
from jax import lax
from jax.experimental import pallas as pl
from jax.experimental.pallas import tpu as pltpu

F32 = jnp.float32
BF16 = jnp.bfloat16
I32 = jnp.int32
U32 = jnp.uint32

CHUNK = 64
HEAD_DIM = 128
NUM_BUCKETS = 32
REL_MAX_DISTANCE = 128
TOP_K = 8
N_EXPERT_GROUPS = 8
TOPK_GROUPS = 4
ROUTED_SCALE = 2.5
RMS_EPS = 1e-6
SUBLN_EPS = 1e-5
LN_EPS = 1e-5
N_MOD = 6
NEG = -1e30
ATT_SCALE = HEAD_DIM ** -0.5

VMEM_LIMIT_BYTES = 56 * 1024 * 1024
ATT_TILE = 256
EXP_ROWS = 128
SEG_BLOCKS = 4
W_DMA_PARTS = 4
EXP_KCHUNKS = 2
EXP_NCHUNKS = 2
COMBINE_ROWS = 128
DISPATCH_ROWS = 256
ROW_GROUPS = 4
MM_ROWS = 1024


def _params(*sem):
    return pltpu.CompilerParams(dimension_semantics=sem, vmem_limit_bytes=VMEM_LIMIT_BYTES)


def _silu(x):
    return x * jax.nn.sigmoid(x)


def _dot_nt(a, b):
    return lax.dot_general(a, b, (((1,), (1,)), ((), ())), preferred_element_type=F32)


def _ada_kernel(c_ref, w_ref, b_ref, o_ref):
    a = _silu(c_ref[...]).astype(BF16)
    o_ref[...] = jnp.dot(a, w_ref[...].astype(BF16), preferred_element_type=F32) + b_ref[...]


def _ada(c_all, w_ada, b_ada):
    rows, d = c_all.shape
    n = w_ada.shape[1]
    tn = min(512, n)
    return pl.pallas_call(
        _ada_kernel,
        grid=(n // tn,),
        in_specs=[pl.BlockSpec((rows, d), lambda j: (0, 0)),
                  pl.BlockSpec((d, tn), lambda j: (0, j)),
                  pl.BlockSpec((1, tn), lambda j: (0, j))],
        out_specs=pl.BlockSpec((rows, tn), lambda j: (0, j)),
        out_shape=jax.ShapeDtypeStruct((rows, n), F32),
        compiler_params=_params("arbitrary"),
        name="ada",
    )(c_all, w_ada, b_ada)


class _RowPlan:
    def __init__(self, n_groups, group_rows, d, nb, mod_nb, mod_index, out_block_offset=0):
        self.n_groups, self.group_rows, self.d, self.nb = n_groups, group_rows, d, nb
        self.mod_nb, self.mod_index, self.out_block_offset = mod_nb, mod_index, out_block_offset
        self.grid = (n_groups // nb,)

    def act(self, offset=0):
        return pl.BlockSpec((self.nb, self.group_rows, self.d), lambda i: (i + offset, 0, 0))

    def mod(self, chunk):
        return pl.BlockSpec((self.mod_nb, 1, self.d), lambda i: (self.mod_index(i), 0, chunk))

    def vec(self):
        return pl.BlockSpec((1, self.d), lambda i: (0, 0))


def _rms(x, g, eps):
    return x * lax.rsqrt(jnp.mean(x * x, axis=-1, keepdims=True) + eps) * g


def _prenorm_kernel(x_ref, g_ref, sh_ref, sc_ref, o_ref):
    y = _rms(x_ref[...], g_ref[...], RMS_EPS)
    o_ref[...] = (y * (1.0 + sc_ref[...]) + sh_ref[...]).astype(o_ref.dtype)


def _prenorm(plan, x3, g, mod3):
    return pl.pallas_call(
        _prenorm_kernel,
        grid=plan.grid,
        in_specs=[plan.act(), plan.vec(), plan.mod(0), plan.mod(1)],
        out_specs=plan.act(),
        out_shape=jax.ShapeDtypeStruct(x3.shape, BF16),
        compiler_params=_params("arbitrary"),
        name="prenorm",
    )(x3, g, mod3, mod3)


def _pack_halves(x):
    n = x.shape[-1] // 2
    lo = lax.bitcast_convert_type(x[..., :n].astype(BF16).astype(F32), U32) >> 16
    hi = lax.bitcast_convert_type(x[..., n:].astype(BF16).astype(F32), U32) & jnp.uint32(0xFFFF0000)
    return hi | lo


def _unpack_halves(w):
    lo = lax.bitcast_convert_type(w << 16, F32).astype(BF16)
    hi = lax.bitcast_convert_type(w & jnp.uint32(0xFFFF0000), F32).astype(BF16)
    return lo, hi


def _postmix_kernel(x_ref, mix_ref, gpost_ref, gpre_ref, g1_ref, sh2_ref, sc2_ref, *rest):
    x1_ref, h2b_ref, h2p_ref = rest[-3:]
    x1 = x_ref[...] + g1_ref[...] * _rms(mix_ref[...], gpost_ref[...], RMS_EPS)
    x1_ref[...] = x1
    h2 = _rms(x1, gpre_ref[...], RMS_EPS) * (1.0 + sc2_ref[...]) + sh2_ref[...]
    h2b_ref[...] = h2.astype(BF16)
    dk = h2.shape[-1] // EXP_KCHUNKS
    for c in range(EXP_KCHUNKS):
        h2p_ref[:, :, c * (dk // 2):(c + 1) * (dk // 2)] = _pack_halves(h2[:, :, c * dk:(c + 1) * dk])


def _postmix(plan, x3, mix3, gpost, gpre, mod3, total_groups, prev=None):
    off = plan.out_block_offset
    in_specs = [plan.act(), plan.act(), plan.vec(), plan.vec(), plan.mod(2), plan.mod(3), plan.mod(4)]
    args = [x3, mix3, gpost, gpre, mod3, mod3, mod3]
    aliases = {}
    if prev is not None:
        in_specs += [pl.BlockSpec(memory_space=pl.ANY), pl.BlockSpec(memory_space=pl.ANY)]
        args += list(prev)
        aliases = {7: 1, 8: 2}
    shape_all = (total_groups, plan.group_rows, plan.d)
    shape_packed = (total_groups, plan.group_rows, plan.d // 2)
    packed_spec = pl.BlockSpec((plan.nb, plan.group_rows, plan.d // 2), lambda i: (i + off, 0, 0))
    return pl.pallas_call(
        _postmix_kernel,
        grid=plan.grid,
        in_specs=in_specs,
        out_specs=[plan.act(), plan.act(off), packed_spec],
        out_shape=[jax.ShapeDtypeStruct(x3.shape, F32),
                   jax.ShapeDtypeStruct(shape_all, BF16),
                   jax.ShapeDtypeStruct(shape_packed, U32)],
        input_output_aliases=aliases,
        compiler_params=_params("arbitrary"),
        name="postmix",
    )(*args)


def _final_kernel(x1_ref, f_ref, gpost_ref, g2_ref, o_ref):
    o_ref[...] = x1_ref[...] + g2_ref[...] * _rms(f_ref[...], gpost_ref[...], RMS_EPS)


def _final(plan, x13, f3, gpost, mod3):
    return pl.pallas_call(
        _final_kernel,
        grid=plan.grid,
        in_specs=[plan.act(), plan.act(plan.out_block_offset), plan.vec(), plan.mod(5)],
        out_specs=plan.act(),
        out_shape=jax.ShapeDtypeStruct(x13.shape, F32),
        compiler_params=_params("arbitrary"),
        name="final",
    )(x13, f3, gpost, mod3)


def _mm_kernel(*refs, n_a, n_w, k_sizes, epilogue):
    a_refs = refs[:n_a]
    w_refs = refs[n_a:n_a + n_w]
    o_ref = refs[n_a + n_w]
    wb_refs = refs[n_a + n_w + 1:]

    @pl.when(pl.program_id(1) == 0)
    def _():
        for w_ref, wb_ref in zip(w_refs, wb_refs):
            wb_ref[...] = w_ref[...].astype(BF16)

    parts = []
    for wb_ref in wb_refs:
        acc, k0 = None, 0
        for a_ref, ka in zip(a_refs, k_sizes):
            d = jnp.dot(a_ref[...], wb_ref[k0:k0 + ka, :], preferred_element_type=F32)
            acc = d if acc is None else acc + d
            k0 += ka
        parts.append(acc)
    o_ref[...] = epilogue(*parts).astype(o_ref.dtype)


def _mm(a_list, w_cols, n_out, epilogue, out_dtype, tm, tn, name):
    m = a_list[0].shape[0]
    k = w_cols[0][0].shape[0]
    tm, tn = math.gcd(tm, m), math.gcd(tn, n_out)
    k_sizes = tuple(a.shape[1] for a in a_list)
    assert sum(k_sizes) == k and m % tm == 0 and n_out % tn == 0 and all(c % tn == 0 for _, c in w_cols)
    a_specs = [pl.BlockSpec((tm, ka), lambda j, i: (i, 0)) for ka in k_sizes]
    w_specs = [pl.BlockSpec((k, tn), functools.partial(lambda j, i, o: (0, o + j), o=c // tn)) for _, c in w_cols]
    kern = functools.partial(_mm_kernel, n_a=len(a_list), n_w=len(w_cols), k_sizes=k_sizes, epilogue=epilogue)
    return pl.pallas_call(
        kern,
        grid=(n_out // tn, m // tm),
        in_specs=a_specs + w_specs,
        out_specs=pl.BlockSpec((tm, tn), lambda j, i: (i, j)),
        out_shape=jax.ShapeDtypeStruct((m, n_out), out_dtype),
        scratch_shapes=[pltpu.VMEM((k, tn), BF16) for _ in w_cols],
        compiler_params=_params("arbitrary", "arbitrary"),
        name=name,
    )(*a_list, *[w for w, _ in w_cols])


def _identity(x):
    return x


def _glu(a, b):
    return a * jax.nn.sigmoid(b)


def _swiglu(a, b):
    return _silu(a) * b


def _bucket(rel):
    half = NUM_BUCKETS // 2
    max_exact = half // 2
    n = np.abs(rel)
    nf = np.maximum(n, 1).astype(np.float32)
    large = max_exact + (np.log(nf / np.float32(max_exact)) / np.float32(math.log(REL_MAX_DISTANCE / max_exact))
                         * np.float32(half - max_exact)).astype(np.int32)
    large = np.minimum(large, half - 1)
    return np.where(rel > 0, half, 0) + np.where(n < max_exact, n, large)


def _bias_table(rel_bias, q_pos, k_pos):
    rel = k_pos[None, :] - q_pos[:, None]
    visible = (k_pos // CHUNK)[None, :] <= (q_pos // CHUNK)[:, None]
    onehot = (jnp.asarray(_bucket(rel))[..., None] == jnp.arange(NUM_BUCKETS)).astype(F32)
    b = jnp.einsum("qkb,bhm->hmqk", onehot, rel_bias.astype(F32), precision=lax.Precision.HIGHEST)
    return jnp.where(jnp.asarray(visible)[None, None], b, NEG)


def _lambda(lq1, lk1, lq2, lk2, lam_init):
    return (jnp.exp(jnp.sum(lq1[...] * lk1[...], keepdims=True))
            - jnp.exp(jnp.sum(lq2[...] * lk2[...], keepdims=True)) + lam_init)


def _softmax_step(s, v, m_ref, l_ref, acc_ref, idx):
    m_old = m_ref[idx]
    m_new = jnp.maximum(m_old, jnp.max(s, axis=-1, keepdims=True))
    alpha = jnp.exp(m_old - m_new)
    p = jnp.exp(s - m_new)
    l_ref[idx] = alpha * l_ref[idx] + jnp.sum(p, axis=-1, keepdims=True)
    acc_ref[idx] = alpha * acc_ref[idx] + jnp.dot(p.astype(BF16), v, preferred_element_type=F32)
    m_ref[idx] = m_new


def _attn_finish(m_ref, l_ref, acc_ref, lam, sw, lam_init, o_ref):
    o = acc_ref[0] / l_ref[0] - lam * (acc_ref[1] / l_ref[1])
    o = o * lax.rsqrt(jnp.mean(o * o, axis=-1, keepdims=True) + SUBLN_EPS) * sw * (1.0 - lam_init)
    o_ref[...] = o.astype(o_ref.dtype)


def _attn_prompt_kernel(q_ref, k_ref, v_ref, b_ref, lq1, lk1, lq2, lk2, sw_ref, o_ref, kb_ref, vb_ref,
                        *, lam_init, nq):
    t = ATT_TILE
    kb_ref[...] = k_ref[...].astype(BF16)
    vb_ref[...] = v_ref[...].astype(BF16)
    lam = _lambda(lq1, lk1, lq2, lk2, lam_init)
    sw = sw_ref[...]
    for qi in range(nq):
        q = q_ref[qi * t:(qi + 1) * t, :]
        n_far = max(qi - 1, 0) * t
        exps, sums = [], []
        for mp in range(2):
            cols = slice(mp * HEAD_DIM, (mp + 1) * HEAD_DIM)
            s = _dot_nt(q[:, cols].astype(BF16), kb_ref[0:(qi + 1) * t, cols]) * ATT_SCALE
            pieces = []
            if n_far:
                pieces.append(s[:, :n_far] + b_ref[mp, 2, 0:1, 0:1])
            if qi >= 1:
                pieces.append(s[:, n_far:n_far + t] + b_ref[mp, 1])
            pieces.append(s[:, qi * t:(qi + 1) * t] + b_ref[mp, 0])
            m = functools.reduce(jnp.maximum, [jnp.max(p, axis=-1, keepdims=True) for p in pieces])
            es = [jnp.exp(p - m) for p in pieces]
            exps.append(es)
            sums.append(functools.reduce(jnp.add, [jnp.sum(e, axis=-1, keepdims=True) for e in es]))
        c1 = 1.0 / sums[0]
        c2 = lam / sums[1]
        o, col = None, 0
        for e1, e2 in zip(exps[0], exps[1]):
            a = (e1 * c1 - e2 * c2).astype(BF16)
            d = jnp.dot(a, vb_ref[col:col + a.shape[1], :], preferred_element_type=F32)
            o = d if o is None else o + d
            col += a.shape[1]
        o = o * lax.rsqrt(jnp.mean(o * o, axis=-1, keepdims=True) + SUBLN_EPS) * sw * (1.0 - lam_init)
        o_ref[qi * t:(qi + 1) * t, :] = o.astype(o_ref.dtype)


def _attn_prompt(q, k, v, rel_bias, lams, subln_w, batch, seq, heads, lam_init):
    t = ATT_TILE
    assert seq % t == 0 and t % CHUNK == 0
    pos = np.arange(t)
    tiles = [_bias_table(rel_bias, pos + dt * t, pos) for dt in range(3)]
    assert np.all(_bucket(np.arange(-3 * t + 1, -t)) == NUM_BUCKETS // 2 - 1)
    bias = jnp.stack(tiles, axis=2)
    hw = 2 * HEAD_DIM
    vec = lambda n: pl.BlockSpec((1, n), lambda b, h: (0, 0))
    seq_spec = lambda: pl.BlockSpec((seq, hw), lambda b, h: (b, h))
    return pl.pallas_call(
        functools.partial(_attn_prompt_kernel, lam_init=lam_init, nq=seq // t),
        grid=(batch, heads),
        in_specs=[seq_spec(), seq_spec(), seq_spec(),
                  pl.BlockSpec((None, 2, 3, t, t), lambda b, h: (h, 0, 0, 0, 0)),
                  vec(HEAD_DIM), vec(HEAD_DIM), vec(HEAD_DIM), vec(HEAD_DIM), vec(hw)],
        out_specs=seq_spec(),
        out_shape=jax.ShapeDtypeStruct((batch * seq, heads * hw), BF16),
        scratch_shapes=[pltpu.VMEM((seq, hw), BF16), pltpu.VMEM((seq, hw), BF16)],
        compiler_params=_params("arbitrary", "arbitrary"),
        name="attn_prompt",
    )(q, k, v, bias, *lams, subln_w)


def _attn_sample_kernel(q_ref, ck_ref, cv_ref, kn_ref, vn_ref, b_ref, bn_ref, lq1, lk1, lq2, lk2, sw_ref, o_ref,
                        m_ref, l_ref, acc_ref, *, lam_init, n_tiles):
    kt_i = pl.program_id(1)

    @pl.when(kt_i == 0)
    def _():
        m_ref[...] = jnp.full(m_ref.shape, NEG, F32)
        l_ref[...] = jnp.zeros(l_ref.shape, F32)
        acc_ref[...] = jnp.zeros(acc_ref.shape, F32)

    def step(k2d, v2d, bias_of_map):
        kb = k2d.astype(BF16)
        vb = v2d.astype(BF16)
        for mp in range(2):
            qm = q_ref[mp].astype(BF16)
            s = _dot_nt(qm, kb[:, mp * HEAD_DIM:(mp + 1) * HEAD_DIM]) * ATT_SCALE + bias_of_map(mp)
            _softmax_step(s, vb, m_ref, l_ref, acc_ref, mp)

    t = b_ref.shape[3] // ck_ref.shape[1]
    n_sub = ck_ref.shape[0] // t
    for sub in range(n_sub):
        keys = slice(sub * t, (sub + 1) * t)
        bi = jnp.where(kt_i == n_tiles - 1, 1, 0) if sub == n_sub - 1 else 0
        step(ck_ref[keys].reshape(t * ck_ref.shape[1], ck_ref.shape[2]),
             cv_ref[keys].reshape(t * cv_ref.shape[1], cv_ref.shape[2]), lambda mp, bi=bi: b_ref[bi, mp])

    @pl.when(kt_i == n_tiles - 1)
    def _():
        step(kn_ref[...], vn_ref[...], lambda mp: bn_ref[mp])
        sw = sw_ref[...]
        _attn_finish(m_ref, l_ref, acc_ref, _lambda(lq1, lk1, lq2, lk2, lam_init), sw, lam_init, o_ref)


def _head_expand(bias, heads):
    h, _, tq, tk = bias.shape
    eye = jnp.asarray(np.eye(heads, dtype=bool))
    full = jnp.where(eye[:, None, None, None, :], bias[..., None], NEG)
    return jnp.transpose(full, (1, 0, 2, 3, 4)).reshape(2, h * tq, tk * heads)


def _attn_sample(q, cache_k, cache_v, k_new, v_new, rel_bias, lams, subln_w, lam_init):
    db, past, heads, hw = cache_k.shape
    tq = q.shape[2] // heads
    t = min(ATT_TILE, past)
    assert past % t == 0 and t >= REL_MAX_DISTANCE + tq
    n_tiles = past // t
    q_pos = past + np.arange(tq)
    far = _bias_table(rel_bias, q_pos, np.arange(t))
    assert n_tiles == 1 or np.all(_bucket(np.arange(past - t)[None, :] - q_pos[:, None]) == NUM_BUCKETS // 2 - 1)
    near = _bias_table(rel_bias, q_pos, past - t + np.arange(t))
    bias = jnp.stack([_head_expand(far, heads), _head_expand(near, heads)])
    bias_new = _head_expand(_bias_table(rel_bias, q_pos, q_pos), heads)
    rq = heads * tq
    n_sub = 2 if n_tiles % 2 == 0 else 1
    tb = t * n_sub
    vec = lambda n: pl.BlockSpec((1, n), lambda b, j: (0, 0))
    return pl.pallas_call(
        functools.partial(_attn_sample_kernel, lam_init=lam_init, n_tiles=n_tiles // n_sub),
        grid=(db, n_tiles // n_sub),
        in_specs=[pl.BlockSpec((None, 2, rq, HEAD_DIM), lambda b, j: (b, 0, 0, 0)),
                  pl.BlockSpec((None, tb, heads, hw), lambda b, j: (b, j, 0, 0)),
                  pl.BlockSpec((None, tb, heads, hw), lambda b, j: (b, j, 0, 0)),
                  pl.BlockSpec((None, tq * heads, hw), lambda b, j: (b, 0, 0)),
                  pl.BlockSpec((None, tq * heads, hw), lambda b, j: (b, 0, 0)),
                  pl.BlockSpec((2, 2, rq, t * heads), lambda b, j: (0, 0, 0, 0)),
                  pl.BlockSpec((2, rq, tq * heads), lambda b, j: (0, 0, 0)),
                  vec(HEAD_DIM), vec(HEAD_DIM), vec(HEAD_DIM), vec(HEAD_DIM), vec(hw)],
        out_specs=pl.BlockSpec((None, rq, hw), lambda b, j: (b, 0, 0)),
        out_shape=jax.ShapeDtypeStruct((db, rq, hw), BF16),
        scratch_shapes=[pltpu.VMEM((2, rq, 1), F32), pltpu.VMEM((2, rq, 1), F32), pltpu.VMEM((2, rq, hw), F32)],
        compiler_params=_params("arbitrary", "arbitrary"),
        name="attn_sample",
    )(q, cache_k, cache_v, k_new, v_new, bias, bias_new, *lams, subln_w)


CONV_HIST = 32


def _conv_kernel(cur_ref, hist_ref, w_ref, b_ref, g_ref, beta_ref, o_ref, xs_ref, wb_ref, *, width, zero_first):
    nb, tt, c = cur_ref.shape
    rows = 8
    length = CONV_HIST + tt

    @pl.when(jnp.logical_and(pl.program_id(0) == 0, pl.program_id(1) == 0))
    def _():
        for tap in range(width):
            wb_ref[tap] = jnp.broadcast_to(w_ref[tap:tap + 1, :], (rows, c))

    hist = hist_ref[...]
    if zero_first:
        hist = jnp.where(pl.program_id(1) == 0, 0.0, hist)
    xs_ref[0, :, 0:CONV_HIST, :] = hist
    xs_ref[0, :, CONV_HIST:, :] = cur_ref[...]
    for s in range(1, rows):
        xs_ref[s, :, 0:length - rows, :] = xs_ref[0, :, s:s + length - rows, :]
    lead = CONV_HIST - (width - 1)
    for n in range(nb):
        for r0 in range(0, tt, rows):
            acc = jnp.zeros((rows, c), F32) + b_ref[...]
            for tap in range(width):
                s = (lead + tap) % rows
                a = lead + tap - s + r0
                acc = acc + xs_ref[s, n, a:a + rows, :] * wb_ref[tap]
            mu = jnp.mean(acc, axis=-1, keepdims=True)
            cen = acc - mu
            var = jnp.mean(cen * cen, axis=-1, keepdims=True)
            y = cen * lax.rsqrt(var + LN_EPS) * g_ref[...] + beta_ref[...]
            o_ref[n, r0:r0 + rows, :] = _silu(y).astype(o_ref.dtype)


def _conv(cur3, hist3, hist_index, nb, tt, w, b, g, beta, zero_first):
    nseq, seq, c = cur3.shape
    width = w.shape[0]
    assert width - 1 <= CONV_HIST and seq % tt == 0 and nseq % nb == 0 and tt % 8 == 0
    vec = lambda: pl.BlockSpec((1, c), lambda s, i: (0, 0))
    return pl.pallas_call(
        functools.partial(_conv_kernel, width=width, zero_first=zero_first),
        grid=(nseq // nb, seq // tt),
        in_specs=[pl.BlockSpec((nb, tt, c), lambda s, i: (s, i, 0)),
                  pl.BlockSpec((nb, CONV_HIST, c), hist_index),
                  pl.BlockSpec((width, c), lambda s, i: (0, 0)),
                  vec(), vec(), vec()],
        out_specs=pl.BlockSpec((nb, tt, c), lambda s, i: (s, i, 0)),
        out_shape=jax.ShapeDtypeStruct(cur3.shape, BF16),
        scratch_shapes=[pltpu.VMEM((8, nb, CONV_HIST + tt, c), F32), pltpu.VMEM((width, 8, c), F32)],
        compiler_params=_params("arbitrary", "arbitrary"),
        name="conv",
    )(cur3, hist3, w, b, g, beta)


def _first_argmax(v, iota, axis, size):
    m = jnp.max(v, axis=axis, keepdims=True)
    i = jnp.min(jnp.where(v == m, iota, size), axis=axis, keepdims=True)
    return m, i


def _router_kernel(h_ref, w_ref, bias_ref, idx_ref, gate_ref, rank_ref, cnt_ref, carry_ref):
    n_exp = w_ref.shape[0]
    tm = h_ref.shape[0]
    per_group = n_exp // N_EXPERT_GROUPS

    @pl.when(pl.program_id(0) == 0)
    def _():
        carry_ref[...] = jnp.zeros(carry_ref.shape, F32)

    w = w_ref[...]
    w1 = w.astype(BF16)
    r1 = w - w1.astype(F32)
    w2 = r1.astype(BF16)
    w3 = (r1 - w2.astype(F32)).astype(BF16)
    h = h_ref[...]
    logits = _dot_nt(w1, h) + _dot_nt(w2, h) + _dot_nt(w3, h)
    scores = jax.nn.sigmoid(logits)
    choice = scores + bias_ref[...]

    ch3 = choice.reshape(N_EXPERT_GROUPS, per_group, tm)
    io3 = lax.broadcasted_iota(I32, ch3.shape, 1)
    m1, i1 = _first_argmax(ch3, io3, 1, per_group)
    m2 = jnp.max(jnp.where(io3 == i1, -jnp.inf, ch3), axis=1, keepdims=True)
    grp = (m1 + m2).reshape(N_EXPERT_GROUPS, tm)
    gio = lax.broadcasted_iota(I32, grp.shape, 0)
    keep = jnp.zeros(grp.shape, jnp.bool_)
    for _ in range(TOPK_GROUPS):
        _, gi = _first_argmax(grp, gio, 0, N_EXPERT_GROUPS)
        hit = gio == gi
        keep = jnp.logical_or(keep, hit)
        grp = jnp.where(hit, -jnp.inf, grp)
    keep3 = jnp.broadcast_to(keep.reshape(N_EXPERT_GROUPS, 1, tm), ch3.shape)
    masked = jnp.where(keep3, ch3, -jnp.inf).reshape(n_exp, tm)

    eio = lax.broadcasted_iota(I32, masked.shape, 0)
    sel = jnp.zeros(masked.shape, jnp.bool_)
    picks, pick_scores = [], []
    for _ in range(TOP_K):
        _, ei = _first_argmax(masked, eio, 0, n_exp)
        hit = eio == ei
        picks.append((ei, hit))
        pick_scores.append(jnp.sum(jnp.where(hit, scores, 0.0), axis=0, keepdims=True))
        sel = jnp.logical_or(sel, hit)
        masked = jnp.where(hit, -jnp.inf, masked)
    total = pick_scores[0]
    for s in pick_scores[1:]:
        total = total + s

    sel_b = jnp.where(sel, 1.0, 0.0).astype(BF16)
    tri = (lax.broadcasted_iota(I32, (tm, tm), 0) < lax.broadcasted_iota(I32, (tm, tm), 1))
    rank = jnp.dot(sel_b, jnp.where(tri, 1.0, 0.0).astype(BF16), preferred_element_type=F32) + carry_ref[...]
    for kk, (ei, hit) in enumerate(picks):
        idx_ref[kk:kk + 1, :] = ei
        gate_ref[kk:kk + 1, :] = pick_scores[kk] / total * ROUTED_SCALE
        rank_ref[kk:kk + 1, :] = jnp.sum(jnp.where(hit, rank, 0.0), axis=0, keepdims=True).astype(I32)
    carry_ref[...] = carry_ref[...] + jnp.sum(jnp.where(sel, 1.0, 0.0), axis=1, keepdims=True)
    cnt_ref[...] = carry_ref[...].astype(I32)


def _router(h2b, w_router_t, bias_col):
    t, d = h2b.shape
    n_exp = w_router_t.shape[0]
    tm = math.gcd(512, t)
    assert t % tm == 0
    tok = lambda: pl.BlockSpec((TOP_K, tm), lambda i: (0, i))
    return pl.pallas_call(
        _router_kernel,
        grid=(t // tm,),
        in_specs=[pl.BlockSpec((tm, d), lambda i: (i, 0)),
                  pl.BlockSpec((n_exp, d), lambda i: (0, 0)),
                  pl.BlockSpec((n_exp, 1), lambda i: (0, 0))],
        out_specs=[tok(), tok(), tok(), pl.BlockSpec((n_exp, 1), lambda i: (0, 0))],
        out_shape=[jax.ShapeDtypeStruct((TOP_K, t), I32), jax.ShapeDtypeStruct((TOP_K, t), F32),
                   jax.ShapeDtypeStruct((TOP_K, t), I32), jax.ShapeDtypeStruct((n_exp, 1), I32)],
        scratch_shapes=[pltpu.VMEM((n_exp, 1), F32)],
        compiler_params=_params("arbitrary"),
        name="router",
    )(h2b, w_router_t, bias_col)


def _dispatch_kernel(rows_ref, h_ref, xs_ref, sem):
    n_tok = h_ref.shape[0]

    def body(t, carry):
        for kk in range(TOP_K):
            pltpu.make_async_copy(h_ref.at[pl.ds(t, 1)], xs_ref.at[pl.ds(rows_ref[kk, t], 1)], sem).start()
        return carry

    lax.fori_loop(0, n_tok, body, 0)
    for _ in range(TOP_K):
        pltpu.make_async_copy(h_ref, xs_ref.at[pl.ds(0, n_tok)], sem).wait()


def _dispatch(rows, h2p, n_rows):
    t, w = h2p.shape
    nt = math.gcd(DISPATCH_ROWS, t)
    assert t % nt == 0 and n_rows >= nt
    rows = jnp.transpose(rows.reshape(TOP_K, t // nt, nt), (1, 0, 2))
    return pl.pallas_call(
        _dispatch_kernel,
        grid=(t // nt,),
        in_specs=[pl.BlockSpec((None, TOP_K, nt), lambda i: (i, 0, 0), memory_space=pltpu.SMEM),
                  pl.BlockSpec((nt, w), lambda i: (i, 0))],
        out_specs=pl.BlockSpec(memory_space=pl.ANY),
        out_shape=jax.ShapeDtypeStruct((n_rows, w), h2p.dtype),
        scratch_shapes=[pltpu.SemaphoreType.DMA(())],
        compiler_params=pltpu.CompilerParams(dimension_semantics=("arbitrary",), has_side_effects=True),
        name="dispatch",
    )(rows, h2p)


def _seg_kernel(se_ref, sb0_ref, snb_ref, srows_ref, nseg_ref, xs_hbm, wg_hbm, wu_hbm, wd_hbm, ys_hbm,
                xbuf, wgf, wuf, wdf, wgb, wub, wdb, gacc, uacc, hid, ybuf, sem_x, sem_w, sem_y, ycnt):
    s = pl.program_id(0)
    nseg = nseg_ref[0]
    kc, phases = EXP_KCHUNKS, EXP_KCHUNKS + EXP_NCHUNKS
    r = EXP_ROWS
    dk, dn = wgf.shape[1], wdf.shape[2]
    hw = dk // 2

    def w_copies(e, ph):
        slot = ph & 1
        out = []
        if ph < kc:
            step = dk // W_DMA_PARTS
            for part in range(W_DMA_PARTS):
                dst = pl.ds(part * step, step)
                src = pl.ds(ph * dk + part * step, step)
                out.append(pltpu.make_async_copy(wg_hbm.at[e, src, :], wgf.at[slot, dst, :], sem_w.at[0, slot]))
                out.append(pltpu.make_async_copy(wu_hbm.at[e, src, :], wuf.at[slot, dst, :], sem_w.at[1, slot]))
            return out
        step = wdf.shape[1] // W_DMA_PARTS
        for part in range(W_DMA_PARTS):
            rows = pl.ds(part * step, step)
            out.append(pltpu.make_async_copy(wd_hbm.at[e, rows, pl.ds((ph - kc) * dn, dn)], wdf.at[slot, rows, :],
                                             sem_w.at[2, slot]))
        return out

    def for_x_blocks(seg, slot, action):
        for b in range(SEG_BLOCKS):
            @pl.when(b < snb_ref[seg])
            def _():
                row = pl.multiple_of((sb0_ref[seg] + b) * r, r)
                action(pltpu.make_async_copy(xs_hbm.at[pl.ds(row, r), :], xbuf.at[slot, pl.ds(b * r, r), :],
                                             sem_x.at[slot]))

    def y_copy(slot, row, chunk):
        return pltpu.make_async_copy(ybuf.at[slot], ys_hbm.at[pl.ds(row, r), pl.ds(chunk * (dn // 2), dn // 2)],
                                     sem_y.at[slot])

    @pl.when(s < nseg)
    def _():
        e, nb, par = se_ref[s], snb_ref[s], s & 1
        row_base = sb0_ref[s] * r
        valid = srows_ref[s]

        @pl.when(s == 0)
        def _():
            ycnt[0] = 0
            for_x_blocks(0, 0, lambda c: c.start())
            for ph in range(phases):
                for c in w_copies(e, ph):
                    c.start()

        @pl.when(s + 1 < nseg)
        def _():
            for_x_blocks(s + 1, 1 - par, lambda c: c.start())

        for_x_blocks(s, par, lambda c: c.wait())

        def refill(ph):
            @pl.when(s + 1 < nseg)
            def _():
                for c in w_copies(se_ref[s + 1], ph):
                    c.start()

        for ph in range(phases):
            for c in w_copies(e, ph):
                c.wait()
            slot = ph & 1
            if ph < kc:
                wgb[...] = wgf[slot].astype(BF16)
                wub[...] = wuf[slot].astype(BF16)
                refill(ph)

                def body(b, carry, ph=ph):
                    r0 = pl.multiple_of(b * r, r)
                    rows = r0 + lax.broadcasted_iota(I32, (r, 1), 0)
                    words = xbuf[par, pl.ds(r0, r), ph * hw:(ph + 1) * hw]
                    lo, hi = _unpack_halves(jnp.where(rows < valid, words, jnp.uint32(0)))
                    g = (jnp.dot(lo, wgb[:hw, :], preferred_element_type=F32)
                         + jnp.dot(hi, wgb[hw:, :], preferred_element_type=F32))
                    u = (jnp.dot(lo, wub[:hw, :], preferred_element_type=F32)
                         + jnp.dot(hi, wub[hw:, :], preferred_element_type=F32))
                    if ph > 0:
                        g = g + gacc[pl.ds(r0, r), :]
                        u = u + uacc[pl.ds(r0, r), :]
                    if ph < kc - 1:
                        gacc[pl.ds(r0, r), :] = g
                        uacc[pl.ds(r0, r), :] = u
                    else:
                        hid[pl.ds(r0, r), :] = (_silu(g) * u).astype(BF16)
                    return carry
            else:
                wdb[...] = wdf[slot].astype(BF16)
                refill(ph)

                def body(b, carry, ph=ph):
                    r0 = pl.multiple_of(b * r, r)
                    n = ycnt[0]
                    ys = n & 1

                    @pl.when(n >= 2)
                    def _():
                        y_copy(ys, 0, 0).wait()

                    ybuf[ys] = _pack_halves(jnp.dot(hid[pl.ds(r0, r), :], wdb[...], preferred_element_type=F32))
                    y_copy(ys, pl.multiple_of(row_base + r0, r), ph - kc).start()
                    ycnt[0] = n + 1
                    return carry

            lax.fori_loop(0, nb, body, 0)

        @pl.when(s == nseg - 1)
        def _():
            n = ycnt[0]
            for back in (1, 2):
                @pl.when(n >= back)
                def _():
                    y_copy((n - back) & 1, 0, 0).wait()


def _expert_segments(counts, n_blocks_max):
    n_exp = counts.shape[0]
    span = SEG_BLOCKS * EXP_ROWS
    nb_e = (counts + EXP_ROWS - 1) // EXP_ROWS
    blk_start_e = jnp.cumsum(nb_e) - nb_e
    nseg_e = (nb_e + SEG_BLOCKS - 1) // SEG_BLOCKS
    seg_end_e = jnp.cumsum(nseg_e)
    n_seg_max = n_exp + n_blocks_max // SEG_BLOCKS
    sid = jnp.arange(n_seg_max, dtype=I32)
    seg_e = jnp.minimum(jnp.sum(seg_end_e[None, :] <= sid[:, None], axis=1), n_exp - 1).astype(I32)
    mine = seg_e[:, None] == jnp.arange(n_exp, dtype=I32)[None, :]
    of_expert = lambda v: jnp.sum(jnp.where(mine, v[None, :], 0), axis=1)
    part = sid - of_expert(seg_end_e - nseg_e)
    seg_nb = jnp.clip(of_expert(nb_e) - part * SEG_BLOCKS, 0, SEG_BLOCKS)
    seg_blk0 = of_expert(blk_start_e) + part * SEG_BLOCKS
    seg_rows = jnp.clip(of_expert(counts) - part * span, 0, span)
    to_i32 = lambda v: v.astype(I32)
    return (seg_e, to_i32(seg_blk0), to_i32(seg_nb), to_i32(seg_rows), to_i32(seg_end_e[-1:]),
            to_i32(blk_start_e * EXP_ROWS))


def _experts_by_segment(seg_tables, xs, w_gate, w_up, w_down):
    n_exp, d, ff = w_gate.shape
    kc, nc = EXP_KCHUNKS, EXP_NCHUNKS
    dk, dn = d // kc, d // nc
    span = SEG_BLOCKS * EXP_ROWS
    assert kc <= 2 and nc <= 2
    hbm = lambda: pl.BlockSpec(memory_space=pl.ANY)
    grid_spec = pltpu.PrefetchScalarGridSpec(
        num_scalar_prefetch=5,
        grid=(seg_tables[0].shape[0],),
        in_specs=[hbm(), hbm(), hbm(), hbm()],
        out_specs=hbm(),
        scratch_shapes=[pltpu.VMEM((2, span, d // 2), U32),
                        pltpu.VMEM((2, dk, ff), F32), pltpu.VMEM((2, dk, ff), F32), pltpu.VMEM((2, ff, dn), F32),
                        pltpu.VMEM((dk, ff), BF16), pltpu.VMEM((dk, ff), BF16), pltpu.VMEM((ff, dn), BF16),
                        pltpu.VMEM((span, ff), F32), pltpu.VMEM((span, ff), F32), pltpu.VMEM((span, ff), BF16),
                        pltpu.VMEM((2, EXP_ROWS, dn // 2), U32),
                        pltpu.SemaphoreType.DMA((2,)), pltpu.SemaphoreType.DMA((3, 2)),
                        pltpu.SemaphoreType.DMA((2,)), pltpu.SMEM((1,), I32)],
    )
    return pl.pallas_call(
        _seg_kernel,
        grid_spec=grid_spec,
        out_shape=jax.ShapeDtypeStruct((xs.shape[0], d // 2), U32),
        compiler_params=_params("arbitrary"),
        name="experts",
    )(*seg_tables, xs, w_gate, w_up, w_down)


def _combine_kernel(rows_ref, next_rows_ref, gate_ref, hs_ref, wsd_ref, ys_ref, f_ref, buf_ref, wb_ref, sem):
    i = pl.program_id(0)
    n_tok = gate_ref.shape[0]
    cur = i & 1

    def gather(table_ref, slot, action):
        def body(t, carry):
            for kk in range(TOP_K):
                action(pltpu.make_async_copy(ys_ref.at[pl.ds(table_ref[kk, t], 1)],
                                             buf_ref.at[slot, kk, pl.ds(t, 1)], sem.at[slot]))
            return carry
        lax.fori_loop(0, n_tok, body, 0)

    @pl.when(i == 0)
    def _():
        wb_ref[...] = wsd_ref[...].astype(BF16)
        gather(rows_ref, 0, lambda c: c.start())

    @pl.when(i + 1 < pl.num_programs(0))
    def _():
        gather(next_rows_ref, 1 - cur, lambda c: c.start())

    shared = jnp.dot(hs_ref[...], wb_ref[...], preferred_element_type=F32)
    for kk in range(TOP_K):
        pltpu.make_async_copy(ys_ref.at[pl.ds(0, n_tok)], buf_ref.at[cur, kk], sem.at[cur]).wait()
    gates = gate_ref[...]
    dn = f_ref.shape[1] // EXP_NCHUNKS
    hw = dn // 2
    for c in range(EXP_NCHUNKS):
        lo = shared[:, c * dn:c * dn + hw]
        hi = shared[:, c * dn + hw:(c + 1) * dn]
        for kk in range(TOP_K):
            words = buf_ref[cur, kk, :, c * hw:(c + 1) * hw]
            g = gates[:, kk:kk + 1]
            lo = lo + lax.bitcast_convert_type(words << 16, F32) * g
            hi = hi + lax.bitcast_convert_type(words & jnp.uint32(0xFFFF0000), F32) * g
        f_ref[:, c * dn:c * dn + hw] = lo
        f_ref[:, c * dn + hw:(c + 1) * dn] = hi


def _combine(rows, gates_t, hid_sh, w_sh_down, ys):
    t, ff = hid_sh.shape
    d = w_sh_down.shape[1]
    nt = math.gcd(COMBINE_ROWS, t)
    n_steps = t // nt
    rows = jnp.transpose(rows.reshape(TOP_K, n_steps, nt), (1, 0, 2))
    rows_spec = lambda ahead: pl.BlockSpec((None, TOP_K, nt), lambda i: (jnp.minimum(i + ahead, n_steps - 1), 0, 0),
                                           memory_space=pltpu.SMEM)
    return pl.pallas_call(
        _combine_kernel,
        grid=(n_steps,),
        in_specs=[rows_spec(0), rows_spec(1),
                  pl.BlockSpec((nt, TOP_K), lambda i: (i, 0)),
                  pl.BlockSpec((nt, ff), lambda i: (i, 0)),
                  pl.BlockSpec((ff, d), lambda i: (0, 0)),
                  pl.BlockSpec(memory_space=pl.ANY)],
        out_specs=pl.BlockSpec((nt, d), lambda i: (i, 0)),
        out_shape=jax.ShapeDtypeStruct((t, d), F32),
        scratch_shapes=[pltpu.VMEM((2, TOP_K, nt, d // 2), U32), pltpu.VMEM((ff, d), BF16),
                        pltpu.SemaphoreType.DMA((2,))],
        compiler_params=_params("arbitrary"),
        name="combine",
    )(rows, rows, gates_t, hid_sh, w_sh_down, ys)


def _layer(l, lam_init, x_prompt, x_sample, c_prompt, c_sample, cache_k, cache_v, state_conv, rel_bias, p):
    batch, seq, d = x_prompt.shape
    db, dseq, _ = x_sample.shape
    past, heads = cache_k.shape[1], cache_k.shape[2]
    hw = 2 * HEAD_DIM
    aw = heads * hw
    cc = p["conv_dw_w"].shape[1]
    width = p["conv_dw_w"].shape[0]
    assert p["w_in"].shape[1] == 3 * aw + 2 * cc and seq % dseq == 0
    tp, ts = batch * seq, db * dseq
    row = lambda v: v.reshape(1, -1)

    n_mod_rows = -(-(db + batch) // 16) * 16
    c_all = jnp.concatenate([c_sample, c_prompt, jnp.zeros((n_mod_rows - db - batch, d), F32)], axis=0)
    mod3 = _ada(c_all, p["w_ada"], row(p["b_ada"])).reshape(n_mod_rows, 1, N_MOD * d)

    gp = tp // dseq
    gps = seq // dseq
    nb_p = math.gcd(ROW_GROUPS, gps)
    nb_s = math.gcd(ROW_GROUPS, db)
    assert gps % nb_p == 0 and db % nb_s == 0 and gp % nb_s == 0
    plan_p = _RowPlan(gp, dseq, d, nb_p, 1, lambda i: db + (i * nb_p) // gps, 0)
    plan_s = _RowPlan(db, dseq, d, nb_s, nb_s, lambda i: i, gp // nb_s)
    xp3 = x_prompt.reshape(gp, dseq, d)

    hp = _prenorm(plan_p, xp3, row(p["g_pre_mix"]), mod3).reshape(tp, d)
    hs = _prenorm(plan_s, x_sample, row(p["g_pre_mix"]), mod3).reshape(ts, d)

    w_in = p["w_in"]
    proj = lambda h, off, nm: _mm([h], [(w_in, off)], aw, _identity, F32, MM_ROWS, 512, nm)
    qp, kp, vp = proj(hp, 0, "q_prompt"), proj(hp, aw, "k_prompt"), proj(hp, 2 * aw, "v_prompt")
    qs, ks, vs = proj(hs, 0, "q_sample"), proj(hs, aw, "k_sample"), proj(hs, 2 * aw, "v_sample")
    glu_cols = [(w_in, 3 * aw), (w_in, 3 * aw + cc)]
    glu_p = _mm([hp], glu_cols, cc, _glu, F32, MM_ROWS, 256, "glu_prompt")
    glu_s = _mm([hs], glu_cols, cc, _glu, F32, MM_ROWS, 256, "glu_sample")

    lams = [row(p[n]) for n in ("lambda_q1", "lambda_k1", "lambda_q2", "lambda_k2")]
    subln = row(p["subln_w"])
    attn_p = _attn_prompt(qp, kp, vp, rel_bias, lams, subln, batch, seq, heads, lam_init)

    new_k_s = ks.reshape(db, dseq, heads, hw)
    new_v_s = vs.reshape(db, dseq, heads, hw)
    q_s = jnp.transpose(qs.reshape(db, dseq, heads, 2, HEAD_DIM), (0, 3, 2, 1, 4)).reshape(db, 2, heads * dseq, HEAD_DIM)
    o_s = _attn_sample(q_s, cache_k, cache_v, new_k_s.reshape(db, dseq * heads, hw),
                       new_v_s.reshape(db, dseq * heads, hw), rel_bias, lams, subln, lam_init)
    attn_s = jnp.transpose(o_s.reshape(db, heads, dseq, hw), (0, 2, 1, 3)).reshape(ts, aw)

    conv_args = (p["conv_dw_w"], row(p["conv_dw_b"]), row(p["conv_ln_g"]), row(p["conv_ln_b"]))
    glu_p3 = glu_p.reshape(batch, seq, cc)
    tt = min(128, seq)
    per = tt // CONV_HIST
    conv_p = _conv(glu_p3, glu_p3, lambda s, i: (s, jnp.maximum(i * per - 1, 0), 0), 1, tt, *conv_args,
                   zero_first=True).reshape(tp, cc)
    glu_s3 = glu_s.reshape(db, dseq, cc)
    hist_s = jnp.concatenate([jnp.zeros((db, CONV_HIST - (width - 1), cc), F32), state_conv], axis=1)
    nb_c = math.gcd(2, db)
    conv_s = _conv(glu_s3, hist_s, lambda s, i: (s, 0, 0), nb_c, dseq, *conv_args, zero_first=False).reshape(ts, cc)
    new_conv_p = glu_p3[:, seq - (width - 1):]
    new_conv_s = jnp.concatenate([state_conv, glu_s3], axis=1)[:, -(width - 1):]

    mix_p = _mm([attn_p, conv_p], [(p["w_out"], 0)], d, _identity, F32, MM_ROWS, 512, "out_prompt")
    mix_s = _mm([attn_s, conv_s], [(p["w_out"], 0)], d, _identity, F32, MM_ROWS, 512, "out_sample")

    gt = gp + db
    gpost, gpre = row(p["g_post_mix"]), row(p["g_pre_ffn"])
    x1p, h2ba, h2pa = _postmix(plan_p, xp3, mix_p.reshape(gp, dseq, d), gpost, gpre, mod3, gt)
    x1s, h2ba, h2pa = _postmix(plan_s, x_sample, mix_s.reshape(db, dseq, d), gpost, gpre, mod3, gt,
                               prev=(h2ba, h2pa))
    t = tp + ts
    h2b = h2ba.reshape(t, d)
    h2p = h2pa.reshape(t, d // 2)

    n_exp = p["w_router"].shape[1]
    idx, gates, rank, counts = _router(h2b, p["w_router"].T, p["b_router_corr"].reshape(n_exp, 1))
    n_blocks_max = (t * TOP_K) // EXP_ROWS + n_exp
    *seg_tables, row_start = _expert_segments(counts.reshape(n_exp), n_blocks_max)
    pick = idx[..., None] == jnp.arange(n_exp, dtype=I32)
    rows = jnp.sum(jnp.where(pick, row_start, 0), axis=-1) + rank

    xs = _dispatch(rows.astype(I32), h2p, n_blocks_max * EXP_ROWS)
    ys = _experts_by_segment(seg_tables, xs, p["w_exp_gate"], p["w_exp_up"], p["w_exp_down"])
    ff_sh = p["w_sh_gate"].shape[1]
    hid_sh = _mm([h2b], [(p["w_sh_gate"], 0), (p["w_sh_up"], 0)], ff_sh, _swiglu, BF16, MM_ROWS, 256, "shared_up")
    f = _combine(rows.astype(I32), gates.T, hid_sh, p["w_sh_down"], ys)

    f3 = f.reshape(gt, dseq, d)
    gpf = row(p["g_post_ffn"])
    yp = _final(plan_p, x1p, f3, gpf, mod3).reshape(batch, seq, d)
    ysmp = _final(plan_s, x1s, f3, gpf, mod3)
    new_k_p = kp.reshape(batch, seq, heads, hw)
    new_v_p = vp.reshape(batch, seq, heads, hw)
    return yp, ysmp, new_k_p, new_v_p, new_conv_p, new_k_s, new_v_s, new_conv_s


def kernel(x_prompt, x_sample, c_prompt, c_sample, cache_k, cache_v, state_conv, rel_bias, w_ada, b_ada, g_pre_mix, g_post_mix, g_pre_ffn, g_post_ffn, w_in, lambda_q1, lambda_k1, lambda_q2, lambda_k2, subln_w, conv_dw_w, conv_dw_b, conv_ln_g, conv_ln_b, w_out, w_router, b_router_corr, w_exp_gate, w_exp_up, w_exp_down, w_sh_gate, w_sh_up, w_sh_down):
    weights = dict(w_ada=w_ada, b_ada=b_ada, g_pre_mix=g_pre_mix, g_post_mix=g_post_mix, g_pre_ffn=g_pre_ffn,
                   g_post_ffn=g_post_ffn, w_in=w_in, lambda_q1=lambda_q1, lambda_k1=lambda_k1,
                   lambda_q2=lambda_q2, lambda_k2=lambda_k2, subln_w=subln_w, conv_dw_w=conv_dw_w,
                   conv_dw_b=conv_dw_b, conv_ln_g=conv_ln_g, conv_ln_b=conv_ln_b, w_out=w_out,
                   w_router=w_router, b_router_corr=b_router_corr, w_exp_gate=w_exp_gate, w_exp_up=w_exp_up,
                   w_exp_down=w_exp_down, w_sh_gate=w_sh_gate, w_sh_up=w_sh_up, w_sh_down=w_sh_down)
    depth = w_in.shape[0]
    xp, xs = x_prompt, x_sample
    outs = [[] for _ in range(6)]
    for l in range(depth):
        p = {k: (v.reshape(v.shape[1:]) if depth == 1 else v[l]) for k, v in weights.items()}
        lam_init = 0.8 - 0.6 * math.exp(-0.3 * l)
        ck, cv, sc = ((a.reshape(a.shape[1:]) if depth == 1 else a[l]) for a in (cache_k, cache_v, state_conv))
        xp, xs, *state = _layer(l, lam_init, xp, xs, c_prompt, c_sample, ck, cv, sc, rel_bias, p)
        for acc, s in zip(outs, state):
            acc.append(s)
    return (xp, xs) + tuple(jnp.stack(o) for o in outs)
```

```python
import functools
import math

import numpy as np
import jax
import jax.numpy as jnp
from jax import lax
from jax.experimental import pallas as pl
from jax.experimental.pallas import tpu as pltpu

F32 = jnp.float32
BF16 = jnp.bfloat16
I32 = jnp.int32
U32 = jnp.uint32

CHUNK = 64
HEAD_DIM = 128
NUM_BUCKETS = 32
REL_MAX_DISTANCE = 128
TOP_K = 8
N_EXPERT_GROUPS = 8
TOPK_GROUPS = 4
ROUTED_SCALE = 2.5
RMS_EPS = 1e-6
SUBLN_EPS = 1e-5
LN_EPS = 1e-5
N_MOD = 6
NEG = -1e30
ATT_SCALE = HEAD_DIM ** -0.5

VMEM_LIMIT_BYTES = 56 * 1024 * 1024
ATT_TILE = 256
EXP_ROWS = 128
SEG_BLOCKS = 4
W_DMA_PARTS = 4
EXP_KCHUNKS = 2
EXP_NCHUNKS = 2
COMBINE_ROWS = 128
DISPATCH_ROWS = 256
ROW_GROUPS = 4
MM_ROWS = 1024


def _params(*sem):
    return pltpu.CompilerParams(dimension_semantics=sem, vmem_limit_bytes=VMEM_LIMIT_BYTES)


def _silu(x):
    return x * jax.nn.sigmoid(x)


def _dot_nt(a, b):
    return lax.dot_general(a, b, (((1,), (1,)), ((), ())), preferred_element_type=F32)


def _ada_kernel(c_ref, w_ref, b_ref, o_ref):
    a = _silu(c_ref[...]).astype(BF16)
    o_ref[...] = jnp.dot(a, w_ref[...].astype(BF16), preferred_element_type=F32) + b_ref[...]


def _ada(c_all, w_ada, b_ada):
    rows, d = c_all.shape
    n = w_ada.shape[1]
    tn = min(512, n)
    return pl.pallas_call(
        _ada_kernel,
        grid=(n // tn,),
        in_specs=[pl.BlockSpec((rows, d), lambda j: (0, 0)),
                  pl.BlockSpec((d, tn), lambda j: (0, j)),
                  pl.BlockSpec((1, tn), lambda j: (0, j))],
        out_specs=pl.BlockSpec((rows, tn), lambda j: (0, j)),
        out_shape=jax.ShapeDtypeStruct((rows, n), F32),
        compiler_params=_params("arbitrary"),
        name="ada",
    )(c_all, w_ada, b_ada)


class _RowPlan:
    def __init__(self, n_groups, group_rows, d, nb, mod_nb, mod_index, out_block_offset=0):
        self.n_groups, self.group_rows, self.d, self.nb = n_groups, group_rows, d, nb
        self.mod_nb, self.mod_index, self.out_block_offset = mod_nb, mod_index, out_block_offset
        self.grid = (n_groups // nb,)

    def act(self, offset=0):
        return pl.BlockSpec((self.nb, self.group_rows, self.d), lambda i: (i + offset, 0, 0))

    def mod(self, chunk):
        return pl.BlockSpec((self.mod_nb, 1, self.d), lambda i: (self.mod_index(i), 0, chunk))

    def vec(self):
        return pl.BlockSpec((1, self.d), lambda i: (0, 0))


def _rms(x, g, eps):
    return x * lax.rsqrt(jnp.mean(x * x, axis=-1, keepdims=True) + eps) * g


def _prenorm_kernel(x_ref, g_ref, sh_ref, sc_ref, o_ref):
    y = _rms(x_ref[...], g_ref[...], RMS_EPS)
    o_ref[...] = (y * (1.0 + sc_ref[...]) + sh_ref[...]).astype(o_ref.dtype)


def _prenorm(plan, x3, g, mod3):
    return pl.pallas_call(
        _prenorm_kernel,
        grid=plan.grid,
        in_specs=[plan.act(), plan.vec(), plan.mod(0), plan.mod(1)],
        out_specs=plan.act(),
        out_shape=jax.ShapeDtypeStruct(x3.shape, BF16),
        compiler_params=_params("arbitrary"),
        name="prenorm",
    )(x3, g, mod3, mod3)


def _pack_halves(x):
    n = x.shape[-1] // 2
    lo = lax.bitcast_convert_type(x[..., :n].astype(BF16).astype(F32), U32) >> 16
    hi = lax.bitcast_convert_type(x[..., n:].astype(BF16).astype(F32), U32) & jnp.uint32(0xFFFF0000)
    return hi | lo


def _unpack_halves(w):
    lo = lax.bitcast_convert_type(w << 16, F32).astype(BF16)
    hi = lax.bitcast_convert_type(w & jnp.uint32(0xFFFF0000), F32).astype(BF16)
    return lo, hi


def _postmix_kernel(x_ref, mix_ref, gpost_ref, gpre_ref, g1_ref, sh2_ref, sc2_ref, *rest):
    x1_ref, h2b_ref, h2p_ref = rest[-3:]
    x1 = x_ref[...] + g1_ref[...] * _rms(mix_ref[...], gpost_ref[...], RMS_EPS)
    x1_ref[...] = x1
    h2 = _rms(x1, gpre_ref[...], RMS_EPS) * (1.0 + sc2_ref[...]) + sh2_ref[...]
    h2b_ref[...] = h2.astype(BF16)
    dk = h2.shape[-1] // EXP_KCHUNKS
    for c in range(EXP_KCHUNKS):
        h2p_ref[:, :, c * (dk // 2):(c + 1) * (dk // 2)] = _pack_halves(h2[:, :, c * dk:(c + 1) * dk])


def _postmix(plan, x3, mix3, gpost, gpre, mod3, total_groups, prev=None):
    off = plan.out_block_offset
    in_specs = [plan.act(), plan.act(), plan.vec(), plan.vec(), plan.mod(2), plan.mod(3), plan.mod(4)]
    args = [x3, mix3, gpost, gpre, mod3, mod3, mod3]
    aliases = {}
    if prev is not None:
        in_specs += [pl.BlockSpec(memory_space=pl.ANY)] * 3
        args += list(prev)
        aliases = {7: 0, 8: 1, 9: 2}
    shape_all = (total_groups, plan.group_rows, plan.d)
    shape_packed = (total_groups, plan.group_rows, plan.d // 2)
    packed_spec = pl.BlockSpec((plan.nb, plan.group_rows, plan.d // 2), lambda i: (i + off, 0, 0))
    return pl.pallas_call(
        _postmix_kernel,
        grid=plan.grid,
        in_specs=in_specs,
        out_specs=[plan.act(off), plan.act(off), packed_spec],
        out_shape=[jax.ShapeDtypeStruct(shape_all, F32),
                   jax.ShapeDtypeStruct(shape_all, BF16),
                   jax.ShapeDtypeStruct(shape_packed, U32)],
        input_output_aliases=aliases,
        compiler_params=_params("arbitrary"),
        name="postmix",
    )(*args)


def _mm_kernel(*refs, n_a, n_w, k_sizes, epilogue):
    a_refs = refs[:n_a]
    w_refs = refs[n_a:n_a + n_w]
    o_ref = refs[n_a + n_w]
    wb_refs = refs[n_a + n_w + 1:]

    @pl.when(pl.program_id(1) == 0)
    def _():
        for w_ref, wb_ref in zip(w_refs, wb_refs):
            wb_ref[...] = w_ref[...].astype(BF16)

    parts = []
    for wb_ref in wb_refs:
        acc, k0 = None, 0
        for a_ref, ka in zip(a_refs, k_sizes):
            d = jnp.dot(a_ref[...], wb_ref[k0:k0 + ka, :], preferred_element_type=F32)
            acc = d if acc is None else acc + d
            k0 += ka
        parts.append(acc)
    o_ref[...] = epilogue(*parts).astype(o_ref.dtype)


def _mm(a_list, w_cols, n_out, epilogue, out_dtype, tm, tn, name):
    m = a_list[0].shape[0]
    k = w_cols[0][0].shape[0]
    tm, tn = math.gcd(tm, m), math.gcd(tn, n_out)
    k_sizes = tuple(a.shape[1] for a in a_list)
    assert sum(k_sizes) == k and m % tm == 0 and n_out % tn == 0 and all(c % tn == 0 for _, c in w_cols)
    a_specs = [pl.BlockSpec((tm, ka), lambda j, i: (i, 0)) for ka in k_sizes]
    w_specs = [pl.BlockSpec((k, tn), functools.partial(lambda j, i, o: (0, o + j), o=c // tn)) for _, c in w_cols]
    kern = functools.partial(_mm_kernel, n_a=len(a_list), n_w=len(w_cols), k_sizes=k_sizes, epilogue=epilogue)
    return pl.pallas_call(
        kern,
        grid=(n_out // tn, m // tm),
        in_specs=a_specs + w_specs,
        out_specs=pl.BlockSpec((tm, tn), lambda j, i: (i, j)),
        out_shape=jax.ShapeDtypeStruct((m, n_out), out_dtype),
        scratch_shapes=[pltpu.VMEM((k, tn), BF16) for _ in w_cols],
        compiler_params=_params("arbitrary", "arbitrary"),
        name=name,
    )(*a_list, *[w for w, _ in w_cols])


def _identity(x):
    return x


def _glu(a, b):
    return a * jax.nn.sigmoid(b)


def _swiglu(a, b):
    return _silu(a) * b


def _bucket(rel):
    half = NUM_BUCKETS // 2
    max_exact = half // 2
    n = np.abs(rel)
    nf = np.maximum(n, 1).astype(np.float32)
    large = max_exact + (np.log(nf / np.float32(max_exact)) / np.float32(math.log(REL_MAX_DISTANCE / max_exact))
                         * np.float32(half - max_exact)).astype(np.int32)
    large = np.minimum(large, half - 1)
    return np.where(rel > 0, half, 0) + np.where(n < max_exact, n, large)


def _bias_table(rel_bias, q_pos, k_pos):
    rel = k_pos[None, :] - q_pos[:, None]
    visible = (k_pos // CHUNK)[None, :] <= (q_pos // CHUNK)[:, None]
    onehot = (jnp.asarray(_bucket(rel))[..., None] == jnp.arange(NUM_BUCKETS)).astype(F32)
    b = jnp.einsum("qkb,bhm->hmqk", onehot, rel_bias.astype(F32), precision=lax.Precision.HIGHEST)
    return jnp.where(jnp.asarray(visible)[None, None], b, NEG)


def _lambda(lq1, lk1, lq2, lk2, lam_init):
    return (jnp.exp(jnp.sum(lq1[...] * lk1[...], keepdims=True))
            - jnp.exp(jnp.sum(lq2[...] * lk2[...], keepdims=True)) + lam_init)


def _softmax_update(state, s, v):
    m_old, l_old, acc_old = state
    m_new = jnp.maximum(m_old, jnp.max(s, axis=-1, keepdims=True))
    alpha = jnp.exp(m_old - m_new)
    p = jnp.exp(s - m_new)
    return (m_new, alpha * l_old + jnp.sum(p, axis=-1, keepdims=True),
            alpha * acc_old + jnp.dot(p.astype(BF16), v, preferred_element_type=F32))


def _attn_finish(states, lam, sw, lam_init, o_ref):
    (_, l1, acc1), (_, l2, acc2) = states
    o = acc1 / l1 - lam * (acc2 / l2)
    o = o * lax.rsqrt(jnp.mean(o * o, axis=-1, keepdims=True) + SUBLN_EPS) * sw * (1.0 - lam_init)
    o_ref[...] = o.astype(o_ref.dtype)


def _attn_prompt_kernel(q_ref, k_ref, v_ref, b_ref, lq1, lk1, lq2, lk2, sw_ref, o_ref, kb_ref, vb_ref,
                        *, lam_init, nq):
    t = ATT_TILE
    kb_ref[...] = k_ref[...].astype(BF16)
    vb_ref[...] = v_ref[...].astype(BF16)
    lam = _lambda(lq1, lk1, lq2, lk2, lam_init)
    sw = sw_ref[...]
    for qi in range(nq):
        q = q_ref[qi * t:(qi + 1) * t, :]
        n_far = max(qi - 1, 0) * t
        exps, sums = [], []
        for mp in range(2):
            cols = slice(mp * HEAD_DIM, (mp + 1) * HEAD_DIM)
            s = _dot_nt(q[:, cols].astype(BF16), kb_ref[0:(qi + 1) * t, cols]) * ATT_SCALE
            pieces = []
            if n_far:
                pieces.append(s[:, :n_far] + b_ref[mp, 2, 0:1, 0:1])
            if qi >= 1:
                pieces.append(s[:, n_far:n_far + t] + b_ref[mp, 1])
            pieces.append(s[:, qi * t:(qi + 1) * t] + b_ref[mp, 0])
            m = functools.reduce(jnp.maximum, [jnp.max(p, axis=-1, keepdims=True) for p in pieces])
            es = [jnp.exp(p - m) for p in pieces]
            exps.append(es)
            sums.append(functools.reduce(jnp.add, [jnp.sum(e, axis=-1, keepdims=True) for e in es]))
        c1 = 1.0 / sums[0]
        c2 = lam / sums[1]
        o, col = None, 0
        for e1, e2 in zip(exps[0], exps[1]):
            a = (e1 * c1 - e2 * c2).astype(BF16)
            d = jnp.dot(a, vb_ref[col:col + a.shape[1], :], preferred_element_type=F32)
            o = d if o is None else o + d
            col += a.shape[1]
        o = o * lax.rsqrt(jnp.mean(o * o, axis=-1, keepdims=True) + SUBLN_EPS) * sw * (1.0 - lam_init)
        o_ref[qi * t:(qi + 1) * t, :] = o.astype(o_ref.dtype)


def _attn_prompt(q, k, v, rel_bias, lams, subln_w, batch, seq, heads, lam_init):
    t = ATT_TILE
    assert seq % t == 0 and t % CHUNK == 0
    pos = np.arange(t)
    tiles = [_bias_table(rel_bias, pos + dt * t, pos) for dt in range(3)]
    assert np.all(_bucket(np.arange(-3 * t + 1, -t)) == NUM_BUCKETS // 2 - 1)
    bias = jnp.stack(tiles, axis=2)
    hw = 2 * HEAD_DIM
    vec = lambda n: pl.BlockSpec((1, n), lambda b, h: (0, 0))
    seq_spec = lambda: pl.BlockSpec((seq, hw), lambda b, h: (b, h))
    return pl.pallas_call(
        functools.partial(_attn_prompt_kernel, lam_init=lam_init, nq=seq // t),
        grid=(batch, heads),
        in_specs=[seq_spec(), seq_spec(), seq_spec(),
                  pl.BlockSpec((None, 2, 3, t, t), lambda b, h: (h, 0, 0, 0, 0)),
                  vec(HEAD_DIM), vec(HEAD_DIM), vec(HEAD_DIM), vec(HEAD_DIM), vec(hw)],
        out_specs=seq_spec(),
        out_shape=jax.ShapeDtypeStruct((batch * seq, heads * hw), BF16),
        scratch_shapes=[pltpu.VMEM((seq, hw), BF16), pltpu.VMEM((seq, hw), BF16)],
        compiler_params=_params("arbitrary", "arbitrary"),
        name="attn_prompt",
    )(q, k, v, bias, *lams, subln_w)


def _attn_sample_kernel(q_ref, ck_ref, cv_ref, kn_ref, vn_ref, b_ref, bn_ref, lq1, lk1, lq2, lk2, sw_ref, o_ref,
                        m_ref, l_ref, acc_ref, *, lam_init, n_tiles):
    kt_i = pl.program_id(1)

    @pl.when(kt_i == 0)
    def _():
        m_ref[...] = jnp.full(m_ref.shape, NEG, F32)
        l_ref[...] = jnp.zeros(l_ref.shape, F32)
        acc_ref[...] = jnp.zeros(acc_ref.shape, F32)

    def step(states, k2d, v2d, bias_of_map):
        kb = k2d.astype(BF16)
        vb = v2d.astype(BF16)
        out = []
        for mp in range(2):
            qm = q_ref[mp].astype(BF16)
            s = _dot_nt(qm, kb[:, mp * HEAD_DIM:(mp + 1) * HEAD_DIM]) * ATT_SCALE + bias_of_map(mp)
            out.append(_softmax_update(states[mp], s, vb))
        return out

    states = [(m_ref[mp], l_ref[mp], acc_ref[mp]) for mp in range(2)]
    t = b_ref.shape[3] // ck_ref.shape[1]
    n_sub = ck_ref.shape[0] // t
    for sub in range(n_sub):
        keys = slice(sub * t, (sub + 1) * t)
        bi = jnp.where(kt_i == n_tiles - 1, 1, 0) if sub == n_sub - 1 else 0
        states = step(states, ck_ref[keys].reshape(t * ck_ref.shape[1], ck_ref.shape[2]),
                      cv_ref[keys].reshape(t * cv_ref.shape[1], cv_ref.shape[2]), lambda mp, bi=bi: b_ref[bi, mp])

    @pl.when(kt_i < n_tiles - 1)
    def _():
        for mp in range(2):
            m_ref[mp], l_ref[mp], acc_ref[mp] = states[mp]

    @pl.when(kt_i == n_tiles - 1)
    def _():
        final = step(states, kn_ref[...], vn_ref[...], lambda mp: bn_ref[mp])
        _attn_finish(final, _lambda(lq1, lk1, lq2, lk2, lam_init), sw_ref[...], lam_init, o_ref)


def _head_expand(bias, heads):
    h, _, tq, tk = bias.shape
    eye = jnp.asarray(np.eye(heads, dtype=bool))
    full = jnp.where(eye[:, None, None, None, :], bias[..., None], NEG)
    return jnp.transpose(full, (1, 0, 2, 3, 4)).reshape(2, h * tq, tk * heads)


def _attn_sample(q, cache_k, cache_v, k_new, v_new, rel_bias, lams, subln_w, lam_init):
    db, past, heads, hw = cache_k.shape
    tq = q.shape[2] // heads
    t = min(ATT_TILE, past)
    assert past % t == 0 and t >= REL_MAX_DISTANCE + tq
    n_tiles = past // t
    q_pos = past + np.arange(tq)
    far = _bias_table(rel_bias, q_pos, np.arange(t))
    assert n_tiles == 1 or np.all(_bucket(np.arange(past - t)[None, :] - q_pos[:, None]) == NUM_BUCKETS // 2 - 1)
    near = _bias_table(rel_bias, q_pos, past - t + np.arange(t))
    bias = jnp.stack([_head_expand(far, heads), _head_expand(near, heads)])
    bias_new = _head_expand(_bias_table(rel_bias, q_pos, q_pos), heads)
    rq = heads * tq
    n_sub = 2 if n_tiles % 2 == 0 else 1
    tb = t * n_sub
    vec = lambda n: pl.BlockSpec((1, n), lambda b, j: (0, 0))
    return pl.pallas_call(
        functools.partial(_attn_sample_kernel, lam_init=lam_init, n_tiles=n_tiles // n_sub),
        grid=(db, n_tiles // n_sub),
        in_specs=[pl.BlockSpec((None, 2, rq, HEAD_DIM), lambda b, j: (b, 0, 0, 0)),
                  pl.BlockSpec((None, tb, heads, hw), lambda b, j: (b, j, 0, 0)),
                  pl.BlockSpec((None, tb, heads, hw), lambda b, j: (b, j, 0, 0)),
                  pl.BlockSpec((None, tq * heads, hw), lambda b, j: (b, 0, 0)),
                  pl.BlockSpec((None, tq * heads, hw), lambda b, j: (b, 0, 0)),
                  pl.BlockSpec((2, 2, rq, t * heads), lambda b, j: (0, 0, 0, 0)),
                  pl.BlockSpec((2, rq, tq * heads), lambda b, j: (0, 0, 0)),
                  vec(HEAD_DIM), vec(HEAD_DIM), vec(HEAD_DIM), vec(HEAD_DIM), vec(hw)],
        out_specs=pl.BlockSpec((None, rq, hw), lambda b, j: (b, 0, 0)),
        out_shape=jax.ShapeDtypeStruct((db, rq, hw), BF16),
        scratch_shapes=[pltpu.VMEM((2, rq, 1), F32), pltpu.VMEM((2, rq, 1), F32), pltpu.VMEM((2, rq, hw), F32)],
        compiler_params=_params("arbitrary", "arbitrary"),
        name="attn_sample",
    )(q, cache_k, cache_v, k_new, v_new, bias, bias_new, *lams, subln_w)


CONV_HIST = 32


def _conv_kernel(cur_ref, hist_ref, w_ref, b_ref, g_ref, beta_ref, o_ref, xs_ref, wb_ref, *, width, zero_first):
    nb, tt, c = cur_ref.shape
    rows = 8
    length = CONV_HIST + tt

    @pl.when(jnp.logical_and(pl.program_id(0) == 0, pl.program_id(1) == 0))
    def _():
        for tap in range(width):
            wb_ref[tap] = jnp.broadcast_to(w_ref[tap:tap + 1, :], (rows, c))

    hist = hist_ref[...]
    if zero_first:
        hist = jnp.where(pl.program_id(1) == 0, 0.0, hist)
    xs_ref[0, :, 0:CONV_HIST, :] = hist
    xs_ref[0, :, CONV_HIST:, :] = cur_ref[...]
    for s in range(1, rows):
        xs_ref[s, :, 0:length - rows, :] = xs_ref[0, :, s:s + length - rows, :]
    lead = CONV_HIST - (width - 1)
    for n in range(nb):
        for r0 in range(0, tt, rows):
            acc = jnp.zeros((rows, c), F32) + b_ref[...]
            for tap in range(width):
                s = (lead + tap) % rows
                a = lead + tap - s + r0
                acc = acc + xs_ref[s, n, a:a + rows, :] * wb_ref[tap]
            mu = jnp.mean(acc, axis=-1, keepdims=True)
            cen = acc - mu
            var = jnp.mean(cen * cen, axis=-1, keepdims=True)
            y = cen * lax.rsqrt(var + LN_EPS) * g_ref[...] + beta_ref[...]
            o_ref[n, r0:r0 + rows, :] = _silu(y).astype(o_ref.dtype)


def _conv(cur3, hist3, hist_index, nb, tt, w, b, g, beta, zero_first):
    nseq, seq, c = cur3.shape
    width = w.shape[0]
    assert width - 1 <= CONV_HIST and seq % tt == 0 and nseq % nb == 0 and tt % 8 == 0
    vec = lambda: pl.BlockSpec((1, c), lambda s, i: (0, 0))
    return pl.pallas_call(
        functools.partial(_conv_kernel, width=width, zero_first=zero_first),
        grid=(nseq // nb, seq // tt),
        in_specs=[pl.BlockSpec((nb, tt, c), lambda s, i: (s, i, 0)),
                  pl.BlockSpec((nb, CONV_HIST, c), hist_index),
                  pl.BlockSpec((width, c), lambda s, i: (0, 0)),
                  vec(), vec(), vec()],
        out_specs=pl.BlockSpec((nb, tt, c), lambda s, i: (s, i, 0)),
        out_shape=jax.ShapeDtypeStruct(cur3.shape, BF16),
        scratch_shapes=[pltpu.VMEM((8, nb, CONV_HIST + tt, c), F32), pltpu.VMEM((width, 8, c), F32)],
        compiler_params=_params("arbitrary", "arbitrary"),
        name="conv",
    )(cur3, hist3, w, b, g, beta)


def _first_argmax(v, iota, axis, size):
    m = jnp.max(v, axis=axis, keepdims=True)
    i = jnp.min(jnp.where(v == m, iota, size), axis=axis, keepdims=True)
    return m, i


def _router_kernel(h_ref, w_ref, bias_ref, idx_ref, gate_ref, rank_ref, cnt_ref, carry_ref):
    n_exp = w_ref.shape[0]
    tm = h_ref.shape[0]
    per_group = n_exp // N_EXPERT_GROUPS

    @pl.when(pl.program_id(0) == 0)
    def _():
        carry_ref[...] = jnp.zeros(carry_ref.shape, F32)

    w = w_ref[...]
    w1 = w.astype(BF16)
    r1 = w - w1.astype(F32)
    w2 = r1.astype(BF16)
    w3 = (r1 - w2.astype(F32)).astype(BF16)
    h = h_ref[...]
    logits = _dot_nt(w1, h) + _dot_nt(w2, h) + _dot_nt(w3, h)
    scores = jax.nn.sigmoid(logits)
    choice = scores + bias_ref[...]

    ch3 = choice.reshape(N_EXPERT_GROUPS, per_group, tm)
    io3 = lax.broadcasted_iota(I32, ch3.shape, 1)
    m1, i1 = _first_argmax(ch3, io3, 1, per_group)
    m2 = jnp.max(jnp.where(io3 == i1, -jnp.inf, ch3), axis=1, keepdims=True)
    grp = (m1 + m2).reshape(N_EXPERT_GROUPS, tm)
    gio = lax.broadcasted_iota(I32, grp.shape, 0)
    keep = jnp.zeros(grp.shape, jnp.bool_)
    for _ in range(TOPK_GROUPS):
        _, gi = _first_argmax(grp, gio, 0, N_EXPERT_GROUPS)
        hit = gio == gi
        keep = jnp.logical_or(keep, hit)
        grp = jnp.where(hit, -jnp.inf, grp)
    keep3 = jnp.broadcast_to(keep.reshape(N_EXPERT_GROUPS, 1, tm), ch3.shape)
    masked = jnp.where(keep3, ch3, -jnp.inf).reshape(n_exp, tm)

    eio = lax.broadcasted_iota(I32, masked.shape, 0)
    sel = jnp.zeros(masked.shape, jnp.bool_)
    picks, pick_scores = [], []
    for _ in range(TOP_K):
        _, ei = _first_argmax(masked, eio, 0, n_exp)
        hit = eio == ei
        picks.append((ei, hit))
        pick_scores.append(jnp.sum(jnp.where(hit, scores, 0.0), axis=0, keepdims=True))
        sel = jnp.logical_or(sel, hit)
        masked = jnp.where(hit, -jnp.inf, masked)
    total = pick_scores[0]
    for s in pick_scores[1:]:
        total = total + s

    sel_b = jnp.where(sel, 1.0, 0.0).astype(BF16)
    tri = (lax.broadcasted_iota(I32, (tm, tm), 0) < lax.broadcasted_iota(I32, (tm, tm), 1))
    rank = jnp.dot(sel_b, jnp.where(tri, 1.0, 0.0).astype(BF16), preferred_element_type=F32) + carry_ref[...]
    for kk, (ei, hit) in enumerate(picks):
        idx_ref[kk:kk + 1, :] = ei
        gate_ref[kk:kk + 1, :] = pick_scores[kk] / total * ROUTED_SCALE
        rank_ref[kk:kk + 1, :] = jnp.sum(jnp.where(hit, rank, 0.0), axis=0, keepdims=True).astype(I32)
    carry_ref[...] = carry_ref[...] + jnp.sum(jnp.where(sel, 1.0, 0.0), axis=1, keepdims=True)
    cnt_ref[...] = carry_ref[...].astype(I32)


def _router(h2b, w_router_t, bias_col):
    t, d = h2b.shape
    n_exp = w_router_t.shape[0]
    tm = math.gcd(512, t)
    assert t % tm == 0
    tok = lambda: pl.BlockSpec((TOP_K, tm), lambda i: (0, i))
    return pl.pallas_call(
        _router_kernel,
        grid=(t // tm,),
        in_specs=[pl.BlockSpec((tm, d), lambda i: (i, 0)),
                  pl.BlockSpec((n_exp, d), lambda i: (0, 0)),
                  pl.BlockSpec((n_exp, 1), lambda i: (0, 0))],
        out_specs=[tok(), tok(), tok(), pl.BlockSpec((n_exp, 1), lambda i: (0, 0))],
        out_shape=[jax.ShapeDtypeStruct((TOP_K, t), I32), jax.ShapeDtypeStruct((TOP_K, t), F32),
                   jax.ShapeDtypeStruct((TOP_K, t), I32), jax.ShapeDtypeStruct((n_exp, 1), I32)],
        scratch_shapes=[pltpu.VMEM((n_exp, 1), F32)],
        compiler_params=_params("arbitrary"),
        name="router",
    )(h2b, w_router_t, bias_col)


def _dispatch_kernel(rows_ref, h_ref, xs_ref, sem):
    n_tok = h_ref.shape[0]

    def body(t, carry):
        for kk in range(TOP_K):
            pltpu.make_async_copy(h_ref.at[pl.ds(t, 1)], xs_ref.at[pl.ds(rows_ref[kk, t], 1)], sem).start()
        return carry

    lax.fori_loop(0, n_tok, body, 0)
    for _ in range(TOP_K):
        pltpu.make_async_copy(h_ref, xs_ref.at[pl.ds(0, n_tok)], sem).wait()


def _dispatch(rows, h2p, n_rows):
    t, w = h2p.shape
    nt = math.gcd(DISPATCH_ROWS, t)
    assert t % nt == 0 and n_rows >= nt
    rows = jnp.transpose(rows.reshape(TOP_K, t // nt, nt), (1, 0, 2))
    return pl.pallas_call(
        _dispatch_kernel,
        grid=(t // nt,),
        in_specs=[pl.BlockSpec((None, TOP_K, nt), lambda i: (i, 0, 0), memory_space=pltpu.SMEM),
                  pl.BlockSpec((nt, w), lambda i: (i, 0))],
        out_specs=pl.BlockSpec(memory_space=pl.ANY),
        out_shape=jax.ShapeDtypeStruct((n_rows, w), h2p.dtype),
        scratch_shapes=[pltpu.SemaphoreType.DMA(())],
        compiler_params=pltpu.CompilerParams(dimension_semantics=("arbitrary",), has_side_effects=True),
        name="dispatch",
    )(rows, h2p)


def _seg_kernel(se_ref, sb0_ref, snb_ref, srows_ref, nseg_ref, xs_hbm, wg_hbm, wu_hbm, wd_hbm, ys_hbm,
                xbuf, wgf, wuf, wdf, wgb, wub, wdb, gacc, uacc, hid, ybuf, sem_x, sem_w, sem_y, ycnt):
    s = pl.program_id(0)
    nseg = nseg_ref[0]
    kc, phases = EXP_KCHUNKS, EXP_KCHUNKS + EXP_NCHUNKS
    r = EXP_ROWS
    dk, dn = wgf.shape[1], wdf.shape[2]
    hw = dk // 2

    def w_copies(e, ph):
        slot = ph & 1
        out = []
        if ph < kc:
            step = dk // W_DMA_PARTS
            for part in range(W_DMA_PARTS):
                dst = pl.ds(part * step, step)
                src = pl.ds(ph * dk + part * step, step)
                out.append(pltpu.make_async_copy(wg_hbm.at[e, src, :], wgf.at[slot, dst, :], sem_w.at[0, slot]))
                out.append(pltpu.make_async_copy(wu_hbm.at[e, src, :], wuf.at[slot, dst, :], sem_w.at[1, slot]))
            return out
        step = wdf.shape[1] // W_DMA_PARTS
        for part in range(W_DMA_PARTS):
            rows = pl.ds(part * step, step)
            out.append(pltpu.make_async_copy(wd_hbm.at[e, rows, pl.ds((ph - kc) * dn, dn)], wdf.at[slot, rows, :],
                                             sem_w.at[2, slot]))
        return out

    def for_x_blocks(seg, slot, action):
        for b in range(SEG_BLOCKS):
            @pl.when(b < snb_ref[seg])
            def _():
                row = pl.multiple_of((sb0_ref[seg] + b) * r, r)
                action(pltpu.make_async_copy(xs_hbm.at[pl.ds(row, r), :], xbuf.at[slot, pl.ds(b * r, r), :],
                                             sem_x.at[slot]))

    def y_copy(slot, row, chunk):
        return pltpu.make_async_copy(ybuf.at[slot], ys_hbm.at[pl.ds(row, r), pl.ds(chunk * (dn // 2), dn // 2)],
                                     sem_y.at[slot])

    @pl.when(s < nseg)
    def _():
        e, nb, par = se_ref[s], snb_ref[s], s & 1
        row_base = sb0_ref[s] * r
        valid = srows_ref[s]

        @pl.when(s == 0)
        def _():
            ycnt[0] = 0
            for_x_blocks(0, 0, lambda c: c.start())
            for ph in range(phases):
                for c in w_copies(e, ph):
                    c.start()

        @pl.when(s + 1 < nseg)
        def _():
            for_x_blocks(s + 1, 1 - par, lambda c: c.start())

        for_x_blocks(s, par, lambda c: c.wait())

        def refill(ph):
            @pl.when(s + 1 < nseg)
            def _():
                for c in w_copies(se_ref[s + 1], ph):
                    c.start()

        for ph in range(phases):
            for c in w_copies(e, ph):
                c.wait()
            slot = ph & 1
            if ph < kc:
                wgb[...] = wgf[slot].astype(BF16)
                wub[...] = wuf[slot].astype(BF16)
                refill(ph)

                def body(b, carry, ph=ph):
                    r0 = pl.multiple_of(b * r, r)
                    rows = r0 + lax.broadcasted_iota(I32, (r, 1), 0)
                    words = xbuf[par, pl.ds(r0, r), ph * hw:(ph + 1) * hw]
                    lo, hi = _unpack_halves(jnp.where(rows < valid, words, jnp.uint32(0)))
                    g = (jnp.dot(lo, wgb[:hw, :], preferred_element_type=F32)
                         + jnp.dot(hi, wgb[hw:, :], preferred_element_type=F32))
                    u = (jnp.dot(lo, wub[:hw, :], preferred_element_type=F32)
                         + jnp.dot(hi, wub[hw:, :], preferred_element_type=F32))
                    if ph > 0:
                        g = g + gacc[pl.ds(r0, r), :]
                        u = u + uacc[pl.ds(r0, r), :]
                    if ph < kc - 1:
                        gacc[pl.ds(r0, r), :] = g
                        uacc[pl.ds(r0, r), :] = u
                    else:
                        hid[pl.ds(r0, r), :] = (_silu(g) * u).astype(BF16)
                    return carry
            else:
                wdb[...] = wdf[slot].astype(BF16)
                refill(ph)

                def body(b, carry, ph=ph):
                    r0 = pl.multiple_of(b * r, r)
                    n = ycnt[0]
                    ys = n & 1

                    @pl.when(n >= 2)
                    def _():
                        y_copy(ys, 0, 0).wait()

                    ybuf[ys] = _pack_halves(jnp.dot(hid[pl.ds(r0, r), :], wdb[...], preferred_element_type=F32))
                    y_copy(ys, pl.multiple_of(row_base + r0, r), ph - kc).start()
                    ycnt[0] = n + 1
                    return carry

            lax.fori_loop(0, nb, body, 0)

        @pl.when(s == nseg - 1)
        def _():
            n = ycnt[0]
            for back in (1, 2):
                @pl.when(n >= back)
                def _():
                    y_copy((n - back) & 1, 0, 0).wait()


def _expert_segments(counts, n_blocks_max):
    n_exp = counts.shape[0]
    span = SEG_BLOCKS * EXP_ROWS
    nb_e = (counts + EXP_ROWS - 1) // EXP_ROWS
    blk_start_e = jnp.cumsum(nb_e) - nb_e
    nseg_e = (nb_e + SEG_BLOCKS - 1) // SEG_BLOCKS
    seg_end_e = jnp.cumsum(nseg_e)
    n_seg_max = n_exp + n_blocks_max // SEG_BLOCKS
    sid = jnp.arange(n_seg_max, dtype=I32)
    seg_e = jnp.minimum(jnp.sum(seg_end_e[None, :] <= sid[:, None], axis=1), n_exp - 1).astype(I32)
    mine = seg_e[:, None] == jnp.arange(n_exp, dtype=I32)[None, :]
    of_expert = lambda v: jnp.sum(jnp.where(mine, v[None, :], 0), axis=1)
    part = sid - of_expert(seg_end_e - nseg_e)
    seg_nb = jnp.clip(of_expert(nb_e) - part * SEG_BLOCKS, 0, SEG_BLOCKS)
    seg_blk0 = of_expert(blk_start_e) + part * SEG_BLOCKS
    seg_rows = jnp.clip(of_expert(counts) - part * span, 0, span)
    to_i32 = lambda v: v.astype(I32)
    return (seg_e, to_i32(seg_blk0), to_i32(seg_nb), to_i32(seg_rows), to_i32(seg_end_e[-1:]),
            to_i32(blk_start_e * EXP_ROWS))


def _experts_by_segment(seg_tables, xs, w_gate, w_up, w_down):
    n_exp, d, ff = w_gate.shape
    kc, nc = EXP_KCHUNKS, EXP_NCHUNKS
    dk, dn = d // kc, d // nc
    span = SEG_BLOCKS * EXP_ROWS
    assert kc <= 2 and nc <= 2
    hbm = lambda: pl.BlockSpec(memory_space=pl.ANY)
    grid_spec = pltpu.PrefetchScalarGridSpec(
        num_scalar_prefetch=5,
        grid=(seg_tables[0].shape[0],),
        in_specs=[hbm(), hbm(), hbm(), hbm()],
        out_specs=hbm(),
        scratch_shapes=[pltpu.VMEM((2, span, d // 2), U32),
                        pltpu.VMEM((2, dk, ff), F32), pltpu.VMEM((2, dk, ff), F32), pltpu.VMEM((2, ff, dn), F32),
                        pltpu.VMEM((dk, ff), BF16), pltpu.VMEM((dk, ff), BF16), pltpu.VMEM((ff, dn), BF16),
                        pltpu.VMEM((span, ff), F32), pltpu.VMEM((span, ff), F32), pltpu.VMEM((span, ff), BF16),
                        pltpu.VMEM((2, EXP_ROWS, dn // 2), U32),
                        pltpu.SemaphoreType.DMA((2,)), pltpu.SemaphoreType.DMA((3, 2)),
                        pltpu.SemaphoreType.DMA((2,)), pltpu.SMEM((1,), I32)],
    )
    return pl.pallas_call(
        _seg_kernel,
        grid_spec=grid_spec,
        out_shape=jax.ShapeDtypeStruct((xs.shape[0], d // 2), U32),
        compiler_params=_params("arbitrary"),
        name="experts",
    )(*seg_tables, xs, w_gate, w_up, w_down)


def _combine_kernel(rows_ref, next_rows_ref, gate_ref, hs_ref, wsd_ref, x1_ref, g2_ref, gpost_ref, ys_ref,
                    yp_ref, ysm_ref, buf_ref, wb_ref, sem, *, prompt_steps):
    i = pl.program_id(0)
    n_tok = gate_ref.shape[0]
    cur = i & 1

    def gather(table_ref, slot, action):
        def body(t, carry):
            for kk in range(TOP_K):
                action(pltpu.make_async_copy(ys_ref.at[pl.ds(table_ref[kk, t], 1)],
                                             buf_ref.at[slot, kk, pl.ds(t, 1)], sem.at[slot]))
            return carry
        lax.fori_loop(0, n_tok, body, 0)

    @pl.when(i == 0)
    def _():
        wb_ref[...] = wsd_ref[...].astype(BF16)
        gather(rows_ref, 0, lambda c: c.start())

    @pl.when(i + 1 < pl.num_programs(0))
    def _():
        gather(next_rows_ref, 1 - cur, lambda c: c.start())

    shared = jnp.dot(hs_ref[...], wb_ref[...], preferred_element_type=F32)
    for kk in range(TOP_K):
        pltpu.make_async_copy(ys_ref.at[pl.ds(0, n_tok)], buf_ref.at[cur, kk], sem.at[cur]).wait()
    gates = gate_ref[...]
    d = x1_ref.shape[1]
    dn = d // EXP_NCHUNKS
    hw = dn // 2
    pieces = []
    for c in range(EXP_NCHUNKS):
        lo = shared[:, c * dn:c * dn + hw]
        hi = shared[:, c * dn + hw:(c + 1) * dn]
        for kk in range(TOP_K):
            words = buf_ref[cur, kk, :, c * hw:(c + 1) * hw]
            g = gates[:, kk:kk + 1]
            lo = lo + lax.bitcast_convert_type(words << 16, F32) * g
            hi = hi + lax.bitcast_convert_type(words & jnp.uint32(0xFFFF0000), F32) * g
        pieces += [(c * dn, lo), (c * dn + hw, hi)]

    ssq = functools.reduce(jnp.add, [jnp.sum(f * f, axis=-1, keepdims=True) for _, f in pieces])
    inv = lax.rsqrt(ssq / d + RMS_EPS)
    groups, group_rows = g2_ref.shape[0], n_tok // g2_ref.shape[0]
    outs = []
    for col, f in pieces:
        cols = slice(col, col + hw)
        normed = (f * inv * gpost_ref[:, cols]).reshape(groups, group_rows, hw)
        outs.append((cols, x1_ref[:, cols] + (g2_ref[:, :, cols] * normed).reshape(n_tok, hw)))

    @pl.when(i < prompt_steps)
    def _():
        for cols, y in outs:
            yp_ref[:, cols] = y

    @pl.when(i >= prompt_steps)
    def _():
        for cols, y in outs:
            ysm_ref[:, cols] = y


def _combine(rows, gates_t, hid_sh, w_sh_down, ys, x1, g2_groups, gpost, n_prompt):
    t, ff = hid_sh.shape
    d = w_sh_down.shape[1]
    nt = math.gcd(math.gcd(COMBINE_ROWS, n_prompt), t - n_prompt)
    n_steps, prompt_steps = t // nt, n_prompt // nt
    groups = g2_groups.shape[0] * nt // t
    assert groups >= 1 and nt % groups == 0 and (nt // groups) % 8 == 0
    rows = jnp.transpose(rows.reshape(TOP_K, n_steps, nt), (1, 0, 2))
    rows_spec = lambda ahead: pl.BlockSpec((None, TOP_K, nt), lambda i: (jnp.minimum(i + ahead, n_steps - 1), 0, 0),
                                           memory_space=pltpu.SMEM)
    return pl.pallas_call(
        functools.partial(_combine_kernel, prompt_steps=prompt_steps),
        grid=(n_steps,),
        in_specs=[rows_spec(0), rows_spec(1),
                  pl.BlockSpec((nt, TOP_K), lambda i: (i, 0)),
                  pl.BlockSpec((nt, ff), lambda i: (i, 0)),
                  pl.BlockSpec((ff, d), lambda i: (0, 0), pipeline_mode=pl.Buffered(1)),
                  pl.BlockSpec((nt, d), lambda i: (i, 0)),
                  pl.BlockSpec((groups, 1, d), lambda i: (i, 0, 0)),
                  pl.BlockSpec((1, d), lambda i: (0, 0)),
                  pl.BlockSpec(memory_space=pl.ANY)],
        out_specs=[pl.BlockSpec((nt, d), lambda i: (jnp.minimum(i, prompt_steps - 1), 0)),
                   pl.BlockSpec((nt, d), lambda i: (jnp.maximum(i - prompt_steps, 0), 0))],
        out_shape=[jax.ShapeDtypeStruct((n_prompt, d), F32), jax.ShapeDtypeStruct((t - n_prompt, d), F32)],
        scratch_shapes=[pltpu.VMEM((2, TOP_K, nt, d // 2), U32), pltpu.VMEM((ff, d), BF16),
                        pltpu.SemaphoreType.DMA((2,))],
        compiler_params=_params("arbitrary"),
        name="combine",
    )(rows, rows, gates_t, hid_sh, w_sh_down, x1, g2_groups, gpost, ys)


def _layer(l, lam_init, x_prompt, x_sample, c_prompt, c_sample, cache_k, cache_v, state_conv, rel_bias, p):
    batch, seq, d = x_prompt.shape
    db, dseq, _ = x_sample.shape
    past, heads = cache_k.shape[1], cache_k.shape[2]
    hw = 2 * HEAD_DIM
    aw = heads * hw
    cc = p["conv_dw_w"].shape[1]
    width = p["conv_dw_w"].shape[0]
    assert p["w_in"].shape[1] == 3 * aw + 2 * cc and seq % dseq == 0
    tp, ts = batch * seq, db * dseq
    row = lambda v: v.reshape(1, -1)

    n_mod_rows = -(-(db + batch) // 16) * 16
    c_all = jnp.concatenate([c_sample, c_prompt, jnp.zeros((n_mod_rows - db - batch, d), F32)], axis=0)
    mod3 = _ada(c_all, p["w_ada"], row(p["b_ada"])).reshape(n_mod_rows, 1, N_MOD * d)

    gp = tp // dseq
    gps = seq // dseq
    nb_p = math.gcd(ROW_GROUPS, gps)
    nb_s = math.gcd(ROW_GROUPS, db)
    assert gps % nb_p == 0 and db % nb_s == 0 and gp % nb_s == 0
    plan_p = _RowPlan(gp, dseq, d, nb_p, 1, lambda i: db + (i * nb_p) // gps, 0)
    plan_s = _RowPlan(db, dseq, d, nb_s, nb_s, lambda i: i, gp // nb_s)
    xp3 = x_prompt.reshape(gp, dseq, d)

    hp = _prenorm(plan_p, xp3, row(p["g_pre_mix"]), mod3).reshape(tp, d)
    hs = _prenorm(plan_s, x_sample, row(p["g_pre_mix"]), mod3).reshape(ts, d)

    w_in = p["w_in"]
    proj = lambda h, off, nm: _mm([h], [(w_in, off)], aw, _identity, F32, MM_ROWS, 512, nm)
    qp, kp, vp = proj(hp, 0, "q_prompt"), proj(hp, aw, "k_prompt"), proj(hp, 2 * aw, "v_prompt")
    qs, ks, vs = proj(hs, 0, "q_sample"), proj(hs, aw, "k_sample"), proj(hs, 2 * aw, "v_sample")
    glu_cols = [(w_in, 3 * aw), (w_in, 3 * aw + cc)]
    glu_p = _mm([hp], glu_cols, cc, _glu, F32, MM_ROWS, 256, "glu_prompt")
    glu_s = _mm([hs], glu_cols, cc, _glu, F32, MM_ROWS, 256, "glu_sample")

    lams = [row(p[n]) for n in ("lambda_q1", "lambda_k1", "lambda_q2", "lambda_k2")]
    subln = row(p["subln_w"])
    attn_p = _attn_prompt(qp, kp, vp, rel_bias, lams, subln, batch, seq, heads, lam_init)

    new_k_s = ks.reshape(db, dseq, heads, hw)
    new_v_s = vs.reshape(db, dseq, heads, hw)
    q_s = jnp.transpose(qs.reshape(db, dseq, heads, 2, HEAD_DIM), (0, 3, 2, 1, 4)).reshape(db, 2, heads * dseq, HEAD_DIM)
    o_s = _attn_sample(q_s, cache_k, cache_v, new_k_s.reshape(db, dseq * heads, hw),
                       new_v_s.reshape(db, dseq * heads, hw), rel_bias, lams, subln, lam_init)
    attn_s = jnp.transpose(o_s.reshape(db, heads, dseq, hw), (0, 2, 1, 3)).reshape(ts, aw)

    conv_args = (p["conv_dw_w"], row(p["conv_dw_b"]), row(p["conv_ln_g"]), row(p["conv_ln_b"]))
    glu_p3 = glu_p.reshape(batch, seq, cc)
    tt = min(128, seq)
    per = tt // CONV_HIST
    conv_p = _conv(glu_p3, glu_p3, lambda s, i: (s, jnp.maximum(i * per - 1, 0), 0), 1, tt, *conv_args,
                   zero_first=True).reshape(tp, cc)
    glu_s3 = glu_s.reshape(db, dseq, cc)
    hist_s = jnp.concatenate([jnp.zeros((db, CONV_HIST - (width - 1), cc), F32), state_conv], axis=1)
    nb_c = math.gcd(2, db)
    conv_s = _conv(glu_s3, hist_s, lambda s, i: (s, 0, 0), nb_c, dseq, *conv_args, zero_first=False).reshape(ts, cc)
    new_conv_p = glu_p3[:, seq - (width - 1):]
    new_conv_s = jnp.concatenate([state_conv, glu_s3], axis=1)[:, -(width - 1):]

    mix_p = _mm([attn_p, conv_p], [(p["w_out"], 0)], d, _identity, F32, MM_ROWS, 512, "out_prompt")
    mix_s = _mm([attn_s, conv_s], [(p["w_out"], 0)], d, _identity, F32, MM_ROWS, 512, "out_sample")

    gt = gp + db
    gpost, gpre = row(p["g_post_mix"]), row(p["g_pre_ffn"])
    shared_bufs = _postmix(plan_p, xp3, mix_p.reshape(gp, dseq, d), gpost, gpre, mod3, gt)
    x1a, h2ba, h2pa = _postmix(plan_s, x_sample, mix_s.reshape(db, dseq, d), gpost, gpre, mod3, gt,
                               prev=shared_bufs)
    t = tp + ts
    x1 = x1a.reshape(t, d)
    h2b = h2ba.reshape(t, d)
    h2p = h2pa.reshape(t, d // 2)

    n_exp = p["w_router"].shape[1]
    idx, gates, rank, counts = _router(h2b, p["w_router"].T, p["b_router_corr"].reshape(n_exp, 1))
    n_blocks_max = (t * TOP_K) // EXP_ROWS + n_exp
    *seg_tables, row_start = _expert_segments(counts.reshape(n_exp), n_blocks_max)
    pick = idx[..., None] == jnp.arange(n_exp, dtype=I32)
    rows = jnp.sum(jnp.where(pick, row_start, 0), axis=-1) + rank

    xs = _dispatch(rows.astype(I32), h2p, n_blocks_max * EXP_ROWS)
    ys = _experts_by_segment(seg_tables, xs, p["w_exp_gate"], p["w_exp_up"], p["w_exp_down"])
    ff_sh = p["w_sh_gate"].shape[1]
    hid_sh = _mm([h2b], [(p["w_sh_gate"], 0), (p["w_sh_up"], 0)], ff_sh, _swiglu, BF16, MM_ROWS, 256, "shared_up")
    seq_of_group = np.concatenate([db + np.repeat(np.arange(batch), gps), np.arange(db)])
    g2_groups = jnp.take(mod3[:, :, (N_MOD - 1) * d:], jnp.asarray(seq_of_group, I32), axis=0)
    yp, ysmp = _combine(rows.astype(I32), gates.T, hid_sh, p["w_sh_down"], ys, x1, g2_groups,
                        row(p["g_post_ffn"]), tp)
    yp = yp.reshape(batch, seq, d)
    ysmp = ysmp.reshape(db, dseq, d)
    new_k_p = kp.reshape(batch, seq, heads, hw)
    new_v_p = vp.reshape(batch, seq, heads, hw)
    return yp, ysmp, new_k_p, new_v_p, new_conv_p, new_k_s, new_v_s, new_conv_s


def kernel(x_prompt, x_sample, c_prompt, c_sample, cache_k, cache_v, state_conv, rel_bias, w_ada, b_ada, g_pre_mix, g_post_mix, g_pre_ffn, g_post_ffn, w_in, lambda_q1, lambda_k1, lambda_q2, lambda_k2, subln_w, conv_dw_w, conv_dw_b, conv_ln_g, conv_ln_b, w_out, w_router, b_router_corr, w_exp_gate, w_exp_up, w_exp_down, w_sh_gate, w_sh_up, w_sh_down):
    weights = dict(w_ada=w_ada, b_ada=b_ada, g_pre_mix=g_pre_mix, g_post_mix=g_post_mix, g_pre_ffn=g_pre_ffn,
                   g_post_ffn=g_post_ffn, w_in=w_in, lambda_q1=lambda_q1, lambda_k1=lambda_k1,
                   lambda_q2=lambda_q2, lambda_k2=lambda_k2, subln_w=subln_w, conv_dw_w=conv_dw_w,
                   conv_dw_b=conv_dw_b, conv_ln_g=conv_ln_g, conv_ln_b=conv_ln_b, w_out=w_out,
                   w_router=w_router, b_router_corr=b_router_corr, w_exp_gate=w_exp_gate, w_exp_up=w_exp_up,
                   w_exp_down=w_exp_down, w_sh_gate=w_sh_gate, w_sh_up=w_sh_up, w_sh_down=w_sh_down)
    depth = w_in.shape[0]
    xp, xs = x_prompt, x_sample
    outs = [[] for _ in range(6)]
    for l in range(depth):
        p = {k: (v.reshape(v.shape[1:]) if depth == 1 else v[l]) for k, v in weights.items()}
        lam_init = 0.8 - 0.6 * math.exp(-0.3 * l)
        ck, cv, sc = ((a.reshape(a.shape[1:]) if depth == 1 else a[l]) for a in (cache_k, cache_v, state_conv))
        xp, xs, *state = _layer(l, lam_init, xp, xs, c_prompt, c_sample, ck, cv, sc, rel_bias, p)
        for acc, s in zip(outs, state):
            acc.append(s)
    return (xp, xs) + tuple(jnp.stack(o) for o in outs)
```

```python
import functools
import math

import numpy as np
import jax
import jax.numpy as jnp
from jax import lax
from jax.experimental import pallas as pl
from jax.experimental.pallas import tpu as pltpu

F32 = jnp.float32
BF16 = jnp.bfloat16
I32 = jnp.int32
U32 = jnp.uint32

CHUNK = 64
HEAD_DIM = 128
NUM_BUCKETS = 32
REL_MAX_DISTANCE = 128
TOP_K = 8
N_EXPERT_GROUPS = 8
TOPK_GROUPS = 4
ROUTED_SCALE = 2.5
RMS_EPS = 1e-6
SUBLN_EPS = 1e-5
LN_EPS = 1e-5
N_MOD = 6
NEG = -1e30
ATT_SCALE = HEAD_DIM ** -0.5

VMEM_LIMIT_BYTES = 56 * 1024 * 1024
ATT_TILE = 256
EXP_ROWS = 128
SEG_BLOCKS = 6
W_DMA_PARTS = 4
EXP_KCHUNKS = 2
EXP_NCHUNKS = 2
COMBINE_ROWS = 128
DISPATCH_ROWS = 512
ROW_GROUPS = 8
MM_ROWS = 1024


def _params(*sem):
    return pltpu.CompilerParams(dimension_semantics=sem, vmem_limit_bytes=VMEM_LIMIT_BYTES)


def _silu(x):
    return x * jax.nn.sigmoid(x)


def _dot_nt(a, b):
    return lax.dot_general(a, b, (((1,), (1,)), ((), ())), preferred_element_type=F32)


def _ada_kernel(c_ref, w_ref, b_ref, o_ref):
    a = _silu(c_ref[...]).astype(BF16)
    o_ref[...] = jnp.dot(a, w_ref[...].astype(BF16), preferred_element_type=F32) + b_ref[...]


def _ada(c_all, w_ada, b_ada):
    rows, d = c_all.shape
    n = w_ada.shape[1]
    tn = min(512, n)
    return pl.pallas_call(
        _ada_kernel,
        grid=(n // tn,),
        in_specs=[pl.BlockSpec((rows, d), lambda j: (0, 0)),
                  pl.BlockSpec((d, tn), lambda j: (0, j)),
                  pl.BlockSpec((1, tn), lambda j: (0, j))],
        out_specs=pl.BlockSpec((rows, tn), lambda j: (0, j)),
        out_shape=jax.ShapeDtypeStruct((rows, n), F32),
        compiler_params=_params("arbitrary"),
        name="ada",
    )(c_all, w_ada, b_ada)


class _RowPlan:
    def __init__(self, n_groups, group_rows, d, nb, mod_nb, mod_index, out_block_offset=0):
        self.n_groups, self.group_rows, self.d, self.nb = n_groups, group_rows, d, nb
        self.mod_nb, self.mod_index, self.out_block_offset = mod_nb, mod_index, out_block_offset
        self.grid = (n_groups // nb,)

    def act(self, offset=0):
        return pl.BlockSpec((self.nb, self.group_rows, self.d), lambda i: (i + offset, 0, 0))

    def mod(self, chunk):
        return pl.BlockSpec((self.mod_nb, 1, self.d), lambda i: (self.mod_index(i), 0, chunk))

    def vec(self):
        return pl.BlockSpec((1, self.d), lambda i: (0, 0))


def _rms(x, g, eps):
    return x * lax.rsqrt(jnp.mean(x * x, axis=-1, keepdims=True) + eps) * g


def _prenorm_kernel(x_ref, g_ref, sh_ref, sc_ref, o_ref):
    y = _rms(x_ref[...], g_ref[...], RMS_EPS)
    o_ref[...] = (y * (1.0 + sc_ref[...]) + sh_ref[...]).astype(o_ref.dtype)


def _prenorm(plan, x3, g, mod3):
    return pl.pallas_call(
        _prenorm_kernel,
        grid=plan.grid,
        in_specs=[plan.act(), plan.vec(), plan.mod(0), plan.mod(1)],
        out_specs=plan.act(),
        out_shape=jax.ShapeDtypeStruct(x3.shape, BF16),
        compiler_params=_params("arbitrary"),
        name="prenorm",
    )(x3, g, mod3, mod3)


def _pack_halves(x):
    n = x.shape[-1] // 2
    lo = lax.bitcast_convert_type(x[..., :n].astype(BF16).astype(F32), U32) >> 16
    hi = lax.bitcast_convert_type(x[..., n:].astype(BF16).astype(F32), U32) & jnp.uint32(0xFFFF0000)
    return hi | lo


def _unpack_halves(w):
    lo = lax.bitcast_convert_type(w << 16, F32).astype(BF16)
    hi = lax.bitcast_convert_type(w & jnp.uint32(0xFFFF0000), F32).astype(BF16)
    return lo, hi


def _postmix_kernel(x_ref, mix_ref, gpost_ref, gpre_ref, g1_ref, sh2_ref, sc2_ref, *rest):
    x1_ref, h2b_ref, h2p_ref = rest[-3:]
    x1 = x_ref[...] + g1_ref[...] * _rms(mix_ref[...], gpost_ref[...], RMS_EPS)
    x1_ref[...] = x1
    h2 = _rms(x1, gpre_ref[...], RMS_EPS) * (1.0 + sc2_ref[...]) + sh2_ref[...]
    h2b_ref[...] = h2.astype(BF16)
    dk = h2.shape[-1] // EXP_KCHUNKS
    for c in range(EXP_KCHUNKS):
        h2p_ref[:, :, c * (dk // 2):(c + 1) * (dk // 2)] = _pack_halves(h2[:, :, c * dk:(c + 1) * dk])


def _postmix(plan, x3, mix3, gpost, gpre, mod3, total_groups, prev=None):
    off = plan.out_block_offset
    in_specs = [plan.act(), plan.act(), plan.vec(), plan.vec(), plan.mod(2), plan.mod(3), plan.mod(4)]
    args = [x3, mix3, gpost, gpre, mod3, mod3, mod3]
    aliases = {}
    if prev is not None:
        in_specs += [pl.BlockSpec(memory_space=pl.ANY)] * 3
        args += list(prev)
        aliases = {7: 0, 8: 1, 9: 2}
    shape_all = (total_groups, plan.group_rows, plan.d)
    shape_packed = (total_groups, plan.group_rows, plan.d // 2)
    packed_spec = pl.BlockSpec((plan.nb, plan.group_rows, plan.d // 2), lambda i: (i + off, 0, 0))
    return pl.pallas_call(
        _postmix_kernel,
        grid=plan.grid,
        in_specs=in_specs,
        out_specs=[plan.act(off), plan.act(off), packed_spec],
        out_shape=[jax.ShapeDtypeStruct(shape_all, F32),
                   jax.ShapeDtypeStruct(shape_all, BF16),
                   jax.ShapeDtypeStruct(shape_packed, U32)],
        input_output_aliases=aliases,
        compiler_params=_params("arbitrary"),
        name="postmix",
    )(*args)


def _mm_kernel(*refs, n_a, n_w, k_sizes, epilogue):
    a_refs = refs[:n_a]
    w_refs = refs[n_a:n_a + n_w]
    o_ref = refs[n_a + n_w]
    wb_refs = refs[n_a + n_w + 1:]

    @pl.when(pl.program_id(1) == 0)
    def _():
        for w_ref, wb_ref in zip(w_refs, wb_refs):
            wb_ref[...] = w_ref[...].astype(BF16)

    parts = []
    for wb_ref in wb_refs:
        acc, k0 = None, 0
        for a_ref, ka in zip(a_refs, k_sizes):
            d = jnp.dot(a_ref[...], wb_ref[k0:k0 + ka, :], preferred_element_type=F32)
            acc = d if acc is None else acc + d
            k0 += ka
        parts.append(acc)
    o_ref[...] = epilogue(*parts).astype(o_ref.dtype)


def _mm(a_list, w_cols, n_out, epilogue, out_dtype, tm, tn, name):
    m = a_list[0].shape[0]
    k = w_cols[0][0].shape[0]
    tm, tn = math.gcd(tm, m), math.gcd(tn, n_out)
    k_sizes = tuple(a.shape[1] for a in a_list)
    assert sum(k_sizes) == k and m % tm == 0 and n_out % tn == 0 and all(c % tn == 0 for _, c in w_cols)
    a_specs = [pl.BlockSpec((tm, ka), lambda j, i: (i, 0)) for ka in k_sizes]
    w_specs = [pl.BlockSpec((k, tn), functools.partial(lambda j, i, o: (0, o + j), o=c // tn)) for _, c in w_cols]
    kern = functools.partial(_mm_kernel, n_a=len(a_list), n_w=len(w_cols), k_sizes=k_sizes, epilogue=epilogue)
    return pl.pallas_call(
        kern,
        grid=(n_out // tn, m // tm),
        in_specs=a_specs + w_specs,
        out_specs=pl.BlockSpec((tm, tn), lambda j, i: (i, j)),
        out_shape=jax.ShapeDtypeStruct((m, n_out), out_dtype),
        scratch_shapes=[pltpu.VMEM((k, tn), BF16) for _ in w_cols],
        compiler_params=_params("arbitrary", "arbitrary"),
        name=name,
    )(*a_list, *[w for w, _ in w_cols])


def _identity(x):
    return x


def _glu(a, b):
    return a * jax.nn.sigmoid(b)


def _swiglu(a, b):
    return _silu(a) * b


def _bucket(rel):
    half = NUM_BUCKETS // 2
    max_exact = half // 2
    n = np.abs(rel)
    nf = np.maximum(n, 1).astype(np.float32)
    large = max_exact + (np.log(nf / np.float32(max_exact)) / np.float32(math.log(REL_MAX_DISTANCE / max_exact))
                         * np.float32(half - max_exact)).astype(np.int32)
    large = np.minimum(large, half - 1)
    return np.where(rel > 0, half, 0) + np.where(n < max_exact, n, large)


def _bias_table(rel_bias, q_pos, k_pos):
    rel = k_pos[None, :] - q_pos[:, None]
    visible = (k_pos // CHUNK)[None, :] <= (q_pos // CHUNK)[:, None]
    onehot = (jnp.asarray(_bucket(rel))[..., None] == jnp.arange(NUM_BUCKETS)).astype(F32)
    b = jnp.einsum("qkb,bhm->hmqk", onehot, rel_bias.astype(F32), precision=lax.Precision.HIGHEST)
    return jnp.where(jnp.asarray(visible)[None, None], b, NEG)


def _lambda(lq1, lk1, lq2, lk2, lam_init):
    return (jnp.exp(jnp.sum(lq1[...] * lk1[...], keepdims=True))
            - jnp.exp(jnp.sum(lq2[...] * lk2[...], keepdims=True)) + lam_init)


def _softmax_update(state, s, v):
    m_old, l_old, acc_old = state
    m_new = jnp.maximum(m_old, jnp.max(s, axis=-1, keepdims=True))
    alpha = jnp.exp(m_old - m_new)
    p = jnp.exp(s - m_new)
    return (m_new, alpha * l_old + jnp.sum(p, axis=-1, keepdims=True),
            alpha * acc_old + jnp.dot(p.astype(BF16), v, preferred_element_type=F32))


def _attn_finish(states, lam, sw, lam_init, o_ref):
    (_, l1, acc1), (_, l2, acc2) = states
    o = acc1 / l1 - lam * (acc2 / l2)
    o = o * lax.rsqrt(jnp.mean(o * o, axis=-1, keepdims=True) + SUBLN_EPS) * sw * (1.0 - lam_init)
    o_ref[...] = o.astype(o_ref.dtype)


def _attn_prompt_kernel(q_ref, k_ref, v_ref, b_ref, lq1, lk1, lq2, lk2, sw_ref, o_ref, kb_ref, vb_ref,
                        *, lam_init, nq):
    t = ATT_TILE
    kb_ref[...] = k_ref[...].astype(BF16)
    vb_ref[...] = v_ref[...].astype(BF16)
    lam = _lambda(lq1, lk1, lq2, lk2, lam_init)
    sw = sw_ref[...]
    for qi in range(nq):
        q = q_ref[qi * t:(qi + 1) * t, :]
        n_far = max(qi - 1, 0) * t
        exps, sums = [], []
        for mp in range(2):
            cols = slice(mp * HEAD_DIM, (mp + 1) * HEAD_DIM)
            s = _dot_nt(q[:, cols].astype(BF16), kb_ref[0:(qi + 1) * t, cols]) * ATT_SCALE
            pieces = []
            if n_far:
                pieces.append(s[:, :n_far] + b_ref[mp, 2, 0:1, 0:1])
            if qi >= 1:
                pieces.append(s[:, n_far:n_far + t] + b_ref[mp, 1])
            pieces.append(s[:, qi * t:(qi + 1) * t] + b_ref[mp, 0])
            m = functools.reduce(jnp.maximum, [jnp.max(p, axis=-1, keepdims=True) for p in pieces])
            es = [jnp.exp(p - m) for p in pieces]
            exps.append(es)
            sums.append(functools.reduce(jnp.add, [jnp.sum(e, axis=-1, keepdims=True) for e in es]))
        c1 = 1.0 / sums[0]
        c2 = lam / sums[1]
        o, col = None, 0
        for e1, e2 in zip(exps[0], exps[1]):
            a = (e1 * c1 - e2 * c2).astype(BF16)
            d = jnp.dot(a, vb_ref[col:col + a.shape[1], :], preferred_element_type=F32)
            o = d if o is None else o + d
            col += a.shape[1]
        o = o * lax.rsqrt(jnp.mean(o * o, axis=-1, keepdims=True) + SUBLN_EPS) * sw * (1.0 - lam_init)
        o_ref[qi * t:(qi + 1) * t, :] = o.astype(o_ref.dtype)


def _attn_prompt(q, k, v, rel_bias, lams, subln_w, batch, seq, heads, lam_init):
    t = ATT_TILE
    assert seq % t == 0 and t % CHUNK == 0
    pos = np.arange(t)
    tiles = [_bias_table(rel_bias, pos + dt * t, pos) for dt in range(3)]
    assert np.all(_bucket(np.arange(-3 * t + 1, -t)) == NUM_BUCKETS // 2 - 1)
    bias = jnp.stack(tiles, axis=2)
    hw = 2 * HEAD_DIM
    vec = lambda n: pl.BlockSpec((1, n), lambda b, h: (0, 0))
    seq_spec = lambda: pl.BlockSpec((seq, hw), lambda b, h: (b, h))
    return pl.pallas_call(
        functools.partial(_attn_prompt_kernel, lam_init=lam_init, nq=seq // t),
        grid=(batch, heads),
        in_specs=[seq_spec(), seq_spec(), seq_spec(),
                  pl.BlockSpec((None, 2, 3, t, t), lambda b, h: (h, 0, 0, 0, 0)),
                  vec(HEAD_DIM), vec(HEAD_DIM), vec(HEAD_DIM), vec(HEAD_DIM), vec(hw)],
        out_specs=seq_spec(),
        out_shape=jax.ShapeDtypeStruct((batch * seq, heads * hw), BF16),
        scratch_shapes=[pltpu.VMEM((seq, hw), BF16), pltpu.VMEM((seq, hw), BF16)],
        compiler_params=_params("arbitrary", "arbitrary"),
        name="attn_prompt",
    )(q, k, v, bias, *lams, subln_w)


def _attn_sample_kernel(q_ref, ck_ref, cv_ref, kn_ref, vn_ref, b_ref, bn_ref, lq1, lk1, lq2, lk2, sw_ref, o_ref,
                        m_ref, l_ref, acc_ref, *, lam_init, n_tiles):
    kt_i = pl.program_id(1)

    @pl.when(kt_i == 0)
    def _():
        m_ref[...] = jnp.full(m_ref.shape, NEG, F32)
        l_ref[...] = jnp.zeros(l_ref.shape, F32)
        acc_ref[...] = jnp.zeros(acc_ref.shape, F32)

    def step(states, k2d, v2d, bias_of_map):
        kb = k2d.astype(BF16)
        vb = v2d.astype(BF16)
        out = []
        for mp in range(2):
            qm = q_ref[mp].astype(BF16)
            s = _dot_nt(qm, kb[:, mp * HEAD_DIM:(mp + 1) * HEAD_DIM]) * ATT_SCALE + bias_of_map(mp)
            out.append(_softmax_update(states[mp], s, vb))
        return out

    states = [(m_ref[mp], l_ref[mp], acc_ref[mp]) for mp in range(2)]
    t = b_ref.shape[3] // ck_ref.shape[1]
    n_sub = ck_ref.shape[0] // t
    for sub in range(n_sub):
        keys = slice(sub * t, (sub + 1) * t)
        bi = jnp.where(kt_i == n_tiles - 1, 1, 0) if sub == n_sub - 1 else 0
        states = step(states, ck_ref[keys].reshape(t * ck_ref.shape[1], ck_ref.shape[2]),
                      cv_ref[keys].reshape(t * cv_ref.shape[1], cv_ref.shape[2]), lambda mp, bi=bi: b_ref[bi, mp])

    @pl.when(kt_i < n_tiles - 1)
    def _():
        for mp in range(2):
            m_ref[mp], l_ref[mp], acc_ref[mp] = states[mp]

    @pl.when(kt_i == n_tiles - 1)
    def _():
        final = step(states, kn_ref[...], vn_ref[...], lambda mp: bn_ref[mp])
        _attn_finish(final, _lambda(lq1, lk1, lq2, lk2, lam_init), sw_ref[...], lam_init, o_ref)


def _head_expand(bias, heads):
    h, _, tq, tk = bias.shape
    eye = jnp.asarray(np.eye(heads, dtype=bool))
    full = jnp.where(eye[:, None, None, None, :], bias[..., None], NEG)
    return jnp.transpose(full, (1, 0, 2, 3, 4)).reshape(2, h * tq, tk * heads)


def _attn_sample(q, cache_k, cache_v, k_new, v_new, rel_bias, lams, subln_w, lam_init):
    db, past, heads, hw = cache_k.shape
    tq = q.shape[2] // heads
    t = min(ATT_TILE, past)
    assert past % t == 0 and t >= REL_MAX_DISTANCE + tq
    n_tiles = past // t
    q_pos = past + np.arange(tq)
    far = _bias_table(rel_bias, q_pos, np.arange(t))
    assert n_tiles == 1 or np.all(_bucket(np.arange(past - t)[None, :] - q_pos[:, None]) == NUM_BUCKETS // 2 - 1)
    near = _bias_table(rel_bias, q_pos, past - t + np.arange(t))
    bias = jnp.stack([_head_expand(far, heads), _head_expand(near, heads)])
    bias_new = _head_expand(_bias_table(rel_bias, q_pos, q_pos), heads)
    rq = heads * tq
    n_sub = math.gcd(2, n_tiles)
    tb = t * n_sub
    vec = lambda n: pl.BlockSpec((1, n), lambda b, j: (0, 0))
    return pl.pallas_call(
        functools.partial(_attn_sample_kernel, lam_init=lam_init, n_tiles=n_tiles // n_sub),
        grid=(db, n_tiles // n_sub),
        in_specs=[pl.BlockSpec((None, 2, rq, HEAD_DIM), lambda b, j: (b, 0, 0, 0)),
                  pl.BlockSpec((None, tb, heads, hw), lambda b, j: (b, j, 0, 0)),
                  pl.BlockSpec((None, tb, heads, hw), lambda b, j: (b, j, 0, 0)),
                  pl.BlockSpec((None, tq * heads, hw), lambda b, j: (b, 0, 0)),
                  pl.BlockSpec((None, tq * heads, hw), lambda b, j: (b, 0, 0)),
                  pl.BlockSpec((2, 2, rq, t * heads), lambda b, j: (0, 0, 0, 0), pipeline_mode=pl.Buffered(1)),
                  pl.BlockSpec((2, rq, tq * heads), lambda b, j: (0, 0, 0)),
                  vec(HEAD_DIM), vec(HEAD_DIM), vec(HEAD_DIM), vec(HEAD_DIM), vec(hw)],
        out_specs=pl.BlockSpec((None, rq, hw), lambda b, j: (b, 0, 0)),
        out_shape=jax.ShapeDtypeStruct((db, rq, hw), BF16),
        scratch_shapes=[pltpu.VMEM((2, rq, 1), F32), pltpu.VMEM((2, rq, 1), F32), pltpu.VMEM((2, rq, hw), F32)],
        compiler_params=_params("arbitrary", "arbitrary"),
        name="attn_sample",
    )(q, cache_k, cache_v, k_new, v_new, bias, bias_new, *lams, subln_w)


CONV_HIST = 32


def _conv_kernel(cur_ref, hist_ref, w_ref, b_ref, g_ref, beta_ref, o_ref, xs_ref, wb_ref, *, width, zero_first):
    nb, tt, c = cur_ref.shape
    rows = 8
    length = CONV_HIST + tt

    @pl.when(jnp.logical_and(pl.program_id(0) == 0, pl.program_id(1) == 0))
    def _():
        for tap in range(width):
            wb_ref[tap] = jnp.broadcast_to(w_ref[tap:tap + 1, :], (rows, c))

    hist = hist_ref[...]
    if zero_first:
        hist = jnp.where(pl.program_id(1) == 0, 0.0, hist)
    xs_ref[0, :, 0:CONV_HIST, :] = hist
    xs_ref[0, :, CONV_HIST:, :] = cur_ref[...]
    for s in range(1, rows):
        xs_ref[s, :, 0:length - rows, :] = xs_ref[0, :, s:s + length - rows, :]
    lead = CONV_HIST - (width - 1)
    for n in range(nb):
        for r0 in range(0, tt, rows):
            acc = jnp.zeros((rows, c), F32) + b_ref[...]
            for tap in range(width):
                s = (lead + tap) % rows
                a = lead + tap - s + r0
                acc = acc + xs_ref[s, n, a:a + rows, :] * wb_ref[tap]
            mu = jnp.mean(acc, axis=-1, keepdims=True)
            cen = acc - mu
            var = jnp.mean(cen * cen, axis=-1, keepdims=True)
            y = cen * lax.rsqrt(var + LN_EPS) * g_ref[...] + beta_ref[...]
            o_ref[n, r0:r0 + rows, :] = _silu(y).astype(o_ref.dtype)


def _conv(cur3, hist3, hist_index, nb, tt, w, b, g, beta, zero_first):
    nseq, seq, c = cur3.shape
    width = w.shape[0]
    assert width - 1 <= CONV_HIST and seq % tt == 0 and nseq % nb == 0 and tt % 8 == 0
    vec = lambda: pl.BlockSpec((1, c), lambda s, i: (0, 0))
    return pl.pallas_call(
        functools.partial(_conv_kernel, width=width, zero_first=zero_first),
        grid=(nseq // nb, seq // tt),
        in_specs=[pl.BlockSpec((nb, tt, c), lambda s, i: (s, i, 0)),
                  pl.BlockSpec((nb, CONV_HIST, c), hist_index),
                  pl.BlockSpec((width, c), lambda s, i: (0, 0)),
                  vec(), vec(), vec()],
        out_specs=pl.BlockSpec((nb, tt, c), lambda s, i: (s, i, 0)),
        out_shape=jax.ShapeDtypeStruct(cur3.shape, BF16),
        scratch_shapes=[pltpu.VMEM((8, nb, CONV_HIST + tt, c), F32), pltpu.VMEM((width, 8, c), F32)],
        compiler_params=_params("arbitrary", "arbitrary"),
        name="conv",
    )(cur3, hist3, w, b, g, beta)


def _first_argmax(v, iota, axis, size):
    m = jnp.max(v, axis=axis, keepdims=True)
    i = jnp.min(jnp.where(v == m, iota, size), axis=axis, keepdims=True)
    return m, i


def _router_kernel(h_ref, w_ref, bias_ref, idx_ref, gate_ref, rank_ref, cnt_ref, carry_ref):
    n_exp = w_ref.shape[0]
    tm = h_ref.shape[0]
    per_group = n_exp // N_EXPERT_GROUPS

    @pl.when(pl.program_id(0) == 0)
    def _():
        carry_ref[...] = jnp.zeros(carry_ref.shape, F32)

    w = w_ref[...]
    w1 = w.astype(BF16)
    r1 = w - w1.astype(F32)
    w2 = r1.astype(BF16)
    w3 = (r1 - w2.astype(F32)).astype(BF16)
    h = h_ref[...]
    logits = _dot_nt(w1, h) + _dot_nt(w2, h) + _dot_nt(w3, h)
    scores = jax.nn.sigmoid(logits)
    choice = scores + bias_ref[...]

    ch3 = choice.reshape(N_EXPERT_GROUPS, per_group, tm)
    io3 = lax.broadcasted_iota(I32, ch3.shape, 1)
    m1, i1 = _first_argmax(ch3, io3, 1, per_group)
    m2 = jnp.max(jnp.where(io3 == i1, -jnp.inf, ch3), axis=1, keepdims=True)
    grp = (m1 + m2).reshape(N_EXPERT_GROUPS, tm)
    gio = lax.broadcasted_iota(I32, grp.shape, 0)
    keep = jnp.zeros(grp.shape, jnp.bool_)
    for _ in range(TOPK_GROUPS):
        _, gi = _first_argmax(grp, gio, 0, N_EXPERT_GROUPS)
        hit = gio == gi
        keep = jnp.logical_or(keep, hit)
        grp = jnp.where(hit, -jnp.inf, grp)
    keep3 = jnp.broadcast_to(keep.reshape(N_EXPERT_GROUPS, 1, tm), ch3.shape)
    masked = jnp.where(keep3, ch3, -jnp.inf).reshape(n_exp, tm)

    eio = lax.broadcasted_iota(I32, masked.shape, 0)
    sel = jnp.zeros(masked.shape, jnp.bool_)
    picks, pick_scores = [], []
    for _ in range(TOP_K):
        _, ei = _first_argmax(masked, eio, 0, n_exp)
        hit = eio == ei
        picks.append((ei, hit))
        pick_scores.append(jnp.sum(jnp.where(hit, scores, 0.0), axis=0, keepdims=True))
        sel = jnp.logical_or(sel, hit)
        masked = jnp.where(hit, -jnp.inf, masked)
    total = pick_scores[0]
    for s in pick_scores[1:]:
        total = total + s

    sel_b = jnp.where(sel, 1.0, 0.0).astype(BF16)
    tri = (lax.broadcasted_iota(I32, (tm, tm), 0) < lax.broadcasted_iota(I32, (tm, tm), 1))
    rank = jnp.dot(sel_b, jnp.where(tri, 1.0, 0.0).astype(BF16), preferred_element_type=F32) + carry_ref[...]
    for kk, (ei, hit) in enumerate(picks):
        idx_ref[kk:kk + 1, :] = ei
        gate_ref[kk:kk + 1, :] = pick_scores[kk] / total * ROUTED_SCALE
        rank_ref[kk:kk + 1, :] = jnp.sum(jnp.where(hit, rank, 0.0), axis=0, keepdims=True).astype(I32)
    carry_ref[...] = carry_ref[...] + jnp.sum(jnp.where(sel, 1.0, 0.0), axis=1, keepdims=True)
    cnt_ref[...] = carry_ref[...].astype(I32)


def _router(h2b, w_router_t, bias_col):
    t, d = h2b.shape
    n_exp = w_router_t.shape[0]
    tm = math.gcd(512, t)
    assert t % tm == 0
    tok = lambda: pl.BlockSpec((TOP_K, tm), lambda i: (0, i))
    return pl.pallas_call(
        _router_kernel,
        grid=(t // tm,),
        in_specs=[pl.BlockSpec((tm, d), lambda i: (i, 0)),
                  pl.BlockSpec((n_exp, d), lambda i: (0, 0)),
                  pl.BlockSpec((n_exp, 1), lambda i: (0, 0))],
        out_specs=[tok(), tok(), tok(), pl.BlockSpec((n_exp, 1), lambda i: (0, 0))],
        out_shape=[jax.ShapeDtypeStruct((TOP_K, t), I32), jax.ShapeDtypeStruct((TOP_K, t), F32),
                   jax.ShapeDtypeStruct((TOP_K, t), I32), jax.ShapeDtypeStruct((n_exp, 1), I32)],
        scratch_shapes=[pltpu.VMEM((n_exp, 1), F32)],
        compiler_params=_params("arbitrary"),
        name="router",
    )(h2b, w_router_t, bias_col)


def _dispatch_kernel(rows_ref, h_ref, xs_ref, sem):
    n_tok = h_ref.shape[0]

    def body(t, carry):
        for kk in range(TOP_K):
            pltpu.make_async_copy(h_ref.at[pl.ds(t, 1)], xs_ref.at[pl.ds(rows_ref[kk, t], 1)], sem).start()
        return carry

    lax.fori_loop(0, n_tok, body, 0)
    for _ in range(TOP_K):
        pltpu.make_async_copy(h_ref, xs_ref.at[pl.ds(0, n_tok)], sem).wait()


def _dispatch(rows, h2p, n_rows):
    t, w = h2p.shape
    nt = math.gcd(DISPATCH_ROWS, t)
    assert t % nt == 0 and n_rows >= nt
    rows = jnp.transpose(rows.reshape(TOP_K, t // nt, nt), (1, 0, 2))
    return pl.pallas_call(
        _dispatch_kernel,
        grid=(t // nt,),
        in_specs=[pl.BlockSpec((None, TOP_K, nt), lambda i: (i, 0, 0), memory_space=pltpu.SMEM),
                  pl.BlockSpec((nt, w), lambda i: (i, 0))],
        out_specs=pl.BlockSpec(memory_space=pl.ANY),
        out_shape=jax.ShapeDtypeStruct((n_rows, w), h2p.dtype),
        scratch_shapes=[pltpu.SemaphoreType.DMA(())],
        compiler_params=pltpu.CompilerParams(dimension_semantics=("arbitrary",), has_side_effects=True),
        name="dispatch",
    )(rows, h2p)


def _seg_kernel(se_ref, sb0_ref, snb_ref, srows_ref, nseg_ref, xs_hbm, wg_hbm, wu_hbm, wd_hbm, ys_hbm,
                xbuf, wgf, wuf, wdf, wgb, wub, wdb, gacc, uacc, hid, ybuf, sem_x, sem_w, sem_y, ycnt):
    s = pl.program_id(0)
    nseg = nseg_ref[0]
    kc, phases = EXP_KCHUNKS, EXP_KCHUNKS + EXP_NCHUNKS
    r = EXP_ROWS
    dk, dn = wgf.shape[1], wdf.shape[2]
    hw = dk // 2

    def w_copies(e, ph):
        slot = ph & 1
        out = []
        if ph < kc:
            step = dk // W_DMA_PARTS
            for part in range(W_DMA_PARTS):
                dst = pl.ds(part * step, step)
                src = pl.ds(ph * dk + part * step, step)
                out.append(pltpu.make_async_copy(wg_hbm.at[e, src, :], wgf.at[slot, dst, :], sem_w.at[0, slot]))
                out.append(pltpu.make_async_copy(wu_hbm.at[e, src, :], wuf.at[slot, dst, :], sem_w.at[1, slot]))
            return out
        step = wdf.shape[1] // W_DMA_PARTS
        for part in range(W_DMA_PARTS):
            rows = pl.ds(part * step, step)
            out.append(pltpu.make_async_copy(wd_hbm.at[e, rows, pl.ds((ph - kc) * dn, dn)], wdf.at[slot, rows, :],
                                             sem_w.at[2, slot]))
        return out

    def for_x_blocks(seg, slot, action):
        for b in range(SEG_BLOCKS):
            @pl.when(b < snb_ref[seg])
            def _():
                row = pl.multiple_of((sb0_ref[seg] + b) * r, r)
                action(pltpu.make_async_copy(xs_hbm.at[pl.ds(row, r), :], xbuf.at[slot, pl.ds(b * r, r), :],
                                             sem_x.at[slot]))

    def y_copy(slot, row, chunk):
        return pltpu.make_async_copy(ybuf.at[slot], ys_hbm.at[pl.ds(row, r), pl.ds(chunk * (dn // 2), dn // 2)],
                                     sem_y.at[slot])

    @pl.when(s < nseg)
    def _():
        e, nb, par = se_ref[s], snb_ref[s], s & 1
        row_base = sb0_ref[s] * r
        valid = srows_ref[s]

        @pl.when(s == 0)
        def _():
            ycnt[0] = 0
            for_x_blocks(0, 0, lambda c: c.start())
            for ph in range(phases):
                for c in w_copies(e, ph):
                    c.start()

        @pl.when(s + 1 < nseg)
        def _():
            for_x_blocks(s + 1, 1 - par, lambda c: c.start())

        for_x_blocks(s, par, lambda c: c.wait())

        def refill(ph):
            @pl.when(s + 1 < nseg)
            def _():
                for c in w_copies(se_ref[s + 1], ph):
                    c.start()

        for ph in range(phases):
            for c in w_copies(e, ph):
                c.wait()
            slot = ph & 1
            if ph < kc:
                wgb[...] = wgf[slot].astype(BF16)
                wub[...] = wuf[slot].astype(BF16)
                refill(ph)

                def body(b, carry, ph=ph):
                    r0 = pl.multiple_of(b * r, r)
                    rows = r0 + lax.broadcasted_iota(I32, (r, 1), 0)
                    words = xbuf[par, pl.ds(r0, r), ph * hw:(ph + 1) * hw]
                    lo, hi = _unpack_halves(jnp.where(rows < valid, words, jnp.uint32(0)))
                    g = (jnp.dot(lo, wgb[:hw, :], preferred_element_type=F32)
                         + jnp.dot(hi, wgb[hw:, :], preferred_element_type=F32))
                    u = (jnp.dot(lo, wub[:hw, :], preferred_element_type=F32)
                         + jnp.dot(hi, wub[hw:, :], preferred_element_type=F32))
                    if ph > 0:
                        g = g + gacc[pl.ds(r0, r), :]
                        u = u + uacc[pl.ds(r0, r), :]
                    if ph < kc - 1:
                        gacc[pl.ds(r0, r), :] = g
                        uacc[pl.ds(r0, r), :] = u
                    else:
                        hid[pl.ds(r0, r), :] = (_silu(g) * u).astype(BF16)
                    return carry
            else:
                wdb[...] = wdf[slot].astype(BF16)
                refill(ph)

                def body(b, carry, ph=ph):
                    r0 = pl.multiple_of(b * r, r)
                    n = ycnt[0]
                    ys = n & 1

                    @pl.when(n >= 2)
                    def _():
                        y_copy(ys, 0, 0).wait()

                    ybuf[ys] = _pack_halves(jnp.dot(hid[pl.ds(r0, r), :], wdb[...], preferred_element_type=F32))
                    y_copy(ys, pl.multiple_of(row_base + r0, r), ph - kc).start()
                    ycnt[0] = n + 1
                    return carry

            lax.fori_loop(0, nb, body, 0)

        @pl.when(s == nseg - 1)
        def _():
            n = ycnt[0]
            for back in (1, 2):
                @pl.when(n >= back)
                def _():
                    y_copy((n - back) & 1, 0, 0).wait()


def _expert_segments(counts, n_blocks_max):
    n_exp = counts.shape[0]
    span = SEG_BLOCKS * EXP_ROWS
    nb_e = (counts + EXP_ROWS - 1) // EXP_ROWS
    blk_start_e = jnp.cumsum(nb_e) - nb_e
    nseg_e = (nb_e + SEG_BLOCKS - 1) // SEG_BLOCKS
    seg_end_e = jnp.cumsum(nseg_e)
    n_seg_max = (n_blocks_max + (SEG_BLOCKS - 1) * n_exp) // SEG_BLOCKS
    sid = jnp.arange(n_seg_max, dtype=I32)
    seg_e = jnp.minimum(jnp.sum(seg_end_e[None, :] <= sid[:, None], axis=1), n_exp - 1).astype(I32)
    mine = seg_e[:, None] == jnp.arange(n_exp, dtype=I32)[None, :]
    of_expert = lambda v: jnp.sum(jnp.where(mine, v[None, :], 0), axis=1)
    part = sid - of_expert(seg_end_e - nseg_e)
    seg_nb = jnp.clip(of_expert(nb_e) - part * SEG_BLOCKS, 0, SEG_BLOCKS)
    seg_blk0 = of_expert(blk_start_e) + part * SEG_BLOCKS
    seg_rows = jnp.clip(of_expert(counts) - part * span, 0, span)
    to_i32 = lambda v: v.astype(I32)
    return (seg_e, to_i32(seg_blk0), to_i32(seg_nb), to_i32(seg_rows), to_i32(seg_end_e[-1:]),
            to_i32(blk_start_e * EXP_ROWS))


def _experts_by_segment(seg_tables, xs, w_gate, w_up, w_down):
    n_exp, d, ff = w_gate.shape
    kc, nc = EXP_KCHUNKS, EXP_NCHUNKS
    dk, dn = d // kc, d // nc
    span = SEG_BLOCKS * EXP_ROWS
    assert kc <= 2 and nc <= 2
    hbm = lambda: pl.BlockSpec(memory_space=pl.ANY)
    grid_spec = pltpu.PrefetchScalarGridSpec(
        num_scalar_prefetch=5,
        grid=(seg_tables[0].shape[0],),
        in_specs=[hbm(), hbm(), hbm(), hbm()],
        out_specs=hbm(),
        scratch_shapes=[pltpu.VMEM((2, span, d // 2), U32),
                        pltpu.VMEM((2, dk, ff), F32), pltpu.VMEM((2, dk, ff), F32), pltpu.VMEM((2, ff, dn), F32),
                        pltpu.VMEM((dk, ff), BF16), pltpu.VMEM((dk, ff), BF16), pltpu.VMEM((ff, dn), BF16),
                        pltpu.VMEM((span, ff), F32), pltpu.VMEM((span, ff), F32), pltpu.VMEM((span, ff), BF16),
                        pltpu.VMEM((2, EXP_ROWS, dn // 2), U32),
                        pltpu.SemaphoreType.DMA((2,)), pltpu.SemaphoreType.DMA((3, 2)),
                        pltpu.SemaphoreType.DMA((2,)), pltpu.SMEM((1,), I32)],
    )
    return pl.pallas_call(
        _seg_kernel,
        grid_spec=grid_spec,
        out_shape=jax.ShapeDtypeStruct((xs.shape[0], d // 2), U32),
        compiler_params=_params("arbitrary"),
        name="experts",
    )(*seg_tables, xs, w_gate, w_up, w_down)


def _combine_kernel(rows_ref, next_rows_ref, gate_ref, hs_ref, wsd_ref, x1_ref, g2_ref, gpost_ref, ys_ref,
                    yp_ref, ysm_ref, buf_ref, wb_ref, sem, *, prompt_steps):
    i = pl.program_id(0)
    n_tok = gate_ref.shape[0]
    cur = i & 1

    def gather(table_ref, slot, action):
        def body(t, carry):
            for kk in range(TOP_K):
                action(pltpu.make_async_copy(ys_ref.at[pl.ds(table_ref[kk, t], 1)],
                                             buf_ref.at[slot, kk, pl.ds(t, 1)], sem.at[slot]))
            return carry
        lax.fori_loop(0, n_tok, body, 0)

    @pl.when(i == 0)
    def _():
        wb_ref[...] = wsd_ref[...].astype(BF16)
        gather(rows_ref, 0, lambda c: c.start())

    @pl.when(i + 1 < pl.num_programs(0))
    def _():
        gather(next_rows_ref, 1 - cur, lambda c: c.start())

    shared = jnp.dot(hs_ref[...], wb_ref[...], preferred_element_type=F32)
    for kk in range(TOP_K):
        pltpu.make_async_copy(ys_ref.at[pl.ds(0, n_tok)], buf_ref.at[cur, kk], sem.at[cur]).wait()
    gates = gate_ref[...]
    d = x1_ref.shape[1]
    dn = d // EXP_NCHUNKS
    hw = dn // 2
    pieces = []
    for c in range(EXP_NCHUNKS):
        lo = shared[:, c * dn:c * dn + hw]
        hi = shared[:, c * dn + hw:(c + 1) * dn]
        for kk in range(TOP_K):
            words = buf_ref[cur, kk, :, c * hw:(c + 1) * hw]
            g = gates[:, kk:kk + 1]
            lo = lo + lax.bitcast_convert_type(words << 16, F32) * g
            hi = hi + lax.bitcast_convert_type(words & jnp.uint32(0xFFFF0000), F32) * g
        pieces += [(c * dn, lo), (c * dn + hw, hi)]

    ssq = functools.reduce(jnp.add, [jnp.sum(f * f, axis=-1, keepdims=True) for _, f in pieces])
    inv = lax.rsqrt(ssq / d + RMS_EPS)
    groups, group_rows = g2_ref.shape[0], n_tok // g2_ref.shape[0]
    outs = []
    for col, f in pieces:
        cols = slice(col, col + hw)
        normed = (f * inv * gpost_ref[:, cols]).reshape(groups, group_rows, hw)
        outs.append((cols, x1_ref[:, cols] + (g2_ref[:, :, cols] * normed).reshape(n_tok, hw)))

    @pl.when(i < prompt_steps)
    def _():
        for cols, y in outs:
            yp_ref[:, cols] = y

    @pl.when(i >= prompt_steps)
    def _():
        for cols, y in outs:
            ysm_ref[:, cols] = y


def _combine(rows, gates_t, hid_sh, w_sh_down, ys, x1, g2_groups, gpost, n_prompt):
    t, ff = hid_sh.shape
    d = w_sh_down.shape[1]
    nt = math.gcd(math.gcd(COMBINE_ROWS, n_prompt), t - n_prompt)
    n_steps, prompt_steps = t // nt, n_prompt // nt
    groups = g2_groups.shape[0] * nt // t
    assert groups >= 1 and nt % groups == 0 and (nt // groups) % 8 == 0
    rows = jnp.transpose(rows.reshape(TOP_K, n_steps, nt), (1, 0, 2))
    rows_spec = lambda ahead: pl.BlockSpec((None, TOP_K, nt), lambda i: (jnp.minimum(i + ahead, n_steps - 1), 0, 0),
                                           memory_space=pltpu.SMEM)
    return pl.pallas_call(
        functools.partial(_combine_kernel, prompt_steps=prompt_steps),
        grid=(n_steps,),
        in_specs=[rows_spec(0), rows_spec(1),
                  pl.BlockSpec((nt, TOP_K), lambda i: (i, 0)),
                  pl.BlockSpec((nt, ff), lambda i: (i, 0)),
                  pl.BlockSpec((ff, d), lambda i: (0, 0), pipeline_mode=pl.Buffered(1)),
                  pl.BlockSpec((nt, d), lambda i: (i, 0)),
                  pl.BlockSpec((groups, 1, d), lambda i: (i, 0, 0)),
                  pl.BlockSpec((1, d), lambda i: (0, 0)),
                  pl.BlockSpec(memory_space=pl.ANY)],
        out_specs=[pl.BlockSpec((nt, d), lambda i: (jnp.minimum(i, prompt_steps - 1), 0)),
                   pl.BlockSpec((nt, d), lambda i: (jnp.maximum(i - prompt_steps, 0), 0))],
        out_shape=[jax.ShapeDtypeStruct((n_prompt, d), F32), jax.ShapeDtypeStruct((t - n_prompt, d), F32)],
        scratch_shapes=[pltpu.VMEM((2, TOP_K, nt, d // 2), U32), pltpu.VMEM((ff, d), BF16),
                        pltpu.SemaphoreType.DMA((2,))],
        compiler_params=_params("arbitrary"),
        name="combine",
    )(rows, rows, gates_t, hid_sh, w_sh_down, x1, g2_groups, gpost, ys)


def _layer(l, lam_init, x_prompt, x_sample, c_prompt, c_sample, cache_k, cache_v, state_conv, rel_bias, p):
    batch, seq, d = x_prompt.shape
    db, dseq, _ = x_sample.shape
    past, heads = cache_k.shape[1], cache_k.shape[2]
    hw = 2 * HEAD_DIM
    aw = heads * hw
    cc = p["conv_dw_w"].shape[1]
    width = p["conv_dw_w"].shape[0]
    assert p["w_in"].shape[1] == 3 * aw + 2 * cc and seq % dseq == 0
    tp, ts = batch * seq, db * dseq
    row = lambda v: v.reshape(1, -1)

    n_mod_rows = -(-(db + batch) // 16) * 16
    c_all = jnp.concatenate([c_sample, c_prompt, jnp.zeros((n_mod_rows - db - batch, d), F32)], axis=0)
    mod3 = _ada(c_all, p["w_ada"], row(p["b_ada"])).reshape(n_mod_rows, 1, N_MOD * d)

    gp = tp // dseq
    gps = seq // dseq
    nb_p = math.gcd(ROW_GROUPS, gps)
    nb_s = math.gcd(ROW_GROUPS, db)
    assert gps % nb_p == 0 and db % nb_s == 0 and gp % nb_s == 0
    plan_p = _RowPlan(gp, dseq, d, nb_p, 1, lambda i: db + (i * nb_p) // gps, 0)
    plan_s = _RowPlan(db, dseq, d, nb_s, nb_s, lambda i: i, gp // nb_s)
    xp3 = x_prompt.reshape(gp, dseq, d)

    hp = _prenorm(plan_p, xp3, row(p["g_pre_mix"]), mod3).reshape(tp, d)
    hs = _prenorm(plan_s, x_sample, row(p["g_pre_mix"]), mod3).reshape(ts, d)

    w_in = p["w_in"]
    proj = lambda h, off, nm: _mm([h], [(w_in, off)], aw, _identity, F32, MM_ROWS, 512, nm)
    qp, kp, vp = proj(hp, 0, "q_prompt"), proj(hp, aw, "k_prompt"), proj(hp, 2 * aw, "v_prompt")
    qs, ks, vs = proj(hs, 0, "q_sample"), proj(hs, aw, "k_sample"), proj(hs, 2 * aw, "v_sample")
    glu_cols = [(w_in, 3 * aw), (w_in, 3 * aw + cc)]
    glu_p = _mm([hp], glu_cols, cc, _glu, F32, MM_ROWS, 256, "glu_prompt")
    glu_s = _mm([hs], glu_cols, cc, _glu, F32, MM_ROWS, 256, "glu_sample")

    lams = [row(p[n]) for n in ("lambda_q1", "lambda_k1", "lambda_q2", "lambda_k2")]
    subln = row(p["subln_w"])
    attn_p = _attn_prompt(qp, kp, vp, rel_bias, lams, subln, batch, seq, heads, lam_init)

    new_k_s = ks.reshape(db, dseq, heads, hw)
    new_v_s = vs.reshape(db, dseq, heads, hw)
    q_s = jnp.transpose(qs.reshape(db, dseq, heads, 2, HEAD_DIM), (0, 3, 2, 1, 4)).reshape(db, 2, heads * dseq, HEAD_DIM)
    o_s = _attn_sample(q_s, cache_k, cache_v, new_k_s.reshape(db, dseq * heads, hw),
                       new_v_s.reshape(db, dseq * heads, hw), rel_bias, lams, subln, lam_init)
    attn_s = jnp.transpose(o_s.reshape(db, heads, dseq, hw), (0, 2, 1, 3)).reshape(ts, aw)

    conv_args = (p["conv_dw_w"], row(p["conv_dw_b"]), row(p["conv_ln_g"]), row(p["conv_ln_b"]))
    glu_p3 = glu_p.reshape(batch, seq, cc)
    tt = min(128, seq)
    per = tt // CONV_HIST
    conv_p = _conv(glu_p3, glu_p3, lambda s, i: (s, jnp.maximum(i * per - 1, 0), 0), 1, tt, *conv_args,
                   zero_first=True).reshape(tp, cc)
    glu_s3 = glu_s.reshape(db, dseq, cc)
    hist_s = jnp.concatenate([jnp.zeros((db, CONV_HIST - (width - 1), cc), F32), state_conv], axis=1)
    nb_c = math.gcd(2, db)
    conv_s = _conv(glu_s3, hist_s, lambda s, i: (s, 0, 0), nb_c, dseq, *conv_args, zero_first=False).reshape(ts, cc)
    new_conv_p = glu_p3[:, seq - (width - 1):]
    new_conv_s = jnp.concatenate([state_conv, glu_s3], axis=1)[:, -(width - 1):]

    mix_p = _mm([attn_p, conv_p], [(p["w_out"], 0)], d, _identity, F32, MM_ROWS, 512, "out_prompt")
    mix_s = _mm([attn_s, conv_s], [(p["w_out"], 0)], d, _identity, F32, MM_ROWS, 512, "out_sample")

    gt = gp + db
    gpost, gpre = row(p["g_post_mix"]), row(p["g_pre_ffn"])
    shared_bufs = _postmix(plan_p, xp3, mix_p.reshape(gp, dseq, d), gpost, gpre, mod3, gt)
    x1a, h2ba, h2pa = _postmix(plan_s, x_sample, mix_s.reshape(db, dseq, d), gpost, gpre, mod3, gt,
                               prev=shared_bufs)
    t = tp + ts
    x1 = x1a.reshape(t, d)
    h2b = h2ba.reshape(t, d)
    h2p = h2pa.reshape(t, d // 2)

    n_exp = p["w_router"].shape[1]
    idx, gates, rank, counts = _router(h2b, p["w_router"].T, p["b_router_corr"].reshape(n_exp, 1))
    n_blocks_max = (t * TOP_K) // EXP_ROWS + n_exp
    *seg_tables, row_start = _expert_segments(counts.reshape(n_exp), n_blocks_max)
    pick = idx[..., None] == jnp.arange(n_exp, dtype=I32)
    rows = jnp.sum(jnp.where(pick, row_start, 0), axis=-1) + rank

    xs = _dispatch(rows.astype(I32), h2p, n_blocks_max * EXP_ROWS)
    ys = _experts_by_segment(seg_tables, xs, p["w_exp_gate"], p["w_exp_up"], p["w_exp_down"])
    ff_sh = p["w_sh_gate"].shape[1]
    hid_sh = _mm([h2b], [(p["w_sh_gate"], 0), (p["w_sh_up"], 0)], ff_sh, _swiglu, BF16, MM_ROWS, 256, "shared_up")
    seq_of_group = np.concatenate([db + np.repeat(np.arange(batch), gps), np.arange(db)])
    g2_groups = jnp.take(mod3[:, :, (N_MOD - 1) * d:], jnp.asarray(seq_of_group, I32), axis=0)
    yp, ysmp = _combine(rows.astype(I32), gates.T, hid_sh, p["w_sh_down"], ys, x1, g2_groups,
                        row(p["g_post_ffn"]), tp)
    yp = yp.reshape(batch, seq, d)
    ysmp = ysmp.reshape(db, dseq, d)
    new_k_p = kp.reshape(batch, seq, heads, hw)
    new_v_p = vp.reshape(batch, seq, heads, hw)
    return yp, ysmp, new_k_p, new_v_p, new_conv_p, new_k_s, new_v_s, new_conv_s


def kernel(x_prompt, x_sample, c_prompt, c_sample, cache_k, cache_v, state_conv, rel_bias, w_ada, b_ada, g_pre_mix, g_post_mix, g_pre_ffn, g_post_ffn, w_in, lambda_q1, lambda_k1, lambda_q2, lambda_k2, subln_w, conv_dw_w, conv_dw_b, conv_ln_g, conv_ln_b, w_out, w_router, b_router_corr, w_exp_gate, w_exp_up, w_exp_down, w_sh_gate, w_sh_up, w_sh_down):
    weights = dict(w_ada=w_ada, b_ada=b_ada, g_pre_mix=g_pre_mix, g_post_mix=g_post_mix, g_pre_ffn=g_pre_ffn,
                   g_post_ffn=g_post_ffn, w_in=w_in, lambda_q1=lambda_q1, lambda_k1=lambda_k1,
                   lambda_q2=lambda_q2, lambda_k2=lambda_k2, subln_w=subln_w, conv_dw_w=conv_dw_w,
                   conv_dw_b=conv_dw_b, conv_ln_g=conv_ln_g, conv_ln_b=conv_ln_b, w_out=w_out,
                   w_router=w_router, b_router_corr=b_router_corr, w_exp_gate=w_exp_gate, w_exp_up=w_exp_up,
                   w_exp_down=w_exp_down, w_sh_gate=w_sh_gate, w_sh_up=w_sh_up, w_sh_down=w_sh_down)
    depth = w_in.shape[0]
    xp, xs = x_prompt, x_sample
    outs = [[] for _ in range(6)]
    for l in range(depth):
        p = {k: (v.reshape(v.shape[1:]) if depth == 1 else v[l]) for k, v in weights.items()}
        lam_init = 0.8 - 0.6 * math.exp(-0.3 * l)
        ck, cv, sc = ((a.reshape(a.shape[1:]) if depth == 1 else a[l]) for a in (cache_k, cache_v, state_conv))
        xp, xs, *state = _layer(l, lam_init, xp, xs, c_prompt, c_sample, ck, cv, sc, rel_bias, p)
        for acc, s in zip(outs, state):
            acc.append(s)
    return (xp, xs) + tuple(jnp.stack(o) for o in outs)
```

```python
import functools
import math

import numpy as np
import jax
import jax.numpy as jnp
from jax import lax
from jax.experimental import pallas as pl
from jax.experimental.pallas import tpu as pltpu

F32 = jnp.float32
BF16 = jnp.bfloat16
I32 = jnp.int32
U32 = jnp.uint32

CHUNK = 64
HEAD_DIM = 128
NUM_BUCKETS = 32
REL_MAX_DISTANCE = 128
TOP_K = 8
N_EXPERT_GROUPS = 8
TOPK_GROUPS = 4
ROUTED_SCALE = 2.5
RMS_EPS = 1e-6
SUBLN_EPS = 1e-5
LN_EPS = 1e-5
N_MOD = 6
NEG = -1e30
ATT_SCALE = HEAD_DIM ** -0.5

VMEM_LIMIT_BYTES = 56 * 1024 * 1024
ATT_TILE = 256
EXP_ROWS = 128
SEG_BLOCKS = 6
W_DMA_PARTS = 4
Y_STAGES = 4
EXP_KCHUNKS = 2
EXP_NCHUNKS = 2
COMBINE_ROWS = 128
DISPATCH_ROWS = 512
ROW_GROUPS = 8
MM_ROWS = 1024


def _params(*sem):
    return pltpu.CompilerParams(dimension_semantics=sem, vmem_limit_bytes=VMEM_LIMIT_BYTES)


def _silu(x):
    return x * jax.nn.sigmoid(x)


def _dot_nt(a, b):
    return lax.dot_general(a, b, (((1,), (1,)), ((), ())), preferred_element_type=F32)


def _ada_kernel(c_ref, w_ref, b_ref, o_ref):
    a = _silu(c_ref[...]).astype(BF16)
    o_ref[...] = jnp.dot(a, w_ref[...].astype(BF16), preferred_element_type=F32) + b_ref[...]


def _ada(c_all, w_ada, b_ada):
    rows, d = c_all.shape
    n = w_ada.shape[1]
    tn = min(512, n)
    return pl.pallas_call(
        _ada_kernel,
        grid=(n // tn,),
        in_specs=[pl.BlockSpec((rows, d), lambda j: (0, 0)),
                  pl.BlockSpec((d, tn), lambda j: (0, j)),
                  pl.BlockSpec((1, tn), lambda j: (0, j))],
        out_specs=pl.BlockSpec((rows, tn), lambda j: (0, j)),
        out_shape=jax.ShapeDtypeStruct((rows, n), F32),
        compiler_params=_params("arbitrary"),
        name="ada",
    )(c_all, w_ada, b_ada)


class _RowPlan:
    def __init__(self, n_groups, group_rows, d, nb, mod_nb, mod_index, out_block_offset=0):
        self.n_groups, self.group_rows, self.d, self.nb = n_groups, group_rows, d, nb
        self.mod_nb, self.mod_index, self.out_block_offset = mod_nb, mod_index, out_block_offset
        self.grid = (n_groups // nb,)

    def act(self, offset=0):
        return pl.BlockSpec((self.nb, self.group_rows, self.d), lambda i: (i + offset, 0, 0))

    def mod(self, chunk):
        return pl.BlockSpec((self.mod_nb, 1, self.d), lambda i: (self.mod_index(i), 0, chunk))

    def vec(self):
        return pl.BlockSpec((1, self.d), lambda i: (0, 0))


def _rms(x, g, eps):
    return x * lax.rsqrt(jnp.mean(x * x, axis=-1, keepdims=True) + eps) * g


def _prenorm_kernel(x_ref, g_ref, sh_ref, sc_ref, o_ref):
    y = _rms(x_ref[...], g_ref[...], RMS_EPS)
    o_ref[...] = (y * (1.0 + sc_ref[...]) + sh_ref[...]).astype(o_ref.dtype)


def _prenorm(plan, x3, g, mod3):
    return pl.pallas_call(
        _prenorm_kernel,
        grid=plan.grid,
        in_specs=[plan.act(), plan.vec(), plan.mod(0), plan.mod(1)],
        out_specs=plan.act(),
        out_shape=jax.ShapeDtypeStruct(x3.shape, BF16),
        compiler_params=_params("arbitrary"),
        name="prenorm",
    )(x3, g, mod3, mod3)


def _pack_halves(x):
    n = x.shape[-1] // 2
    lo = lax.bitcast_convert_type(x[..., :n].astype(BF16).astype(F32), U32) >> 16
    hi = lax.bitcast_convert_type(x[..., n:].astype(BF16).astype(F32), U32) & jnp.uint32(0xFFFF0000)
    return hi | lo


def _unpack_halves(w):
    lo = lax.bitcast_convert_type(w << 16, F32).astype(BF16)
    hi = lax.bitcast_convert_type(w & jnp.uint32(0xFFFF0000), F32).astype(BF16)
    return lo, hi


def _postmix_kernel(x_ref, mix_ref, gpost_ref, gpre_ref, g1_ref, sh2_ref, sc2_ref, *rest):
    x1_ref, h2b_ref, h2p_ref = rest[-3:]
    x1 = x_ref[...] + g1_ref[...] * _rms(mix_ref[...], gpost_ref[...], RMS_EPS)
    x1_ref[...] = x1
    h2 = _rms(x1, gpre_ref[...], RMS_EPS) * (1.0 + sc2_ref[...]) + sh2_ref[...]
    h2b_ref[...] = h2.astype(BF16)
    dk = h2.shape[-1] // EXP_KCHUNKS
    for c in range(EXP_KCHUNKS):
        h2p_ref[:, :, c * (dk // 2):(c + 1) * (dk // 2)] = _pack_halves(h2[:, :, c * dk:(c + 1) * dk])


def _postmix(plan, x3, mix3, gpost, gpre, mod3, total_groups, prev=None):
    off = plan.out_block_offset
    in_specs = [plan.act(), plan.act(), plan.vec(), plan.vec(), plan.mod(2), plan.mod(3), plan.mod(4)]
    args = [x3, mix3, gpost, gpre, mod3, mod3, mod3]
    aliases = {}
    if prev is not None:
        in_specs += [pl.BlockSpec(memory_space=pl.ANY)] * 3
        args += list(prev)
        aliases = {7: 0, 8: 1, 9: 2}
    shape_all = (total_groups, plan.group_rows, plan.d)
    shape_packed = (total_groups, plan.group_rows, plan.d // 2)
    packed_spec = pl.BlockSpec((plan.nb, plan.group_rows, plan.d // 2), lambda i: (i + off, 0, 0))
    return pl.pallas_call(
        _postmix_kernel,
        grid=plan.grid,
        in_specs=in_specs,
        out_specs=[plan.act(off), plan.act(off), packed_spec],
        out_shape=[jax.ShapeDtypeStruct(shape_all, F32),
                   jax.ShapeDtypeStruct(shape_all, BF16),
                   jax.ShapeDtypeStruct(shape_packed, U32)],
        input_output_aliases=aliases,
        compiler_params=_params("arbitrary"),
        name="postmix",
    )(*args)


def _mm_kernel(*refs, n_a, n_w, k_sizes, epilogue):
    a_refs = refs[:n_a]
    w_refs = refs[n_a:n_a + n_w]
    o_ref = refs[n_a + n_w]
    wb_refs = refs[n_a + n_w + 1:]

    @pl.when(pl.program_id(1) == 0)
    def _():
        for w_ref, wb_ref in zip(w_refs, wb_refs):
            wb_ref[...] = w_ref[...].astype(BF16)

    parts = []
    for wb_ref in wb_refs:
        acc, k0 = None, 0
        for a_ref, ka in zip(a_refs, k_sizes):
            d = jnp.dot(a_ref[...], wb_ref[k0:k0 + ka, :], preferred_element_type=F32)
            acc = d if acc is None else acc + d
            k0 += ka
        parts.append(acc)
    o_ref[...] = epilogue(*parts).astype(o_ref.dtype)


def _mm(a_list, w_cols, n_out, epilogue, out_dtype, tm, tn, name):
    m = a_list[0].shape[0]
    k = w_cols[0][0].shape[0]
    tm, tn = math.gcd(tm, m), math.gcd(tn, n_out)
    k_sizes = tuple(a.shape[1] for a in a_list)
    assert sum(k_sizes) == k and m % tm == 0 and n_out % tn == 0 and all(c % tn == 0 for _, c in w_cols)
    a_specs = [pl.BlockSpec((tm, ka), lambda j, i: (i, 0)) for ka in k_sizes]
    w_specs = [pl.BlockSpec((k, tn), functools.partial(lambda j, i, o: (0, o + j), o=c // tn)) for _, c in w_cols]
    kern = functools.partial(_mm_kernel, n_a=len(a_list), n_w=len(w_cols), k_sizes=k_sizes, epilogue=epilogue)
    return pl.pallas_call(
        kern,
        grid=(n_out // tn, m // tm),
        in_specs=a_specs + w_specs,
        out_specs=pl.BlockSpec((tm, tn), lambda j, i: (i, j)),
        out_shape=jax.ShapeDtypeStruct((m, n_out), out_dtype),
        scratch_shapes=[pltpu.VMEM((k, tn), BF16) for _ in w_cols],
        compiler_params=_params("arbitrary", "arbitrary"),
        name=name,
    )(*a_list, *[w for w, _ in w_cols])


def _identity(x):
    return x


def _glu(a, b):
    return a * jax.nn.sigmoid(b)


def _swiglu(a, b):
    return _silu(a) * b


def _bucket(rel):
    half = NUM_BUCKETS // 2
    max_exact = half // 2
    n = np.abs(rel)
    nf = np.maximum(n, 1).astype(np.float32)
    large = max_exact + (np.log(nf / np.float32(max_exact)) / np.float32(math.log(REL_MAX_DISTANCE / max_exact))
                         * np.float32(half - max_exact)).astype(np.int32)
    large = np.minimum(large, half - 1)
    return np.where(rel > 0, half, 0) + np.where(n < max_exact, n, large)


def _bias_table(rel_bias, q_pos, k_pos):
    rel = k_pos[None, :] - q_pos[:, None]
    visible = (k_pos // CHUNK)[None, :] <= (q_pos // CHUNK)[:, None]
    onehot = (jnp.asarray(_bucket(rel))[..., None] == jnp.arange(NUM_BUCKETS)).astype(F32)
    b = jnp.einsum("qkb,bhm->hmqk", onehot, rel_bias.astype(F32), precision=lax.Precision.HIGHEST)
    return jnp.where(jnp.asarray(visible)[None, None], b, NEG)


def _lambda(lq1, lk1, lq2, lk2, lam_init):
    return (jnp.exp(jnp.sum(lq1[...] * lk1[...], keepdims=True))
            - jnp.exp(jnp.sum(lq2[...] * lk2[...], keepdims=True)) + lam_init)


def _softmax_update(state, s, v):
    m_old, l_old, acc_old = state
    m_new = jnp.maximum(m_old, jnp.max(s, axis=-1, keepdims=True))
    alpha = jnp.exp(m_old - m_new)
    p = jnp.exp(s - m_new)
    return (m_new, alpha * l_old + jnp.sum(p, axis=-1, keepdims=True),
            alpha * acc_old + jnp.dot(p.astype(BF16), v, preferred_element_type=F32))


def _attn_finish(states, lam, sw, lam_init, o_ref):
    (_, l1, acc1), (_, l2, acc2) = states
    o = acc1 / l1 - lam * (acc2 / l2)
    o = o * lax.rsqrt(jnp.mean(o * o, axis=-1, keepdims=True) + SUBLN_EPS) * sw * (1.0 - lam_init)
    o_ref[...] = o.astype(o_ref.dtype)


def _attn_prompt_kernel(q_ref, k_ref, v_ref, b_ref, lq1, lk1, lq2, lk2, sw_ref, o_ref, kb_ref, vb_ref,
                        *, lam_init, nq):
    t = ATT_TILE
    kb_ref[...] = k_ref[...].astype(BF16)
    vb_ref[...] = v_ref[...].astype(BF16)
    lam = _lambda(lq1, lk1, lq2, lk2, lam_init)
    sw = sw_ref[...]
    for qi in range(nq):
        q = q_ref[qi * t:(qi + 1) * t, :]
        n_far = max(qi - 1, 0) * t
        exps, sums = [], []
        for mp in range(2):
            cols = slice(mp * HEAD_DIM, (mp + 1) * HEAD_DIM)
            s = _dot_nt(q[:, cols].astype(BF16), kb_ref[0:(qi + 1) * t, cols]) * ATT_SCALE
            pieces = []
            if n_far:
                pieces.append(s[:, :n_far] + b_ref[mp, 2, 0:1, 0:1])
            if qi >= 1:
                pieces.append(s[:, n_far:n_far + t] + b_ref[mp, 1])
            pieces.append(s[:, qi * t:(qi + 1) * t] + b_ref[mp, 0])
            m = functools.reduce(jnp.maximum, [jnp.max(p, axis=-1, keepdims=True) for p in pieces])
            es = [jnp.exp(p - m) for p in pieces]
            exps.append(es)
            sums.append(functools.reduce(jnp.add, [jnp.sum(e, axis=-1, keepdims=True) for e in es]))
        c1 = 1.0 / sums[0]
        c2 = lam / sums[1]
        o, col = None, 0
        for e1, e2 in zip(exps[0], exps[1]):
            a = (e1 * c1 - e2 * c2).astype(BF16)
            d = jnp.dot(a, vb_ref[col:col + a.shape[1], :], preferred_element_type=F32)
            o = d if o is None else o + d
            col += a.shape[1]
        o = o * lax.rsqrt(jnp.mean(o * o, axis=-1, keepdims=True) + SUBLN_EPS) * sw * (1.0 - lam_init)
        o_ref[qi * t:(qi + 1) * t, :] = o.astype(o_ref.dtype)


def _attn_prompt(q, k, v, rel_bias, lams, subln_w, batch, seq, heads, lam_init):
    t = ATT_TILE
    assert seq % t == 0 and t % CHUNK == 0
    pos = np.arange(t)
    tiles = [_bias_table(rel_bias, pos + dt * t, pos) for dt in range(3)]
    assert np.all(_bucket(np.arange(-3 * t + 1, -t)) == NUM_BUCKETS // 2 - 1)
    bias = jnp.stack(tiles, axis=2)
    hw = 2 * HEAD_DIM
    vec = lambda n: pl.BlockSpec((1, n), lambda b, h: (0, 0))
    seq_spec = lambda: pl.BlockSpec((seq, hw), lambda b, h: (b, h))
    return pl.pallas_call(
        functools.partial(_attn_prompt_kernel, lam_init=lam_init, nq=seq // t),
        grid=(batch, heads),
        in_specs=[seq_spec(), seq_spec(), seq_spec(),
                  pl.BlockSpec((None, 2, 3, t, t), lambda b, h: (h, 0, 0, 0, 0)),
                  vec(HEAD_DIM), vec(HEAD_DIM), vec(HEAD_DIM), vec(HEAD_DIM), vec(hw)],
        out_specs=seq_spec(),
        out_shape=jax.ShapeDtypeStruct((batch * seq, heads * hw), BF16),
        scratch_shapes=[pltpu.VMEM((seq, hw), BF16), pltpu.VMEM((seq, hw), BF16)],
        compiler_params=_params("arbitrary", "arbitrary"),
        name="attn_prompt",
    )(q, k, v, bias, *lams, subln_w)


def _attn_sample_kernel(q_ref, ck_ref, cv_ref, kn_ref, vn_ref, b_ref, bn_ref, lq1, lk1, lq2, lk2, sw_ref, o_ref,
                        m_ref, l_ref, acc_ref, *, lam_init, n_tiles):
    kt_i = pl.program_id(1)

    @pl.when(kt_i == 0)
    def _():
        m_ref[...] = jnp.full(m_ref.shape, NEG, F32)
        l_ref[...] = jnp.zeros(l_ref.shape, F32)
        acc_ref[...] = jnp.zeros(acc_ref.shape, F32)

    def step(states, k2d, v2d, bias_of_map):
        kb = k2d.astype(BF16)
        vb = v2d.astype(BF16)
        out = []
        for mp in range(2):
            qm = q_ref[mp].astype(BF16)
            s = _dot_nt(qm, kb[:, mp * HEAD_DIM:(mp + 1) * HEAD_DIM]) * ATT_SCALE + bias_of_map(mp)
            out.append(_softmax_update(states[mp], s, vb))
        return out

    states = [(m_ref[mp], l_ref[mp], acc_ref[mp]) for mp in range(2)]
    t = b_ref.shape[3] // ck_ref.shape[1]
    n_sub = ck_ref.shape[0] // t
    for sub in range(n_sub):
        keys = slice(sub * t, (sub + 1) * t)
        bi = jnp.where(kt_i == n_tiles - 1, 1, 0) if sub == n_sub - 1 else 0
        states = step(states, ck_ref[keys].reshape(t * ck_ref.shape[1], ck_ref.shape[2]),
                      cv_ref[keys].reshape(t * cv_ref.shape[1], cv_ref.shape[2]), lambda mp, bi=bi: b_ref[bi, mp])

    @pl.when(kt_i < n_tiles - 1)
    def _():
        for mp in range(2):
            m_ref[mp], l_ref[mp], acc_ref[mp] = states[mp]

    @pl.when(kt_i == n_tiles - 1)
    def _():
        final = step(states, kn_ref[...], vn_ref[...], lambda mp: bn_ref[mp])
        _attn_finish(final, _lambda(lq1, lk1, lq2, lk2, lam_init), sw_ref[...], lam_init, o_ref)


def _head_expand(bias, heads):
    h, _, tq, tk = bias.shape
    eye = jnp.asarray(np.eye(heads, dtype=bool))
    full = jnp.where(eye[:, None, None, None, :], bias[..., None], NEG)
    return jnp.transpose(full, (1, 0, 2, 3, 4)).reshape(2, h * tq, tk * heads)


def _attn_sample(q, cache_k, cache_v, k_new, v_new, rel_bias, lams, subln_w, lam_init):
    db, past, heads, hw = cache_k.shape
    tq = q.shape[2] // heads
    t = min(ATT_TILE, past)
    assert past % t == 0 and t >= REL_MAX_DISTANCE + tq
    n_tiles = past // t
    q_pos = past + np.arange(tq)
    far = _bias_table(rel_bias, q_pos, np.arange(t))
    assert n_tiles == 1 or np.all(_bucket(np.arange(past - t)[None, :] - q_pos[:, None]) == NUM_BUCKETS // 2 - 1)
    near = _bias_table(rel_bias, q_pos, past - t + np.arange(t))
    bias = jnp.stack([_head_expand(far, heads), _head_expand(near, heads)])
    bias_new = _head_expand(_bias_table(rel_bias, q_pos, q_pos), heads)
    rq = heads * tq
    n_sub = math.gcd(2, n_tiles)
    tb = t * n_sub
    vec = lambda n: pl.BlockSpec((1, n), lambda b, j: (0, 0))
    return pl.pallas_call(
        functools.partial(_attn_sample_kernel, lam_init=lam_init, n_tiles=n_tiles // n_sub),
        grid=(db, n_tiles // n_sub),
        in_specs=[pl.BlockSpec((None, 2, rq, HEAD_DIM), lambda b, j: (b, 0, 0, 0)),
                  pl.BlockSpec((None, tb, heads, hw), lambda b, j: (b, j, 0, 0)),
                  pl.BlockSpec((None, tb, heads, hw), lambda b, j: (b, j, 0, 0)),
                  pl.BlockSpec((None, tq * heads, hw), lambda b, j: (b, 0, 0)),
                  pl.BlockSpec((None, tq * heads, hw), lambda b, j: (b, 0, 0)),
                  pl.BlockSpec((2, 2, rq, t * heads), lambda b, j: (0, 0, 0, 0), pipeline_mode=pl.Buffered(1)),
                  pl.BlockSpec((2, rq, tq * heads), lambda b, j: (0, 0, 0)),
                  vec(HEAD_DIM), vec(HEAD_DIM), vec(HEAD_DIM), vec(HEAD_DIM), vec(hw)],
        out_specs=pl.BlockSpec((None, rq, hw), lambda b, j: (b, 0, 0)),
        out_shape=jax.ShapeDtypeStruct((db, rq, hw), BF16),
        scratch_shapes=[pltpu.VMEM((2, rq, 1), F32), pltpu.VMEM((2, rq, 1), F32), pltpu.VMEM((2, rq, hw), F32)],
        compiler_params=_params("arbitrary", "arbitrary"),
        name="attn_sample",
    )(q, cache_k, cache_v, k_new, v_new, bias, bias_new, *lams, subln_w)


CONV_HIST = 32


def _conv_kernel(cur_ref, hist_ref, w_ref, b_ref, g_ref, beta_ref, o_ref, xs_ref, wb_ref, *, width, zero_first):
    nb, tt, c = cur_ref.shape
    rows = 8
    length = CONV_HIST + tt

    @pl.when(jnp.logical_and(pl.program_id(0) == 0, pl.program_id(1) == 0))
    def _():
        for tap in range(width):
            wb_ref[tap] = jnp.broadcast_to(w_ref[tap:tap + 1, :], (rows, c))

    hist = hist_ref[...]
    if zero_first:
        hist = jnp.where(pl.program_id(1) == 0, 0.0, hist)
    xs_ref[0, :, 0:CONV_HIST, :] = hist
    xs_ref[0, :, CONV_HIST:, :] = cur_ref[...]
    for s in range(1, rows):
        xs_ref[s, :, 0:length - rows, :] = xs_ref[0, :, s:s + length - rows, :]
    lead = CONV_HIST - (width - 1)
    for n in range(nb):
        for r0 in range(0, tt, rows):
            acc = jnp.zeros((rows, c), F32) + b_ref[...]
            for tap in range(width):
                s = (lead + tap) % rows
                a = lead + tap - s + r0
                acc = acc + xs_ref[s, n, a:a + rows, :] * wb_ref[tap]
            mu = jnp.mean(acc, axis=-1, keepdims=True)
            cen = acc - mu
            var = jnp.mean(cen * cen, axis=-1, keepdims=True)
            y = cen * lax.rsqrt(var + LN_EPS) * g_ref[...] + beta_ref[...]
            o_ref[n, r0:r0 + rows, :] = _silu(y).astype(o_ref.dtype)


def _conv(cur3, hist3, hist_index, nb, tt, w, b, g, beta, zero_first):
    nseq, seq, c = cur3.shape
    width = w.shape[0]
    assert width - 1 <= CONV_HIST and seq % tt == 0 and nseq % nb == 0 and tt % 8 == 0
    vec = lambda: pl.BlockSpec((1, c), lambda s, i: (0, 0))
    return pl.pallas_call(
        functools.partial(_conv_kernel, width=width, zero_first=zero_first),
        grid=(nseq // nb, seq // tt),
        in_specs=[pl.BlockSpec((nb, tt, c), lambda s, i: (s, i, 0)),
                  pl.BlockSpec((nb, CONV_HIST, c), hist_index),
                  pl.BlockSpec((width, c), lambda s, i: (0, 0)),
                  vec(), vec(), vec()],
        out_specs=pl.BlockSpec((nb, tt, c), lambda s, i: (s, i, 0)),
        out_shape=jax.ShapeDtypeStruct(cur3.shape, BF16),
        scratch_shapes=[pltpu.VMEM((8, nb, CONV_HIST + tt, c), F32), pltpu.VMEM((width, 8, c), F32)],
        compiler_params=_params("arbitrary", "arbitrary"),
        name="conv",
    )(cur3, hist3, w, b, g, beta)


def _first_argmax(v, iota, axis, size):
    m = jnp.max(v, axis=axis, keepdims=True)
    i = jnp.min(jnp.where(v == m, iota, size), axis=axis, keepdims=True)
    return m, i


def _router_kernel(h_ref, w_ref, bias_ref, idx_ref, gate_ref, rank_ref, cnt_ref, carry_ref):
    n_exp = w_ref.shape[0]
    tm = h_ref.shape[0]
    per_group = n_exp // N_EXPERT_GROUPS

    @pl.when(pl.program_id(0) == 0)
    def _():
        carry_ref[...] = jnp.zeros(carry_ref.shape, F32)

    w = w_ref[...]
    w1 = w.astype(BF16)
    r1 = w - w1.astype(F32)
    w2 = r1.astype(BF16)
    w3 = (r1 - w2.astype(F32)).astype(BF16)
    h = h_ref[...]
    logits = _dot_nt(w1, h) + _dot_nt(w2, h) + _dot_nt(w3, h)
    scores = jax.nn.sigmoid(logits)
    choice = scores + bias_ref[...]

    ch3 = choice.reshape(N_EXPERT_GROUPS, per_group, tm)
    io3 = lax.broadcasted_iota(I32, ch3.shape, 1)
    m1, i1 = _first_argmax(ch3, io3, 1, per_group)
    m2 = jnp.max(jnp.where(io3 == i1, -jnp.inf, ch3), axis=1, keepdims=True)
    grp = (m1 + m2).reshape(N_EXPERT_GROUPS, tm)
    gio = lax.broadcasted_iota(I32, grp.shape, 0)
    keep = jnp.zeros(grp.shape, jnp.bool_)
    for _ in range(TOPK_GROUPS):
        _, gi = _first_argmax(grp, gio, 0, N_EXPERT_GROUPS)
        hit = gio == gi
        keep = jnp.logical_or(keep, hit)
        grp = jnp.where(hit, -jnp.inf, grp)
    keep3 = jnp.broadcast_to(keep.reshape(N_EXPERT_GROUPS, 1, tm), ch3.shape)
    masked = jnp.where(keep3, ch3, -jnp.inf).reshape(n_exp, tm)

    eio = lax.broadcasted_iota(I32, masked.shape, 0)
    sel = jnp.zeros(masked.shape, jnp.bool_)
    picks, pick_scores = [], []
    for _ in range(TOP_K):
        _, ei = _first_argmax(masked, eio, 0, n_exp)
        hit = eio == ei
        picks.append((ei, hit))
        pick_scores.append(jnp.sum(jnp.where(hit, scores, 0.0), axis=0, keepdims=True))
        sel = jnp.logical_or(sel, hit)
        masked = jnp.where(hit, -jnp.inf, masked)
    total = pick_scores[0]
    for s in pick_scores[1:]:
        total = total + s

    sel_b = jnp.where(sel, 1.0, 0.0).astype(BF16)
    tri = (lax.broadcasted_iota(I32, (tm, tm), 0) < lax.broadcasted_iota(I32, (tm, tm), 1))
    rank = jnp.dot(sel_b, jnp.where(tri, 1.0, 0.0).astype(BF16), preferred_element_type=F32) + carry_ref[...]
    for kk, (ei, hit) in enumerate(picks):
        idx_ref[kk:kk + 1, :] = ei
        gate_ref[kk:kk + 1, :] = pick_scores[kk] / total * ROUTED_SCALE
        rank_ref[kk:kk + 1, :] = jnp.sum(jnp.where(hit, rank, 0.0), axis=0, keepdims=True).astype(I32)
    carry_ref[...] = carry_ref[...] + jnp.sum(jnp.where(sel, 1.0, 0.0), axis=1, keepdims=True)
    cnt_ref[...] = carry_ref[...].astype(I32)


def _router(h2b, w_router_t, bias_col):
    t, d = h2b.shape
    n_exp = w_router_t.shape[0]
    tm = math.gcd(512, t)
    assert t % tm == 0
    tok = lambda: pl.BlockSpec((TOP_K, tm), lambda i: (0, i))
    return pl.pallas_call(
        _router_kernel,
        grid=(t // tm,),
        in_specs=[pl.BlockSpec((tm, d), lambda i: (i, 0)),
                  pl.BlockSpec((n_exp, d), lambda i: (0, 0)),
                  pl.BlockSpec((n_exp, 1), lambda i: (0, 0))],
        out_specs=[tok(), tok(), tok(), pl.BlockSpec((n_exp, 1), lambda i: (0, 0))],
        out_shape=[jax.ShapeDtypeStruct((TOP_K, t), I32), jax.ShapeDtypeStruct((TOP_K, t), F32),
                   jax.ShapeDtypeStruct((TOP_K, t), I32), jax.ShapeDtypeStruct((n_exp, 1), I32)],
        scratch_shapes=[pltpu.VMEM((n_exp, 1), F32)],
        compiler_params=_params("arbitrary"),
        name="router",
    )(h2b, w_router_t, bias_col)


def _dispatch_kernel(rows_ref, h_ref, xs_ref, sem):
    n_tok = h_ref.shape[0]

    def body(t, carry):
        for kk in range(TOP_K):
            pltpu.make_async_copy(h_ref.at[pl.ds(t, 1)], xs_ref.at[pl.ds(rows_ref[kk, t], 1)], sem).start()
        return carry

    lax.fori_loop(0, n_tok, body, 0)
    for _ in range(TOP_K):
        pltpu.make_async_copy(h_ref, xs_ref.at[pl.ds(0, n_tok)], sem).wait()


def _dispatch(rows, h2p, n_rows):
    t, w = h2p.shape
    nt = math.gcd(DISPATCH_ROWS, t)
    assert t % nt == 0 and n_rows >= nt
    rows = jnp.transpose(rows.reshape(TOP_K, t // nt, nt), (1, 0, 2))
    return pl.pallas_call(
        _dispatch_kernel,
        grid=(t // nt,),
        in_specs=[pl.BlockSpec((None, TOP_K, nt), lambda i: (i, 0, 0), memory_space=pltpu.SMEM),
                  pl.BlockSpec((nt, w), lambda i: (i, 0))],
        out_specs=pl.BlockSpec(memory_space=pl.ANY),
        out_shape=jax.ShapeDtypeStruct((n_rows, w), h2p.dtype),
        scratch_shapes=[pltpu.SemaphoreType.DMA(())],
        compiler_params=pltpu.CompilerParams(dimension_semantics=("arbitrary",), has_side_effects=True),
        name="dispatch",
    )(rows, h2p)


def _seg_kernel(se_ref, sb0_ref, snb_ref, srows_ref, nseg_ref, xs_hbm, wg_hbm, wu_hbm, wd_hbm, ys_hbm,
                xbuf, wgf, wuf, wdf, wgb, wub, wdb, gacc, uacc, hid, ybuf, sem_x, sem_w, sem_y, ycnt):
    s = pl.program_id(0)
    nseg = nseg_ref[0]
    kc, phases = EXP_KCHUNKS, EXP_KCHUNKS + EXP_NCHUNKS
    r = EXP_ROWS
    dk, dn = wgf.shape[1], wdf.shape[2]
    hw = dk // 2

    def w_copies(e, ph):
        slot = ph & 1
        out = []
        if ph < kc:
            step = dk // W_DMA_PARTS
            for part in range(W_DMA_PARTS):
                dst = pl.ds(part * step, step)
                src = pl.ds(ph * dk + part * step, step)
                out.append(pltpu.make_async_copy(wg_hbm.at[e, src, :], wgf.at[slot, dst, :], sem_w.at[0, slot]))
                out.append(pltpu.make_async_copy(wu_hbm.at[e, src, :], wuf.at[slot, dst, :], sem_w.at[1, slot]))
            return out
        step = wdf.shape[1] // W_DMA_PARTS
        for part in range(W_DMA_PARTS):
            rows = pl.ds(part * step, step)
            out.append(pltpu.make_async_copy(wd_hbm.at[e, rows, pl.ds((ph - kc) * dn, dn)], wdf.at[slot, rows, :],
                                             sem_w.at[2, slot]))
        return out

    def for_x_blocks(seg, slot, action):
        for b in range(SEG_BLOCKS):
            @pl.when(b < snb_ref[seg])
            def _():
                row = pl.multiple_of((sb0_ref[seg] + b) * r, r)
                action(pltpu.make_async_copy(xs_hbm.at[pl.ds(row, r), :], xbuf.at[slot, pl.ds(b * r, r), :],
                                             sem_x.at[slot]))

    def y_copy(slot, row, chunk):
        return pltpu.make_async_copy(ybuf.at[slot], ys_hbm.at[pl.ds(row, r), pl.ds(chunk * (dn // 2), dn // 2)],
                                     sem_y.at[slot])

    @pl.when(s < nseg)
    def _():
        e, nb, par = se_ref[s], snb_ref[s], s & 1
        row_base = sb0_ref[s] * r
        valid = srows_ref[s]

        @pl.when(s == 0)
        def _():
            ycnt[0] = 0
            for_x_blocks(0, 0, lambda c: c.start())
            for ph in range(phases):
                for c in w_copies(e, ph):
                    c.start()

        @pl.when(s + 1 < nseg)
        def _():
            for_x_blocks(s + 1, 1 - par, lambda c: c.start())

        for_x_blocks(s, par, lambda c: c.wait())

        def refill(ph):
            @pl.when(s + 1 < nseg)
            def _():
                for c in w_copies(se_ref[s + 1], ph):
                    c.start()

        for ph in range(phases):
            for c in w_copies(e, ph):
                c.wait()
            slot = ph & 1
            if ph < kc:
                wgb[...] = wgf[slot].astype(BF16)
                wub[...] = wuf[slot].astype(BF16)
                refill(ph)

                def body(b, carry, ph=ph):
                    r0 = pl.multiple_of(b * r, r)
                    rows = r0 + lax.broadcasted_iota(I32, (r, 1), 0)
                    words = xbuf[par, pl.ds(r0, r), ph * hw:(ph + 1) * hw]
                    lo, hi = _unpack_halves(jnp.where(rows < valid, words, jnp.uint32(0)))
                    g = (jnp.dot(lo, wgb[:hw, :], preferred_element_type=F32)
                         + jnp.dot(hi, wgb[hw:, :], preferred_element_type=F32))
                    u = (jnp.dot(lo, wub[:hw, :], preferred_element_type=F32)
                         + jnp.dot(hi, wub[hw:, :], preferred_element_type=F32))
                    if ph > 0:
                        g = g + gacc[pl.ds(r0, r), :]
                        u = u + uacc[pl.ds(r0, r), :]
                    if ph < kc - 1:
                        gacc[pl.ds(r0, r), :] = g
                        uacc[pl.ds(r0, r), :] = u
                    else:
                        hid[pl.ds(r0, r), :] = (_silu(g) * u).astype(BF16)
                    return carry
            else:
                wdb[...] = wdf[slot].astype(BF16)
                refill(ph)

                def body(b, carry, ph=ph):
                    r0 = pl.multiple_of(b * r, r)
                    n = ycnt[0]
                    ys = n % Y_STAGES

                    @pl.when(n >= Y_STAGES)
                    def _():
                        y_copy(ys, 0, 0).wait()

                    ybuf[ys] = _pack_halves(jnp.dot(hid[pl.ds(r0, r), :], wdb[...], preferred_element_type=F32))
                    y_copy(ys, pl.multiple_of(row_base + r0, r), ph - kc).start()
                    ycnt[0] = n + 1
                    return carry

            lax.fori_loop(0, nb, body, 0)

        @pl.when(s == nseg - 1)
        def _():
            n = ycnt[0]
            for back in range(1, Y_STAGES + 1):
                @pl.when(n >= back)
                def _():
                    y_copy((n - back) % Y_STAGES, 0, 0).wait()


def _expert_segments(counts, n_blocks_max):
    n_exp = counts.shape[0]
    span = SEG_BLOCKS * EXP_ROWS
    nb_e = (counts + EXP_ROWS - 1) // EXP_ROWS
    blk_start_e = jnp.cumsum(nb_e) - nb_e
    nseg_e = (nb_e + SEG_BLOCKS - 1) // SEG_BLOCKS
    seg_end_e = jnp.cumsum(nseg_e)
    n_seg_max = (n_blocks_max + (SEG_BLOCKS - 1) * n_exp) // SEG_BLOCKS
    sid = jnp.arange(n_seg_max, dtype=I32)
    seg_e = jnp.minimum(jnp.sum(seg_end_e[None, :] <= sid[:, None], axis=1), n_exp - 1).astype(I32)
    mine = seg_e[:, None] == jnp.arange(n_exp, dtype=I32)[None, :]
    of_expert = lambda v: jnp.sum(jnp.where(mine, v[None, :], 0), axis=1)
    part = sid - of_expert(seg_end_e - nseg_e)
    seg_nb = jnp.clip(of_expert(nb_e) - part * SEG_BLOCKS, 0, SEG_BLOCKS)
    seg_blk0 = of_expert(blk_start_e) + part * SEG_BLOCKS
    seg_rows = jnp.clip(of_expert(counts) - part * span, 0, span)
    to_i32 = lambda v: v.astype(I32)
    return (seg_e, to_i32(seg_blk0), to_i32(seg_nb), to_i32(seg_rows), to_i32(seg_end_e[-1:]),
            to_i32(blk_start_e * EXP_ROWS))


def _experts_by_segment(seg_tables, xs, w_gate, w_up, w_down):
    n_exp, d, ff = w_gate.shape
    kc, nc = EXP_KCHUNKS, EXP_NCHUNKS
    dk, dn = d // kc, d // nc
    span = SEG_BLOCKS * EXP_ROWS
    assert kc <= 2 and nc <= 2
    hbm = lambda: pl.BlockSpec(memory_space=pl.ANY)
    grid_spec = pltpu.PrefetchScalarGridSpec(
        num_scalar_prefetch=5,
        grid=(seg_tables[0].shape[0],),
        in_specs=[hbm(), hbm(), hbm(), hbm()],
        out_specs=hbm(),
        scratch_shapes=[pltpu.VMEM((2, span, d // 2), U32),
                        pltpu.VMEM((2, dk, ff), F32), pltpu.VMEM((2, dk, ff), F32), pltpu.VMEM((2, ff, dn), F32),
                        pltpu.VMEM((dk, ff), BF16), pltpu.VMEM((dk, ff), BF16), pltpu.VMEM((ff, dn), BF16),
                        pltpu.VMEM((span, ff), F32), pltpu.VMEM((span, ff), F32), pltpu.VMEM((span, ff), BF16),
                        pltpu.VMEM((Y_STAGES, EXP_ROWS, dn // 2), U32),
                        pltpu.SemaphoreType.DMA((2,)), pltpu.SemaphoreType.DMA((3, 2)),
                        pltpu.SemaphoreType.DMA((Y_STAGES,)), pltpu.SMEM((1,), I32)],
    )
    return pl.pallas_call(
        _seg_kernel,
        grid_spec=grid_spec,
        out_shape=jax.ShapeDtypeStruct((xs.shape[0], d // 2), U32),
        compiler_params=_params("arbitrary"),
        name="experts",
    )(*seg_tables, xs, w_gate, w_up, w_down)


def _combine_kernel(rows_ref, next_rows_ref, gate_ref, hs_ref, wsd_ref, x1_ref, g2_ref, gpost_ref, ys_ref,
                    yp_ref, ysm_ref, buf_ref, wb_ref, sem, *, prompt_steps):
    i = pl.program_id(0)
    n_tok = gate_ref.shape[0]
    cur = i & 1

    def gather(table_ref, slot, action):
        def body(t, carry):
            for kk in range(TOP_K):
                action(pltpu.make_async_copy(ys_ref.at[pl.ds(table_ref[kk, t], 1)],
                                             buf_ref.at[slot, kk, pl.ds(t, 1)], sem.at[slot]))
            return carry
        lax.fori_loop(0, n_tok, body, 0)

    @pl.when(i == 0)
    def _():
        wb_ref[...] = wsd_ref[...].astype(BF16)
        gather(rows_ref, 0, lambda c: c.start())

    @pl.when(i + 1 < pl.num_programs(0))
    def _():
        gather(next_rows_ref, 1 - cur, lambda c: c.start())

    shared = jnp.dot(hs_ref[...], wb_ref[...], preferred_element_type=F32)
    for kk in range(TOP_K):
        pltpu.make_async_copy(ys_ref.at[pl.ds(0, n_tok)], buf_ref.at[cur, kk], sem.at[cur]).wait()
    gates = gate_ref[...]
    d = x1_ref.shape[1]
    dn = d // EXP_NCHUNKS
    hw = dn // 2
    pieces = []
    for c in range(EXP_NCHUNKS):
        lo = shared[:, c * dn:c * dn + hw]
        hi = shared[:, c * dn + hw:(c + 1) * dn]
        for kk in range(TOP_K):
            words = buf_ref[cur, kk, :, c * hw:(c + 1) * hw]
            g = gates[:, kk:kk + 1]
            lo = lo + lax.bitcast_convert_type(words << 16, F32) * g
            hi = hi + lax.bitcast_convert_type(words & jnp.uint32(0xFFFF0000), F32) * g
        pieces += [(c * dn, lo), (c * dn + hw, hi)]

    ssq = functools.reduce(jnp.add, [jnp.sum(f * f, axis=-1, keepdims=True) for _, f in pieces])
    inv = lax.rsqrt(ssq / d + RMS_EPS)
    groups, group_rows = g2_ref.shape[0], n_tok // g2_ref.shape[0]
    outs = []
    for col, f in pieces:
        cols = slice(col, col + hw)
        normed = (f * inv * gpost_ref[:, cols]).reshape(groups, group_rows, hw)
        outs.append((cols, x1_ref[:, cols] + (g2_ref[:, :, cols] * normed).reshape(n_tok, hw)))

    @pl.when(i < prompt_steps)
    def _():
        for cols, y in outs:
            yp_ref[:, cols] = y

    @pl.when(i >= prompt_steps)
    def _():
        for cols, y in outs:
            ysm_ref[:, cols] = y


def _combine(rows, gates_t, hid_sh, w_sh_down, ys, x1, g2_groups, gpost, n_prompt):
    t, ff = hid_sh.shape
    d = w_sh_down.shape[1]
    nt = math.gcd(math.gcd(COMBINE_ROWS, n_prompt), t - n_prompt)
    n_steps, prompt_steps = t // nt, n_prompt // nt
    groups = g2_groups.shape[0] * nt // t
    assert groups >= 1 and nt % groups == 0 and (nt // groups) % 8 == 0
    rows = jnp.transpose(rows.reshape(TOP_K, n_steps, nt), (1, 0, 2))
    rows_spec = lambda ahead: pl.BlockSpec((None, TOP_K, nt), lambda i: (jnp.minimum(i + ahead, n_steps - 1), 0, 0),
                                           memory_space=pltpu.SMEM)
    return pl.pallas_call(
        functools.partial(_combine_kernel, prompt_steps=prompt_steps),
        grid=(n_steps,),
        in_specs=[rows_spec(0), rows_spec(1),
                  pl.BlockSpec((nt, TOP_K), lambda i: (i, 0)),
                  pl.BlockSpec((nt, ff), lambda i: (i, 0)),
                  pl.BlockSpec((ff, d), lambda i: (0, 0), pipeline_mode=pl.Buffered(1)),
                  pl.BlockSpec((nt, d), lambda i: (i, 0)),
                  pl.BlockSpec((groups, 1, d), lambda i: (i, 0, 0)),
                  pl.BlockSpec((1, d), lambda i: (0, 0)),
                  pl.BlockSpec(memory_space=pl.ANY)],
        out_specs=[pl.BlockSpec((nt, d), lambda i: (jnp.minimum(i, prompt_steps - 1), 0)),
                   pl.BlockSpec((nt, d), lambda i: (jnp.maximum(i - prompt_steps, 0), 0))],
        out_shape=[jax.ShapeDtypeStruct((n_prompt, d), F32), jax.ShapeDtypeStruct((t - n_prompt, d), F32)],
        scratch_shapes=[pltpu.VMEM((2, TOP_K, nt, d // 2), U32), pltpu.VMEM((ff, d), BF16),
                        pltpu.SemaphoreType.DMA((2,))],
        compiler_params=_params("arbitrary"),
        name="combine",
    )(rows, rows, gates_t, hid_sh, w_sh_down, x1, g2_groups, gpost, ys)


def _layer(l, lam_init, x_prompt, x_sample, c_prompt, c_sample, cache_k, cache_v, state_conv, rel_bias, p):
    batch, seq, d = x_prompt.shape
    db, dseq, _ = x_sample.shape
    past, heads = cache_k.shape[1], cache_k.shape[2]
    hw = 2 * HEAD_DIM
    aw = heads * hw
    cc = p["conv_dw_w"].shape[1]
    width = p["conv_dw_w"].shape[0]
    assert p["w_in"].shape[1] == 3 * aw + 2 * cc and seq % dseq == 0
    tp, ts = batch * seq, db * dseq
    row = lambda v: v.reshape(1, -1)

    n_mod_rows = -(-(db + batch) // 16) * 16
    c_all = jnp.concatenate([c_sample, c_prompt, jnp.zeros((n_mod_rows - db - batch, d), F32)], axis=0)
    mod3 = _ada(c_all, p["w_ada"], row(p["b_ada"])).reshape(n_mod_rows, 1, N_MOD * d)

    gp = tp // dseq
    gps = seq // dseq
    nb_p = math.gcd(ROW_GROUPS, gps)
    nb_s = math.gcd(ROW_GROUPS, db)
    assert gps % nb_p == 0 and db % nb_s == 0 and gp % nb_s == 0
    plan_p = _RowPlan(gp, dseq, d, nb_p, 1, lambda i: db + (i * nb_p) // gps, 0)
    plan_s = _RowPlan(db, dseq, d, nb_s, nb_s, lambda i: i, gp // nb_s)
    xp3 = x_prompt.reshape(gp, dseq, d)

    hp = _prenorm(plan_p, xp3, row(p["g_pre_mix"]), mod3).reshape(tp, d)
    hs = _prenorm(plan_s, x_sample, row(p["g_pre_mix"]), mod3).reshape(ts, d)

    w_in = p["w_in"]
    proj = lambda h, off, nm: _mm([h], [(w_in, off)], aw, _identity, F32, MM_ROWS, 512, nm)
    qp, kp, vp = proj(hp, 0, "q_prompt"), proj(hp, aw, "k_prompt"), proj(hp, 2 * aw, "v_prompt")
    qs, ks, vs = proj(hs, 0, "q_sample"), proj(hs, aw, "k_sample"), proj(hs, 2 * aw, "v_sample")
    glu_cols = [(w_in, 3 * aw), (w_in, 3 * aw + cc)]
    glu_p = _mm([hp], glu_cols, cc, _glu, F32, MM_ROWS, 256, "glu_prompt")
    glu_s = _mm([hs], glu_cols, cc, _glu, F32, MM_ROWS, 256, "glu_sample")

    lams = [row(p[n]) for n in ("lambda_q1", "lambda_k1", "lambda_q2", "lambda_k2")]
    subln = row(p["subln_w"])
    attn_p = _attn_prompt(qp, kp, vp, rel_bias, lams, subln, batch, seq, heads, lam_init)

    new_k_s = ks.reshape(db, dseq, heads, hw)
    new_v_s = vs.reshape(db, dseq, heads, hw)
    q_s = jnp.transpose(qs.reshape(db, dseq, heads, 2, HEAD_DIM), (0, 3, 2, 1, 4)).reshape(db, 2, heads * dseq, HEAD_DIM)
    o_s = _attn_sample(q_s, cache_k, cache_v, new_k_s.reshape(db, dseq * heads, hw),
                       new_v_s.reshape(db, dseq * heads, hw), rel_bias, lams, subln, lam_init)
    attn_s = jnp.transpose(o_s.reshape(db, heads, dseq, hw), (0, 2, 1, 3)).reshape(ts, aw)

    conv_args = (p["conv_dw_w"], row(p["conv_dw_b"]), row(p["conv_ln_g"]), row(p["conv_ln_b"]))
    glu_p3 = glu_p.reshape(batch, seq, cc)
    tt = min(128, seq)
    per = tt // CONV_HIST
    conv_p = _conv(glu_p3, glu_p3, lambda s, i: (s, jnp.maximum(i * per - 1, 0), 0), 1, tt, *conv_args,
                   zero_first=True).reshape(tp, cc)
    glu_s3 = glu_s.reshape(db, dseq, cc)
    hist_s = jnp.concatenate([jnp.zeros((db, CONV_HIST - (width - 1), cc), F32), state_conv], axis=1)
    nb_c = math.gcd(2, db)
    conv_s = _conv(glu_s3, hist_s, lambda s, i: (s, 0, 0), nb_c, dseq, *conv_args, zero_first=False).reshape(ts, cc)
    new_conv_p = glu_p3[:, seq - (width - 1):]
    new_conv_s = jnp.concatenate([state_conv, glu_s3], axis=1)[:, -(width - 1):]

    mix_p = _mm([attn_p, conv_p], [(p["w_out"], 0)], d, _identity, F32, MM_ROWS, 512, "out_prompt")
    mix_s = _mm([attn_s, conv_s], [(p["w_out"], 0)], d, _identity, F32, MM_ROWS, 512, "out_sample")

    gt = gp + db
    gpost, gpre = row(p["g_post_mix"]), row(p["g_pre_ffn"])
    shared_bufs = _postmix(plan_p, xp3, mix_p.reshape(gp, dseq, d), gpost, gpre, mod3, gt)
    x1a, h2ba, h2pa = _postmix(plan_s, x_sample, mix_s.reshape(db, dseq, d), gpost, gpre, mod3, gt,
                               prev=shared_bufs)
    t = tp + ts
    x1 = x1a.reshape(t, d)
    h2b = h2ba.reshape(t, d)
    h2p = h2pa.reshape(t, d // 2)

    n_exp = p["w_router"].shape[1]
    idx, gates, rank, counts = _router(h2b, p["w_router"].T, p["b_router_corr"].reshape(n_exp, 1))
    n_blocks_max = (t * TOP_K) // EXP_ROWS + n_exp
    *seg_tables, row_start = _expert_segments(counts.reshape(n_exp), n_blocks_max)
    pick = idx[..., None] == jnp.arange(n_exp, dtype=I32)
    rows = jnp.sum(jnp.where(pick, row_start, 0), axis=-1) + rank

    xs = _dispatch(rows.astype(I32), h2p, n_blocks_max * EXP_ROWS)
    ys = _experts_by_segment(seg_tables, xs, p["w_exp_gate"], p["w_exp_up"], p["w_exp_down"])
    ff_sh = p["w_sh_gate"].shape[1]
    hid_sh = _mm([h2b], [(p["w_sh_gate"], 0), (p["w_sh_up"], 0)], ff_sh, _swiglu, BF16, MM_ROWS, 256, "shared_up")
    seq_of_group = np.concatenate([db + np.repeat(np.arange(batch), gps), np.arange(db)])
    g2_groups = jnp.take(mod3[:, :, (N_MOD - 1) * d:], jnp.asarray(seq_of_group, I32), axis=0)
    yp, ysmp = _combine(rows.astype(I32), gates.T, hid_sh, p["w_sh_down"], ys, x1, g2_groups,
                        row(p["g_post_ffn"]), tp)
    yp = yp.reshape(batch, seq, d)
    ysmp = ysmp.reshape(db, dseq, d)
    new_k_p = kp.reshape(batch, seq, heads, hw)
    new_v_p = vp.reshape(batch, seq, heads, hw)
    return yp, ysmp, new_k_p, new_v_p, new_conv_p, new_k_s, new_v_s, new_conv_s


def kernel(x_prompt, x_sample, c_prompt, c_sample, cache_k, cache_v, state_conv, rel_bias, w_ada, b_ada, g_pre_mix, g_post_mix, g_pre_ffn, g_post_ffn, w_in, lambda_q1, lambda_k1, lambda_q2, lambda_k2, subln_w, conv_dw_w, conv_dw_b, conv_ln_g, conv_ln_b, w_out, w_router, b_router_corr, w_exp_gate, w_exp_up, w_exp_down, w_sh_gate, w_sh_up, w_sh_down):
    weights = dict(w_ada=w_ada, b_ada=b_ada, g_pre_mix=g_pre_mix, g_post_mix=g_post_mix, g_pre_ffn=g_pre_ffn,
                   g_post_ffn=g_post_ffn, w_in=w_in, lambda_q1=lambda_q1, lambda_k1=lambda_k1,
                   lambda_q2=lambda_q2, lambda_k2=lambda_k2, subln_w=subln_w, conv_dw_w=conv_dw_w,
                   conv_dw_b=conv_dw_b, conv_ln_g=conv_ln_g, conv_ln_b=conv_ln_b, w_out=w_out,
                   w_router=w_router, b_router_corr=b_router_corr, w_exp_gate=w_exp_gate, w_exp_up=w_exp_up,
                   w_exp_down=w_exp_down, w_sh_gate=w_sh_gate, w_sh_up=w_sh_up, w_sh_down=w_sh_down)
    depth = w_in.shape[0]
    xp, xs = x_prompt, x_sample
    outs = [[] for _ in range(6)]
    for l in range(depth):
        p = {k: (v.reshape(v.shape[1:]) if depth == 1 else v[l]) for k, v in weights.items()}
        lam_init = 0.8 - 0.6 * math.exp(-0.3 * l)
        ck, cv, sc = ((a.reshape(a.shape[1:]) if depth == 1 else a[l]) for a in (cache_k, cache_v, state_conv))
        xp, xs, *state = _layer(l, lam_init, xp, xs, c_prompt, c_sample, ck, cv, sc, rel_bias, p)
        for acc, s in zip(outs, state):
            acc.append(s)
    return (xp, xs) + tuple(jnp.stack(o) for o in outs)
```
